```python
import math
import jax, jax.numpy as jnp
from jax import lax
import numpy as np

D_MODEL = 4096
BATCH = 4
SEQ = 4096
DEPTH = 1

GRID_W = 64
CTX_LEN = 256
MIX_W = D_MODEL
HY_W = MIX_W // 2
ML_W = MIX_W - HY_W
ML_HEADS = 4
ML_DV = ML_W // ML_HEADS
ML_DQK = ML_DV // 2
ML_QK = ML_HEADS * ML_DQK
ML_CHUNK = 64
HY_ORDER = 2
HY_BANDS = 16
HY_EMB = 1 + 2 * HY_BANDS
HY_FH = 64
HY_DECAY_TARGET = 1e-2
HY_FAST_PCT = 0.3
HY_SLOW_PCT = 1.5
HY_MAX_DECAY = math.log(HY_DECAY_TARGET) / HY_FAST_PCT
HY_MIN_DECAY = math.log(HY_DECAY_TARGET) / HY_SLOW_PCT
FFN_DIM = 256 * ((8 * D_MODEL // 3 + 255) // 256)
N_GATES = 2 * 2 * ML_HEADS
P_HY = 3 * HY_W
P_STATE0 = P_HY + ML_QK + ML_W
P_TOTAL = P_STATE0 + ML_QK + ML_W + N_GATES
ALPHA = (2.0 * DEPTH) ** 0.25
BETA = (8.0 * DEPTH) ** -0.25
LN_EPS = 1e-5

kernel_name = "hyena_mlstm_macaron_deepnorm_prefix"


def layer_norm(h, g=None, b=None):
    hf = h.astype(jnp.float32)
    mu = jnp.mean(hf, axis=-1, keepdims=True)
    var = jnp.mean(jnp.square(hf - mu), axis=-1, keepdims=True)
    y = (hf - mu) * lax.rsqrt(var + LN_EPS)
    if g is not None:
        y = y * g + b
    return y.astype(h.dtype)


def ada_mod(cvec, w, b):
    return (jax.nn.silu(cvec) @ w + b).reshape(cvec.shape[0], 9, D_MODEL)


def modulate(h, m, s):
    return h * (1.0 + m[:, None, 3 * s + 1]) + m[:, None, 3 * s]


def gate_of(m, s):
    return m[:, None, 3 * s + 2]


def swiglu(h, wi, wo):
    g, u = jnp.split(h @ wi, 2, axis=-1)
    return (jax.nn.silu(g) * u) @ wo


def conv3(u, w, b):
    n = u.shape[-2]
    up = jnp.pad(u, [(0, 0)] * (u.ndim - 2) + [(1, 1), (0, 0)])
    return up[..., 0:n, :] * w[0] + up[..., 1:n + 1, :] * w[1] + up[..., 2:n + 2, :] * w[2] + b


def latent_conv3(u, w, b):
    B, L, C = u.shape
    rows = L // GRID_W
    return conv3(u.reshape(B, rows, GRID_W, C), w, b).reshape(B, L, C)


def hyena_filters(L, w1, b1, f1, w2, b2, f2, w3):
    t = jnp.linspace(0.0, 1.0, L, dtype=jnp.float32)[:, None]
    w = (2.0 * math.pi / L) * jnp.arange(L, dtype=jnp.float32)[:, None]
    bands = jnp.linspace(1e-4, HY_BANDS - 1, HY_BANDS, dtype=jnp.float32)[None, :]
    z = jnp.concatenate([t, jnp.cos(bands * w), -jnp.sin(bands * w)], axis=-1)
    h = jnp.sin(f1 * (z @ w1 + b1))
    h = jnp.sin(f2 * (h @ w2 + b2))
    h = (h @ w3).astype(jnp.float32).reshape(L, HY_ORDER, 2, HY_W)
    deltas = jnp.abs(jnp.linspace(HY_MIN_DECAY, HY_MAX_DECAY, HY_W, dtype=jnp.float32))
    h = h * jnp.exp(-t * deltas)[:, None, None, :]
    return h / jnp.sum(jnp.abs(h), axis=(0, 2), keepdims=True)


def two_sided(h_f, h_b):
    head = h_f.at[0].add(h_b[0])
    return jnp.concatenate([head, jnp.zeros_like(h_f[:1]), h_b[:0:-1]], axis=0)


def fft_longconv(u, g, bias):
    L = u.shape[1]
    U = jnp.fft.rfft(u.astype(jnp.float32), n=2 * L, axis=1)
    G = jnp.fft.rfft(g.astype(jnp.float32), n=2 * L, axis=0)
    y = jnp.fft.irfft(U * G[None], n=2 * L, axis=1)[:, :L]
    return (y + u.astype(jnp.float32) * bias).astype(u.dtype)


def hyena(u3, filt, hy_bias):
    v, x1, x2 = jnp.split(u3, 3, axis=-1)
    z = x1 * fft_longconv(v, two_sided(filt[:, 0, 0], filt[:, 0, 1]), hy_bias[0])
    return x2 * fft_longconv(z, two_sided(filt[:, 1, 0], filt[:, 1, 1]), hy_bias[1])


def mlstm_state_inputs(p_s, conv_w, conv_b, gate_b, conv_fn):
    B, L, _ = p_s.shape
    k = conv_fn(p_s[..., :ML_QK], conv_w, conv_b).reshape(B, L, ML_HEADS, ML_DQK)
    v = p_s[..., ML_QK:ML_QK + ML_W].reshape(B, L, ML_HEADS, ML_DV)
    g = (p_s[..., ML_QK + ML_W:] + gate_b).astype(jnp.float32).reshape(B, L, 2, 2, ML_HEADS)
    return k, v, g[:, :, :, 0], jax.nn.log_sigmoid(g[:, :, :, 1])


def mlstm_query_inputs(p_q, conv_w, conv_b, conv_fn):
    B, L, _ = p_q.shape
    q = conv_fn(p_q[..., :ML_QK], conv_w, conv_b).reshape(B, L, ML_HEADS, ML_DQK) * (ML_DQK ** -0.5)
    return q, p_q[..., ML_QK:]


def mlstm_scan(k, v, ig, lf, state, q=None):
    B, L = k.shape[:2]
    nc = L // ML_CHUNK

    def chunks(a):
        a = a.astype(jnp.float32)
        return jnp.moveaxis(a.reshape((B, nc, ML_CHUNK) + a.shape[2:]), 1, 0)

    with_out = q is not None
    xs = (chunks(k), chunks(v), chunks(ig), chunks(lf)) + ((chunks(q),) if with_out else ())
    tri = jnp.tril(jnp.ones((ML_CHUNK, ML_CHUNK), dtype=bool))[None, :, :, None]

    def step(carry, inp):
        C, n, m = carry
        kc, vc, igc, lfc = inp[:4]
        b = jnp.cumsum(lfc, axis=1)
        b_end = b[:, -1]
        to_end = b_end[:, None] - b + igc
        m_new = jnp.maximum(b_end + m, jnp.max(to_end, axis=1))
        h = None
        if with_out:
            qc = inp[4]
            dlog = b[:, :, None] - b[:, None] + igc[:, None]
            dlog = jnp.where(tri, dlog, -jnp.inf)
            inter = b + m[:, None]
            m_j = jnp.maximum(inter, jnp.max(dlog, axis=2))
            s = jnp.einsum('bjhd,bshd->bjsh', qc, kc) * jnp.exp(dlog - m_j[:, :, None])
            w_inter = jnp.exp(inter - m_j)
            num = (jnp.einsum('bjsh,bshe->bjhe', s, vc)
                   + w_inter[..., None] * jnp.einsum('bhed,bjhd->bjhe', C, qc))
            den = jnp.sum(s, axis=2) + w_inter * jnp.einsum('bhd,bjhd->bjh', n, qc)
            h = num / jnp.maximum(jnp.abs(den), jnp.exp(-m_j))[..., None]
        w_state = jnp.exp(to_end - m_new[:, None])
        decay = jnp.exp(b_end + m - m_new)
        C_new = decay[..., None, None] * C + jnp.einsum('bshe,bshd->bhed', vc * w_state[..., None], kc)
        n_new = decay[..., None] * n + jnp.einsum('bsh,bshd->bhd', w_state, kc)
        return (C_new, n_new, m_new), h

    state, hs = lax.scan(step, state, xs)
    if not with_out:
        return None, state
    h = jnp.moveaxis(hs, 0, 1).reshape(B, L, ML_HEADS, ML_DV).astype(v.dtype)
    return h, state


def flip_t(*arrs):
    return tuple(jnp.flip(a, axis=1) for a in arrs)


def mlstm_merge(h, o, norm_w):
    B, L = h.shape[:2]
    return layer_norm(h).reshape(B, L, ML_W) * norm_w * jax.nn.sigmoid(o)


def mixer(u_lat, u_ctx, w_in, hy_conv_w, hy_conv_b, filt_params, hy_bias,
          ml_conv_w, ml_conv_b, ml_gate_b, ml_norm_w, w_out, with_ctx_out):
    B = u_ctx.shape[0]
    p_lat = u_lat @ w_in
    pc_state = u_ctx @ w_in[:, P_STATE0:]

    L_lat = u_lat.shape[1]
    hy_lat = hyena(latent_conv3(p_lat[..., :P_HY], hy_conv_w, hy_conv_b),
                   hyena_filters(L_lat, *filt_params), hy_bias)

    kc, vc, igc, lfc = mlstm_state_inputs(pc_state, ml_conv_w[:, ML_QK:], ml_conv_b[ML_QK:], ml_gate_b, conv3)
    kl, vl, igl, lfl = mlstm_state_inputs(p_lat[..., P_STATE0:], ml_conv_w[:, ML_QK:], ml_conv_b[ML_QK:],
                                          ml_gate_b, latent_conv3)
    ql, ol = mlstm_query_inputs(p_lat[..., P_HY:P_STATE0], ml_conv_w[:, :ML_QK], ml_conv_b[:ML_QK], latent_conv3)
    if with_ctx_out:
        pc_rest = u_ctx @ w_in[:, :P_STATE0]
        qc, oc = mlstm_query_inputs(pc_rest[..., P_HY:], ml_conv_w[:, :ML_QK], ml_conv_b[:ML_QK], conv3)
    else:
        qc = None
    zero = (jnp.zeros((B, ML_HEADS, ML_DV, ML_DQK), jnp.float32),
            jnp.zeros((B, ML_HEADS, ML_DQK), jnp.float32),
            jnp.zeros((B, ML_HEADS), jnp.float32))
    hcf, st_f = mlstm_scan(kc, vc, igc[:, :, 0], lfc[:, :, 0], zero, qc)
    hcb, st_b = mlstm_scan(*flip_t(kc, vc, igc[:, :, 1], lfc[:, :, 1]), zero,
                           None if qc is None else jnp.flip(qc, axis=1))
    hlf, _ = mlstm_scan(kl, vl, igl[:, :, 0], lfl[:, :, 0], st_f, ql)
    hlb, _ = mlstm_scan(*flip_t(kl, vl, igl[:, :, 1], lfl[:, :, 1]), st_b, jnp.flip(ql, axis=1))
    ml_lat = mlstm_merge(hlf + jnp.flip(hlb, axis=1), ol, ml_norm_w)
    y_lat = jnp.concatenate([hy_lat, ml_lat], axis=-1) @ w_out

    y_ctx = None
    if with_ctx_out:
        hy_ctx = hyena(conv3(pc_rest[..., :P_HY], hy_conv_w, hy_conv_b),
                       hyena_filters(u_ctx.shape[1], *filt_params), hy_bias)
        ml_ctx = mlstm_merge(hcf + jnp.flip(hcb, axis=1), oc, ml_norm_w)
        y_ctx = jnp.concatenate([hy_ctx, ml_ctx], axis=-1) @ w_out
    return y_lat, y_ctx


def setup_inputs(seed: int = 0) -> dict:
    key = jax.random.key(seed)
    ks = jax.random.split(key, 32)
    D, F, H = D_MODEL, FFN_DIM, ML_HEADS

    def nrm(k, shape, s):
        return s * jax.random.normal(k, shape, jnp.float32)

    i_b = nrm(ks[26], (DEPTH, 2, 1, H), 0.1)
    f_b = jnp.linspace(3.0, 6.0, H, dtype=jnp.float32) + nrm(ks[27], (DEPTH, 2, 1, H), 0.1)
    ml_gate_b = jnp.concatenate([i_b, f_b], axis=2).reshape(DEPTH, N_GATES)
    return {
        "x": nrm(ks[0], (BATCH, SEQ, D), 1.0),
        "c": nrm(ks[1], (BATCH, D), 1.0),
        "ctx": nrm(ks[2], (BATCH, CTX_LEN, D), 1.0),
        "c_ctx": nrm(ks[3], (D,), 1.0),
        "ada_w": nrm(ks[4], (DEPTH, D, 9 * D), D ** -0.5),
        "ada_b": nrm(ks[5], (DEPTH, 9 * D), 0.02),
        "ln_g": 1.0 + nrm(ks[6], (DEPTH, 3, D), 0.02),
        "ln_b": nrm(ks[7], (DEPTH, 3, D), 0.02),
        "ffn1_wi": nrm(ks[8], (DEPTH, D, 2 * F), D ** -0.5),
        "ffn1_wo": nrm(ks[9], (DEPTH, F, D), BETA * F ** -0.5),
        "ffn2_wi": nrm(ks[10], (DEPTH, D, 2 * F), D ** -0.5),
        "ffn2_wo": nrm(ks[11], (DEPTH, F, D), BETA * F ** -0.5),
        "w_in": nrm(ks[12], (DEPTH, D, P_TOTAL), D ** -0.5),
        "hy_conv_w": nrm(ks[13], (DEPTH, 3, P_HY), 3.0 ** -0.5),
        "hy_conv_b": nrm(ks[14], (DEPTH, P_HY), 0.02),
        "hy_filt_w1": nrm(ks[15], (DEPTH, HY_EMB, HY_FH), HY_EMB ** -0.5),
        "hy_filt_b1": nrm(ks[16], (DEPTH, HY_FH), 0.02),
        "hy_filt_f1": 1.0 + nrm(ks[17], (DEPTH, HY_FH), 0.1),
        "hy_filt_w2": nrm(ks[18], (DEPTH, HY_FH, HY_FH), HY_FH ** -0.5),
        "hy_filt_b2": nrm(ks[19], (DEPTH, HY_FH), 0.02),
        "hy_filt_f2": 1.0 + nrm(ks[20], (DEPTH, HY_FH), 0.1),
        "hy_filt_w3": nrm(ks[21], (DEPTH, HY_FH, HY_ORDER * 2 * HY_W), HY_FH ** -0.5),
        "hy_bias": nrm(ks[22], (DEPTH, HY_ORDER, HY_W), 0.5),
        "ml_conv_w": nrm(ks[23], (DEPTH, 3, 2 * ML_QK), 3.0 ** -0.5),
        "ml_conv_b": nrm(ks[24], (DEPTH, 2 * ML_QK), 0.02),
        "ml_gate_b": ml_gate_b,
        "ml_norm_w": 1.0 + nrm(ks[25], (DEPTH, ML_W), 0.02),
        "w_out": nrm(ks[28], (DEPTH, MIX_W, D), BETA * MIX_W ** -0.5),
    }


def reference(x, c, ctx, c_ctx, ada_w, ada_b, ln_g, ln_b, ffn1_wi, ffn1_wo, ffn2_wi, ffn2_wo,
              w_in, hy_conv_w, hy_conv_b, hy_filt_w1, hy_filt_b1, hy_filt_f1, hy_filt_w2,
              hy_filt_b2, hy_filt_f2, hy_filt_w3, hy_bias, ml_conv_w, ml_conv_b, ml_gate_b,
              ml_norm_w, w_out):
    x = layer_norm(x)
    ctx = layer_norm(ctx)
    for l in range(DEPTH):
        keep_ctx = l < DEPTH - 1
        m_lat = ada_mod(c, ada_w[l], ada_b[l])
        m_ctx = ada_mod(c_ctx[None], ada_w[l], ada_b[l])

        x = layer_norm(ALPHA * x + 0.5 * gate_of(m_lat, 0) * swiglu(modulate(x, m_lat, 0), ffn1_wi[l], ffn1_wo[l]),
                       ln_g[l, 0], ln_b[l, 0])
        ctx = layer_norm(ALPHA * ctx + 0.5 * gate_of(m_ctx, 0) * swiglu(modulate(ctx, m_ctx, 0), ffn1_wi[l], ffn1_wo[l]),
                         ln_g[l, 0], ln_b[l, 0])

        filt = (hy_filt_w1[l], hy_filt_b1[l], hy_filt_f1[l], hy_filt_w2[l], hy_filt_b2[l], hy_filt_f2[l], hy_filt_w3[l])
        y_lat, y_ctx = mixer(modulate(x, m_lat, 1), modulate(ctx, m_ctx, 1), w_in[l], hy_conv_w[l], hy_conv_b[l],
                             filt, hy_bias[l], ml_conv_w[l], ml_conv_b[l], ml_gate_b[l], ml_norm_w[l], w_out[l],
                             keep_ctx)
        x = layer_norm(ALPHA * x + gate_of(m_lat, 1) * y_lat, ln_g[l, 1], ln_b[l, 1])
        if keep_ctx:
            ctx = layer_norm(ALPHA * ctx + gate_of(m_ctx, 1) * y_ctx, ln_g[l, 1], ln_b[l, 1])

        x = layer_norm(ALPHA * x + 0.5 * gate_of(m_lat, 2) * swiglu(modulate(x, m_lat, 2), ffn2_wi[l], ffn2_wo[l]),
                       ln_g[l, 2], ln_b[l, 2])
        if keep_ctx:
            ctx = layer_norm(ALPHA * ctx + 0.5 * gate_of(m_ctx, 2) * swiglu(modulate(ctx, m_ctx, 2), ffn2_wi[l], ffn2_wo[l]),
                             ln_g[l, 2], ln_b[l, 2])
    return x
```

```python
import functools
import math

import numpy as np
import jax
import jax.numpy as jnp
from jax import lax
from jax.experimental import pallas as pl
from jax.experimental.pallas import tpu as pltpu

GRID_W = 64
ML_HEADS = 4
HY_ORDER = 2
HY_BANDS = 16
HY_DECAY_TARGET = 1e-2
HY_FAST_PCT = 0.3
HY_SLOW_PCT = 1.5
LN_EPS = 1e-5

V7X_VMEM_BYTES = 64 * 1024 * 1024
V7X_LANES = 128
V7X_SUBLANES = 8
V7X_MXU_DIM = 256
VMEM_LIMIT_BYTES = V7X_VMEM_BYTES - 8 * 1024 * 1024

DFT_N2 = 128
DFT_GROUPS = DFT_N2 // V7X_SUBLANES

F32 = jnp.float32
BF16 = jnp.bfloat16


def _params(*semantics):
    return pltpu.CompilerParams(dimension_semantics=semantics, vmem_limit_bytes=VMEM_LIMIT_BYTES)


def _ln(x):
    mu = jnp.mean(x, axis=-1, keepdims=True)
    xc = x - mu
    var = jnp.mean(xc * xc, axis=-1, keepdims=True)
    return xc * lax.rsqrt(var + LN_EPS)


def _silu(x):
    return x * jax.nn.sigmoid(x)


def _ada_kernel(c_ref, w_ref, b_ref, o_ref):
    s = _silu(c_ref[...]).astype(BF16)
    o_ref[...] = jnp.dot(s, w_ref[...].astype(BF16), preferred_element_type=F32) + b_ref[...]


def _ada(cvec, w, b, tn):
    rows, d = cvec.shape
    n = w.shape[1]
    return pl.pallas_call(
        _ada_kernel,
        grid=(n // tn,),
        in_specs=[pl.BlockSpec((rows, d), lambda j: (0, 0)),
                  pl.BlockSpec((d, tn), lambda j: (0, j)),
                  pl.BlockSpec((1, tn), lambda j: (0, j))],
        out_specs=pl.BlockSpec((rows, tn), lambda j: (0, j)),
        out_shape=jax.ShapeDtypeStruct((rows, n), F32),
        compiler_params=_params("arbitrary"),
        name="ada",
    )(cvec, w, b)


def _ffn_kernel(x_ref, sh_ref, sc_ref, gt_ref, wg_ref, wu_ref, wo_ref, lg_ref, lb_ref,
                o_ref, h_ref, *, pre_ln, alpha):
    f = pl.program_id(1)

    def stream():
        x = x_ref[...]
        return _ln(x) if pre_ln else x

    @pl.when(f == 0)
    def _():
        h_ref[...] = (stream() * (1.0 + sc_ref[...]) + sh_ref[...]).astype(BF16)
        o_ref[...] = jnp.zeros_like(o_ref)

    h = h_ref[...]
    g = jnp.dot(h, wg_ref[...], preferred_element_type=F32)
    u = jnp.dot(h, wu_ref[...], preferred_element_type=F32)
    a = (_silu(g) * u).astype(BF16)
    o_ref[...] += jnp.dot(a, wo_ref[...], preferred_element_type=F32)

    @pl.when(f == pl.num_programs(1) - 1)
    def _():
        y = alpha * stream() + (0.5 * gt_ref[...]) * o_ref[...]
        o_ref[...] = _ln(y) * lg_ref[...] + lb_ref[...]


def _ffn(x, shift, scale, gate, wi, wo, ln_g, ln_b, *, rows_per_mod, pre_ln, alpha, tm, tf):
    n, d = x.shape
    ff = wo.shape[0]
    nf = ff // tf
    per = rows_per_mod // tm
    mod_spec = pl.BlockSpec((None, 1, d), lambda i, f: (i // per, 0, 0))
    vec_spec = pl.BlockSpec((1, d), lambda i, f: (0, 0))
    return pl.pallas_call(
        functools.partial(_ffn_kernel, pre_ln=pre_ln, alpha=alpha),
        grid=(n // tm, nf),
        in_specs=[pl.BlockSpec((tm, d), lambda i, f: (i, 0), pipeline_mode=pl.Buffered(1)),
                  mod_spec, mod_spec, mod_spec,
                  pl.BlockSpec((d, tf), lambda i, f: (0, f)),
                  pl.BlockSpec((d, tf), lambda i, f: (0, nf + f)),
                  pl.BlockSpec((tf, d), lambda i, f: (f, 0)),
                  vec_spec, vec_spec],
        out_specs=pl.BlockSpec((tm, d), lambda i, f: (i, 0)),
        out_shape=jax.ShapeDtypeStruct((n, d), F32),
        scratch_shapes=[pltpu.VMEM((tm, d), BF16)],
        compiler_params=_params("arbitrary", "arbitrary"),
        name="ffn",
    )(x, shift, scale, gate, wi, wi, wo, ln_g, ln_b)


def _proj_kernel(x_ref, sh_ref, sc_ref, w_ref, wgate_ref, cw_ref, cb_ref, o_ref, og_ref, h_ref,
                 *, period):
    @pl.when(pl.program_id(1) == 0)
    def _():
        h = (x_ref[...] * (1.0 + sc_ref[...]) + sh_ref[...]).astype(BF16)
        h_ref[...] = h
        og_ref[...] = jnp.dot(h, wgate_ref[...], preferred_element_type=F32)

    r = jnp.dot(h_ref[...], w_ref[...], preferred_element_type=F32)
    tm = r.shape[0]
    pos = lax.broadcasted_iota(jnp.int32, (tm, 1), 0) % period
    prev = jnp.where(pos == 0, 0.0, pltpu.roll(r, 1, 0))
    nxt = jnp.where(pos == period - 1, 0.0, pltpu.roll(r, tm - 1, 0))
    cw = cw_ref[...]
    o_ref[...] = prev * cw[0:1] + r * cw[1:2] + nxt * cw[2:3] + cb_ref[...]


def _proj(x, shift, scale, w, w_gate, conv_w, conv_b, *, rows_per_mod, period, tm, tn):
    n, d = x.shape
    p = w.shape[1]
    per = rows_per_mod // tm
    mod_spec = pl.BlockSpec((None, 1, d), lambda i, j: (i // per, 0, 0))
    return pl.pallas_call(
        functools.partial(_proj_kernel, period=period),
        grid=(n // tm, p // tn),
        in_specs=[pl.BlockSpec((tm, d), lambda i, j: (i, 0)),
                  mod_spec, mod_spec,
                  pl.BlockSpec((d, tn), lambda i, j: (0, j)),
                  pl.BlockSpec((d, V7X_LANES), lambda i, j: (0, 0)),
                  pl.BlockSpec((V7X_SUBLANES, tn), lambda i, j: (0, j)),
                  pl.BlockSpec((1, tn), lambda i, j: (0, j))],
        out_specs=[pl.BlockSpec((tm, tn), lambda i, j: (i, j)),
                   pl.BlockSpec((tm, V7X_LANES), lambda i, j: (i, 0))],
        out_shape=[jax.ShapeDtypeStruct((n, p), F32),
                   jax.ShapeDtypeStruct((n, V7X_LANES), F32)],
        scratch_shapes=[pltpu.VMEM((tm, d), BF16)],
        compiler_params=_params("arbitrary", "arbitrary"),
        name="proj",
    )(x, shift, scale, w, w_gate, conv_w, conv_b)


def _filt_kernel(z_ref, w1_ref, b1_ref, f1_ref, w2_ref, b2_ref, f2_ref, w3_ref, dl_ref,
                 h_ref, asum_ref):
    hi = lax.Precision.HIGHEST
    z = z_ref[...]
    a = jnp.sin(f1_ref[...] * (jnp.dot(z, w1_ref[...], precision=hi, preferred_element_type=F32)
                               + b1_ref[...]))
    a = jnp.sin(f2_ref[...] * (jnp.dot(a, w2_ref[...], precision=hi, preferred_element_type=F32)
                               + b2_ref[...]))
    h = jnp.dot(a, w3_ref[...], precision=hi, preferred_element_type=F32)
    h = h * jnp.exp(-z[:, 0:1] * dl_ref[...])

    @pl.when(pl.program_id(0) == 0)
    def _():
        asum_ref[...] = jnp.zeros_like(asum_ref)

    h_ref[...] = h
    asum_ref[...] += jnp.sum(jnp.abs(h), axis=0, keepdims=True)


def _filters(z, w1, b1, f1, w2, b2, f2, w3, deltas, tl):
    l, ze = z.shape
    fh = w2.shape[0]
    n = w3.shape[1]
    full = lambda shape: pl.BlockSpec(shape, lambda i: (0, 0))
    return pl.pallas_call(
        _filt_kernel,
        grid=(l // tl,),
        in_specs=[pl.BlockSpec((tl, ze), lambda i: (i, 0)),
                  full((ze, fh)), full((1, fh)), full((1, fh)),
                  full((fh, fh)), full((1, fh)), full((1, fh)),
                  full((fh, n)), full((1, n))],
        out_specs=[pl.BlockSpec((tl, n), lambda i: (i, 0)), full((1, n))],
        out_shape=[jax.ShapeDtypeStruct((l, n), F32), jax.ShapeDtypeStruct((1, n), F32)],
        compiler_params=_params("arbitrary"),
        name="filt",
    )(z, w1, b1, f1, w2, b2, f2, w3, deltas)


@functools.lru_cache(maxsize=None)
def _dft_constants(seq_len):
    n = 2 * seq_len
    n1 = n // DFT_N2
    t1 = np.arange(n1 // 2)[None, :]
    k1 = np.arange(n1)[:, None]
    th = 2.0 * np.pi * (t1 * k1) / n1
    fwd = np.stack([np.cos(th), -np.sin(th)], axis=1).reshape(2 * n1, n1 // 2)
    inv = np.stack([np.cos(th), -np.sin(th)], axis=1).reshape(2 * n1, n1 // 2).T / n
    eye = np.eye(V7X_SUBLANES)
    fa = np.kron(fwd, eye)
    fa_inv = np.kron(inv, eye)
    t2 = np.arange(DFT_N2)[None, None, :]
    k2 = np.arange(DFT_N2)[None, :, None]
    ph = 2.0 * np.pi * (t2 * k2 / DFT_N2 + t2 * np.arange(n1)[:, None, None] / n)
    mr, mi = np.cos(ph), -np.sin(ph)
    mb = np.concatenate([np.concatenate([mr, -mi], axis=2),
                         np.concatenate([mi, mr], axis=2)], axis=1)
    as_bf16 = lambda a: jnp.asarray(a, dtype=F32).astype(BF16)
    return as_bf16(fa), as_bf16(fa_inv), as_bf16(mb), as_bf16(np.swapaxes(mb, 1, 2))


def _stage_a(fa_ref, u, s_ref):
    def body(g, carry):
        rhs = u(g)
        rows = rhs.shape[0] * rhs.shape[1]
        out = jnp.dot(fa_ref[...], rhs.reshape(rows, rhs.shape[2]).astype(BF16),
                      preferred_element_type=F32)
        s_ref[:, g, :, :] = out.reshape(out.shape[0] // V7X_SUBLANES, V7X_SUBLANES, out.shape[1])
        return carry
    lax.fori_loop(0, DFT_GROUPS, body, 0)


def _spec_kernel(hf_ref, hb_ref, nf_ref, nb_ref, fa_ref, mb_ref, g_ref, s1_ref, s2_ref, *, kb):
    s = pl.program_id(2)

    @pl.when(s == 0)
    def _():
        _stage_a(fa_ref, lambda g: hf_ref[:, g, :, :] + hb_ref[:, g, :, :], s1_ref)
        _stage_a(fa_ref, lambda g: hf_ref[:, g, :, :] - hb_ref[:, g, :, :], s2_ref)

    @pl.when(s > 0)
    def _():
        inv_norm = 1.0 / (nf_ref[...] + nb_ref[...])
        half = DFT_N2
        for i in range(kb):
            k1 = (s - 1) * kb + i
            lanes = s1_ref.shape[-1]
            a1 = s1_ref[pl.ds(2 * k1, 2), :, :, :].reshape(2 * half, lanes).astype(BF16)
            a2 = s2_ref[pl.ds(2 * k1, 2), :, :, :].reshape(2 * half, lanes).astype(BF16)
            z1 = jnp.dot(mb_ref[i], a1, preferred_element_type=F32)
            z2 = jnp.dot(mb_ref[i], a2, preferred_element_type=F32)
            g_ref[i, 0:half, :] = z1[0:half] * inv_norm
            g_ref[i, half:, :] = z2[half:] * inv_norm


def _spectra(hraw, asum, consts, hy_w, cb):
    fa, _, mb, _ = consts
    l = hraw.shape[0]
    n1 = mb.shape[0]
    kb = min(n1, V7X_SUBLANES)
    ncb = hy_w // cb
    h5 = hraw.reshape(n1 // 2, DFT_GROUPS, V7X_SUBLANES, hraw.shape[1])
    blk = (n1 // 2, DFT_GROUPS, V7X_SUBLANES, cb)
    kmap = lambda s: jnp.maximum(s - 1, 0)
    return pl.pallas_call(
        functools.partial(_spec_kernel, kb=kb),
        grid=(HY_ORDER, ncb, 1 + n1 // kb),
        in_specs=[pl.BlockSpec(blk, lambda o, c, s: (0, 0, 0, (2 * o) * ncb + c)),
                  pl.BlockSpec(blk, lambda o, c, s: (0, 0, 0, (2 * o + 1) * ncb + c)),
                  pl.BlockSpec((1, cb), lambda o, c, s: (0, (2 * o) * ncb + c)),
                  pl.BlockSpec((1, cb), lambda o, c, s: (0, (2 * o + 1) * ncb + c)),
                  pl.BlockSpec(fa.shape, lambda o, c, s: (0, 0)),
                  pl.BlockSpec((kb, 2 * DFT_N2, 2 * DFT_N2), lambda o, c, s: (kmap(s), 0, 0))],
        out_specs=pl.BlockSpec((None, kb, 2 * DFT_N2, cb), lambda o, c, s: (o, kmap(s), 0, c)),
        out_shape=jax.ShapeDtypeStruct((HY_ORDER, n1, 2 * DFT_N2, hy_w), F32),
        scratch_shapes=[pltpu.VMEM((2 * n1, DFT_GROUPS, V7X_SUBLANES, cb), F32),
                        pltpu.VMEM((2 * n1, DFT_GROUPS, V7X_SUBLANES, cb), F32)],
        compiler_params=_params("arbitrary", "arbitrary", "arbitrary"),
        name="spec",
    )(h5, h5, asum, asum, fa, mb)


def _conv_kernel(u_ref, x_ref, bias_ref, fa_ref, fai_ref, mb_ref, mbt_ref, g_ref, o_ref, s_ref,
                 *, kb):
    s = pl.program_id(2)
    last = pl.num_programs(2) - 1
    lanes = s_ref.shape[-1]
    half = DFT_N2

    @pl.when(s == 0)
    def _():
        _stage_a(fa_ref, lambda g: u_ref[:, g, :, :], s_ref)

    @pl.when(jnp.logical_and(s > 0, s < last))
    def _():
        for i in range(kb):
            k1 = (s - 1) * kb + i
            a = s_ref[pl.ds(2 * k1, 2), :, :, :].reshape(2 * half, lanes).astype(BF16)
            z = jnp.dot(mb_ref[i], a, preferred_element_type=F32)
            zr, zi = z[0:half], z[half:]
            gr, gi = g_ref[i, 0:half, :], g_ref[i, half:, :]
            p = jnp.concatenate([zr * gr - zi * gi, zr * gi + zi * gr], axis=0).astype(BF16)
            y = jnp.dot(mbt_ref[i], p, preferred_element_type=F32)
            s_ref[pl.ds(2 * k1, 2), :, :, :] = y.reshape(2, DFT_GROUPS, V7X_SUBLANES, lanes)

    @pl.when(s == last)
    def _():
        def body(g, carry):
            rhs = s_ref[:, g, :, :]
            rows = rhs.shape[0] * V7X_SUBLANES
            conv = jnp.dot(fai_ref[...], rhs.reshape(rows, lanes).astype(BF16),
                           preferred_element_type=F32)
            conv = conv.reshape(conv.shape[0] // V7X_SUBLANES, V7X_SUBLANES, lanes)
            u = u_ref[:, g, :, :]
            o_ref[:, g, :, :] = x_ref[:, g, :, :] * (conv + bias_ref[...] * u)
            return carry
        lax.fori_loop(0, DFT_GROUPS, body, 0)


def _longconv_gate(u_src, u_col, x_src, x_col, bias, spectrum, consts, hy_w, cb):
    fa, fa_inv, mb, mbt = consts
    b, l = u_src.shape[:2]
    n1 = mb.shape[0]
    kb = min(n1, V7X_SUBLANES)
    ncb = hy_w // cb
    view = lambda a: a.reshape(b, n1 // 2, DFT_GROUPS, V7X_SUBLANES, a.shape[2])
    blk = (None, n1 // 2, DFT_GROUPS, V7X_SUBLANES, cb)
    kmap = lambda s: jnp.clip(s - 1, 0, n1 // kb - 1)
    mat_spec = pl.BlockSpec((kb, 2 * DFT_N2, 2 * DFT_N2), lambda c, i, s: (kmap(s), 0, 0))
    out = pl.pallas_call(
        functools.partial(_conv_kernel, kb=kb),
        grid=(ncb, b, 2 + n1 // kb),
        in_specs=[pl.BlockSpec(blk, lambda c, i, s: (i, 0, 0, 0, u_col // cb + c)),
                  pl.BlockSpec(blk, lambda c, i, s: (i, 0, 0, 0, x_col // cb + c)),
                  pl.BlockSpec((1, cb), lambda c, i, s: (0, c)),
                  pl.BlockSpec(fa.shape, lambda c, i, s: (0, 0)),
                  pl.BlockSpec(fa_inv.shape, lambda c, i, s: (0, 0)),
                  mat_spec, mat_spec,
                  pl.BlockSpec((kb, 2 * DFT_N2, cb), lambda c, i, s: (kmap(s), 0, c))],
        out_specs=pl.BlockSpec(blk, lambda c, i, s: (i, 0, 0, 0, c)),
        out_shape=jax.ShapeDtypeStruct((b, n1 // 2, DFT_GROUPS, V7X_SUBLANES, hy_w), F32),
        scratch_shapes=[pltpu.VMEM((2 * n1, DFT_GROUPS, V7X_SUBLANES, cb), F32)],
        compiler_params=_params("arbitrary", "arbitrary", "arbitrary"),
        name="conv",
    )(view(u_src), view(x_src), bias, fa, fa_inv, mb, mbt, spectrum)
    return out.reshape(b, l, hy_w)


def _mlstm_dir(q, k, v_aug, ig_col, b_col, ig_row, b_row, b_end, keep, ct_ref, m_ref, idx):
    m = m_ref[idx]
    ct = ct_ref[idx]
    end_row = b_end - b_row + ig_row
    end_col = b_end - b_col + ig_col
    m_new = jnp.maximum(b_end + m, jnp.max(end_row, axis=1, keepdims=True))
    h = None
    if q is not None:
        dlog = jnp.where(keep, b_col - b_row + ig_row, -jnp.inf)
        inter = b_col + m
        m_j = jnp.maximum(inter, jnp.max(dlog, axis=1, keepdims=True))
        s = lax.dot_general(q, k, (((1,), (1,)), ((), ())), preferred_element_type=F32)
        p = (s * jnp.exp(dlog - m_j)).astype(BF16)
        nd = (jnp.dot(p, v_aug, preferred_element_type=F32)
              + jnp.exp(inter - m_j) * jnp.dot(q, ct.astype(BF16), preferred_element_type=F32))
        dv = nd.shape[1] - V7X_LANES
        den = nd[:, dv:dv + 1]
        h = nd[:, :dv] / jnp.maximum(jnp.abs(den), jnp.exp(-m_j))
    w_state = jnp.exp(end_col - m_new)
    decay = jnp.exp(b_end + m - m_new)
    vw = (v_aug.astype(F32) * w_state).astype(BF16)
    ct_ref[idx] = decay * ct + lax.dot_general(k, vw, (((0,), (0,)), ((), ())),
                                               preferred_element_type=F32)
    m_ref[idx] = m_new
    return h


def _mlstm_kernel(qf_ref, kf_ref, vf_ref, gf_ref, gtf_ref,
                  qb_ref, kb_ref, vb_ref, gb_ref, gtb_ref,
                  vc_ref, kc_ref, gc_ref, gtc_ref, bias_col_ref, bias_row_ref,
                  hf_ref, hb_ref, ct_ref, m_ref, *, heads):
    i = pl.program_id(1)
    hi = lax.Precision.HIGHEST
    lc = kc_ref.shape[0]
    dqk = kc_ref.shape[1] // heads
    dv = vc_ref.shape[1] // heads
    ng = 4 * heads
    row = lax.broadcasted_iota(jnp.int32, (lc, lc), 0)
    col = lax.broadcasted_iota(jnp.int32, (lc, lc), 1)
    lower = row >= col
    upper = row <= col
    tri_lo = lower.astype(F32)
    tri_up = upper.astype(F32)
    ones = jnp.ones((lc, V7X_LANES), BF16)

    def run(direction, q_ref, k_ref, v_ref, g_ref, gt_ref, out_ref):
        gates = g_ref[...][:, :ng] + bias_col_ref[...]
        gates_t = gt_ref[...] + bias_row_ref[...]
        tri_col, tri_row, keep = (tri_lo, tri_up, lower) if direction == 0 else (tri_up, tri_lo, upper)
        lf_t = jax.nn.log_sigmoid(gates_t)
        b_cols = jnp.dot(tri_col, jax.nn.log_sigmoid(gates), precision=hi, preferred_element_type=F32)
        b_rows = jnp.dot(lf_t, tri_row, precision=hi, preferred_element_type=F32)
        b_ends = jnp.sum(lf_t, axis=1, keepdims=True)
        for h in range(heads):
            ci = direction * 2 * heads + h
            cf = ci + heads
            k = k_ref[:, h * dqk:(h + 1) * dqk].astype(BF16)
            v_aug = jnp.concatenate([v_ref[:, h * dv:(h + 1) * dv].astype(BF16), ones], axis=1)
            q = None if q_ref is None else q_ref[:, h * dqk:(h + 1) * dqk].astype(BF16)
            out = _mlstm_dir(q, k, v_aug, gates[:, ci:ci + 1], b_cols[:, cf:cf + 1],
                             gates_t[ci:ci + 1, :], b_rows[cf:cf + 1, :], b_ends[cf:cf + 1, :],
                             keep, ct_ref, m_ref, direction * heads + h)
            if out_ref is not None:
                out_ref[:, h * dv:(h + 1) * dv] = out

    @pl.when(i == 0)
    def _():
        ct_ref[...] = jnp.zeros_like(ct_ref)
        m_ref[...] = jnp.zeros_like(m_ref)
        run(0, None, kc_ref, vc_ref, gc_ref, gtc_ref, None)
        run(1, None, kc_ref, vc_ref, gc_ref, gtc_ref, None)

    @pl.when(i > 0)
    def _():
        run(0, qf_ref, kf_ref, vf_ref, gf_ref, gtf_ref, hf_ref)
        run(1, qb_ref, kb_ref, vb_ref, gb_ref, gtb_ref, hb_ref)


def _mlstm(p_lat, g_lat, gt_lat, p_ctx, g_ctx, gt_ctx, gate_b, *, q_col, k_col, v_col,
           kc_col, vc_col, heads, dqk, dv):
    b, l = p_lat.shape[:2]
    lc = p_ctx.shape[1]
    nc = l // lc
    qk_w, v_w = heads * dqk, heads * dv
    ng = 4 * heads
    fwd = lambda i: jnp.maximum(i - 1, 0)
    bwd = lambda i: nc - jnp.maximum(i, 1)

    def lat_specs(chunk):
        return [pl.BlockSpec((None, lc, qk_w), lambda bi, i: (bi, chunk(i), q_col // qk_w)),
                pl.BlockSpec((None, lc, qk_w), lambda bi, i: (bi, chunk(i), k_col // qk_w)),
                pl.BlockSpec((None, lc, v_w), lambda bi, i: (bi, chunk(i), v_col // v_w)),
                pl.BlockSpec((None, lc, V7X_LANES), lambda bi, i: (bi, chunk(i), 0)),
                pl.BlockSpec((None, ng, lc), lambda bi, i: (bi, 0, chunk(i)))]

    ctx_specs = [pl.BlockSpec((None, lc, v_w), lambda bi, i: (bi, 0, vc_col // v_w)),
                 pl.BlockSpec((None, lc, qk_w), lambda bi, i: (bi, 0, kc_col // qk_w)),
                 pl.BlockSpec((None, lc, V7X_LANES), lambda bi, i: (bi, 0, 0)),
                 pl.BlockSpec((None, ng, lc), lambda bi, i: (bi, 0, 0))]
    bias_specs = [pl.BlockSpec((1, ng), lambda bi, i: (0, 0)),
                  pl.BlockSpec((ng, 1), lambda bi, i: (0, 0))]
    out_sds = jax.ShapeDtypeStruct((b, l, v_w), F32)
    return pl.pallas_call(
        functools.partial(_mlstm_kernel, heads=heads),
        grid=(b, nc + 1),
        in_specs=lat_specs(fwd) + lat_specs(bwd) + ctx_specs + bias_specs,
        out_specs=[pl.BlockSpec((None, lc, v_w), lambda bi, i: (bi, fwd(i), 0)),
                   pl.BlockSpec((None, lc, v_w), lambda bi, i: (bi, bwd(i), 0))],
        out_shape=[out_sds, out_sds],
        scratch_shapes=[pltpu.VMEM((2 * heads, dqk, dv + V7X_LANES), F32),
                        pltpu.VMEM((2 * heads, 1, 1), F32)],
        compiler_params=_params("arbitrary", "arbitrary"),
        name="mlstm",
    )(p_lat, p_lat, p_lat, g_lat, gt_lat, p_lat, p_lat, p_lat, g_lat, gt_lat,
      p_ctx, p_ctx, g_ctx, gt_ctx, gate_b.reshape(1, ng), gate_b.reshape(ng, 1))


def _merge_kernel(hf_ref, hb_ref, o_ref, w_ref, out_ref):
    h = _ln(hf_ref[...] + hb_ref[...])
    out_ref[...] = (h * w_ref[...] * jax.nn.sigmoid(o_ref[...])).astype(out_ref.dtype)


def _merge(hf, hb, p_lat, o_col, norm_w, heads, tm):
    n, w = hf.shape
    dv = w // heads
    return pl.pallas_call(
        _merge_kernel,
        grid=(n // tm, heads),
        in_specs=[pl.BlockSpec((tm, dv), lambda i, h: (i, h)),
                  pl.BlockSpec((tm, dv), lambda i, h: (i, h)),
                  pl.BlockSpec((tm, dv), lambda i, h: (i, o_col // dv + h)),
                  pl.BlockSpec((1, dv), lambda i, h: (0, h))],
        out_specs=pl.BlockSpec((tm, dv), lambda i, h: (i, h)),
        out_shape=jax.ShapeDtypeStruct((n, w), BF16),
        compiler_params=_params("arbitrary", "arbitrary"),
        name="merge",
    )(hf, hb, p_lat, norm_w)


def _outp_kernel(a_ref, w_ref, x_ref, gt_ref, lg_ref, lb_ref, o_ref, *, alpha):
    k = pl.program_id(1)

    @pl.when(k == 0)
    def _():
        o_ref[...] = jnp.zeros_like(o_ref)

    o_ref[...] += jnp.dot(a_ref[...], w_ref[...], preferred_element_type=F32)

    @pl.when(k == pl.num_programs(1) - 1)
    def _():
        y = alpha * x_ref[...] + gt_ref[...] * o_ref[...]
        o_ref[...] = _ln(y) * lg_ref[...] + lb_ref[...]


def _outp(a, w, x, gate, ln_g, ln_b, *, rows_per_mod, alpha, tm, tk):
    n, d = x.shape
    per = rows_per_mod // tm
    vec = pl.BlockSpec((1, d), lambda i, k: (0, 0))
    return pl.pallas_call(
        functools.partial(_outp_kernel, alpha=alpha),
        grid=(n // tm, a.shape[1] // tk),
        in_specs=[pl.BlockSpec((tm, tk), lambda i, k: (i, k)),
                  pl.BlockSpec((tk, d), lambda i, k: (k, 0)),
                  pl.BlockSpec((tm, d), lambda i, k: (i, 0)),
                  pl.BlockSpec((None, 1, d), lambda i, k: (i // per, 0, 0)),
                  vec, vec],
        out_specs=pl.BlockSpec((tm, d), lambda i, k: (i, 0)),
        out_shape=jax.ShapeDtypeStruct((n, d), F32),
        compiler_params=_params("arbitrary", "arbitrary"),
        name="outp",
    )(a, w, x, gate, ln_g, ln_b)


def kernel(x, c, ctx, c_ctx, ada_w, ada_b, ln_g, ln_b, ffn1_wi, ffn1_wo, ffn2_wi, ffn2_wo,
           w_in, hy_conv_w, hy_conv_b, hy_filt_w1, hy_filt_b1, hy_filt_f1, hy_filt_w2,
           hy_filt_b2, hy_filt_f2, hy_filt_w3, hy_bias, ml_conv_w, ml_conv_b, ml_gate_b,
           ml_norm_w, w_out):
    depth = ada_w.shape[0]
    assert depth == 1, "only the depth-1 block is implemented"
    b, l, d = x.shape
    lc = ctx.shape[1]
    heads = ML_HEADS
    hy_w = hy_bias.shape[2]
    ml_w = ml_norm_w.shape[1]
    dv = ml_w // heads
    ml_qk = ml_conv_w.shape[2] // 2
    dqk = ml_qk // heads
    p_hy = 3 * hy_w
    p_state0 = p_hy + ml_qk + ml_w
    p_main = p_state0 + ml_qk + ml_w
    n_gates = 4 * heads
    alpha = (2.0 * depth) ** 0.25
    assert l % GRID_W == 0 and l % lc == 0 and lc % GRID_W == 0

    tm = min(512, l)
    tm_ctx = min(512, b * lc)

    cvec = jnp.concatenate([c, c_ctx[None], jnp.zeros((V7X_SUBLANES - (b + 1) % V7X_SUBLANES, d), F32)])
    mods = _ada(cvec, ada_w.reshape(d, 9 * d), ada_b.reshape(1, 9 * d), tn=min(512, d))
    mods = mods.reshape(cvec.shape[0], 9, 1, d)
    m_lat = [mods[:b, k] for k in range(9)]
    m_ctx = [mods[b:b + 1, k] for k in range(9)]

    lg = [ln_g[0, k][None] for k in range(3)]
    lb = [ln_b[0, k][None] for k in range(3)]
    bf = lambda a: a.astype(BF16)
    x2 = x.reshape(b * l, d)
    ctx2 = ctx.reshape(b * lc, d)

    wi1, wo1 = bf(ffn1_wi[0]), bf(ffn1_wo[0])
    tf = 256
    x1 = _ffn(x2, m_lat[0], m_lat[1], m_lat[2], wi1, wo1, lg[0], lb[0],
              rows_per_mod=l, pre_ln=True, alpha=alpha, tm=tm, tf=tf)
    c1 = _ffn(ctx2, m_ctx[0], m_ctx[1], m_ctx[2], wi1, wo1, lg[0], lb[0],
              rows_per_mod=b * lc, pre_ln=True, alpha=alpha, tm=tm_ctx, tf=tf)

    w_in0 = w_in[0]
    w_main = bf(w_in0[:, :p_main])
    w_gate = bf(jnp.pad(w_in0[:, p_main:], ((0, 0), (0, V7X_LANES - n_gates))))
    q_scale = dqk ** -0.5
    ident = jnp.array([0.0, 1.0, 0.0], F32)[:, None]
    taps = jnp.concatenate([
        hy_conv_w[0], ml_conv_w[0][:, :ml_qk] * q_scale, jnp.tile(ident, (1, ml_w)),
        ml_conv_w[0][:, ml_qk:], jnp.tile(ident, (1, ml_w))], axis=1)
    taps = jnp.pad(taps, ((0, V7X_SUBLANES - 3), (0, 0)))
    tap_b = jnp.concatenate([
        hy_conv_b[0], ml_conv_b[0][:ml_qk] * q_scale, jnp.zeros((ml_w,), F32),
        ml_conv_b[0][ml_qk:], jnp.zeros((ml_w,), F32)])[None]
    tn = min(512, ml_qk)
    p_lat, g_lat = _proj(x1, m_lat[3], m_lat[4], w_main, w_gate, taps, tap_b,
                         rows_per_mod=l, period=GRID_W, tm=tm, tn=tn)
    v0 = p_state0 + ml_qk
    ctx_cols = lambda a: jnp.concatenate([a[:, v0:p_main], a[:, p_state0:v0]], axis=1)
    p_ctx, g_ctx = _proj(c1, m_ctx[3], m_ctx[4], ctx_cols(w_main), w_gate,
                         ctx_cols(taps), ctx_cols(tap_b),
                         rows_per_mod=b * lc, period=lc, tm=tm_ctx, tn=tn)
    p_lat3 = p_lat.reshape(b, l, p_main)
    p_ctx3 = p_ctx.reshape(b, lc, p_main - p_state0)

    t = jnp.linspace(0.0, 1.0, l, dtype=F32)[:, None]
    w = (2.0 * math.pi / l) * jnp.arange(l, dtype=F32)[:, None]
    bands = jnp.linspace(1e-4, HY_BANDS - 1, HY_BANDS, dtype=F32)[None, :]
    z = jnp.concatenate([t, jnp.cos(bands * w), -jnp.sin(bands * w)], axis=-1)
    ze = V7X_LANES
    z = jnp.pad(z, ((0, 0), (0, ze - z.shape[1])))
    w1 = jnp.pad(hy_filt_w1[0], ((0, ze - hy_filt_w1.shape[1]), (0, 0)))
    max_decay = math.log(HY_DECAY_TARGET) / HY_FAST_PCT
    min_decay = math.log(HY_DECAY_TARGET) / HY_SLOW_PCT
    deltas = jnp.abs(jnp.linspace(min_decay, max_decay, hy_w, dtype=F32))
    deltas = jnp.tile(deltas, 2 * HY_ORDER)[None]
    hraw, asum = _filters(z, w1, hy_filt_b1[0][None], hy_filt_f1[0][None], hy_filt_w2[0],
                          hy_filt_b2[0][None], hy_filt_f2[0][None], hy_filt_w3[0], deltas,
                          tl=min(256, l))
    consts = _dft_constants(l)
    spectra = _spectra(hraw, asum, consts, hy_w, cb=V7X_LANES)
    cb = min(V7X_MXU_DIM, hy_w)
    z1 = _longconv_gate(p_lat3, 0, p_lat3, hy_w, hy_bias[0, 0][None], spectra[0], consts, hy_w, cb)
    hy = _longconv_gate(z1, 0, p_lat3, 2 * hy_w, hy_bias[0, 1][None], spectra[1], consts, hy_w, cb)

    gt_lat = jnp.swapaxes(g_lat.reshape(b, l, V7X_LANES)[:, :, :n_gates], 1, 2)
    gt_ctx = jnp.swapaxes(g_ctx.reshape(b, lc, V7X_LANES)[:, :, :n_gates], 1, 2)
    hf, hb = _mlstm(p_lat3, g_lat.reshape(b, l, V7X_LANES), gt_lat,
                    p_ctx3, g_ctx.reshape(b, lc, V7X_LANES), gt_ctx, ml_gate_b[0],
                    q_col=p_hy, k_col=p_state0, v_col=p_state0 + ml_qk,
                    kc_col=ml_w, vc_col=0, heads=heads, dqk=dqk, dv=dv)
    ml = _merge(hf.reshape(b * l, ml_w), hb.reshape(b * l, ml_w), p_lat, p_hy + ml_qk,
                ml_norm_w[0][None], heads, tm)

    mixed = jnp.concatenate([bf(hy.reshape(b * l, hy_w)), ml], axis=1)
    x2_ = _outp(mixed, bf(w_out[0]), x1, m_lat[5], lg[1], lb[1],
                rows_per_mod=l, alpha=alpha, tm=tm, tk=min(512, hy_w))

    out = _ffn(x2_, m_lat[6], m_lat[7], m_lat[8], bf(ffn2_wi[0]), bf(ffn2_wo[0]), lg[2], lb[2],
               rows_per_mod=l, pre_ln=False, alpha=alpha, tm=tm, tf=tf)
    return out.reshape(b, l, d)
```

```python
import functools
import math

import numpy as np
import jax
import jax.numpy as jnp
from jax import lax
from jax.experimental import pallas as pl
from jax.experimental.pallas import tpu as pltpu

GRID_W = 64
ML_HEADS = 4
HY_ORDER = 2
HY_BANDS = 16
HY_DECAY_TARGET = 1e-2
HY_FAST_PCT = 0.3
HY_SLOW_PCT = 1.5
LN_EPS = 1e-5

V7X_VMEM_BYTES = 64 * 1024 * 1024
V7X_LANES = 128
V7X_SUBLANES = 8
V7X_MXU_DIM = 256
VMEM_LIMIT_BYTES = V7X_VMEM_BYTES - 4 * 1024 * 1024

DFT_N2 = 128
DFT_GROUPS = DFT_N2 // V7X_SUBLANES
ROW_CHUNK = 64

F32 = jnp.float32
BF16 = jnp.bfloat16


def _params(*semantics):
    return pltpu.CompilerParams(dimension_semantics=semantics, vmem_limit_bytes=VMEM_LIMIT_BYTES)


def _ln(x):
    mu = jnp.mean(x, axis=-1, keepdims=True)
    xc = x - mu
    var = jnp.mean(xc * xc, axis=-1, keepdims=True)
    return xc * lax.rsqrt(var + LN_EPS)


def _silu(x):
    return x * jax.nn.sigmoid(x)


def _ada_kernel(c_ref, w_ref, b_ref, o_ref):
    s = _silu(c_ref[...]).astype(BF16)
    o_ref[...] = jnp.dot(s, w_ref[...].astype(BF16), preferred_element_type=F32) + b_ref[...]


def _ada(cvec, w, b, tn):
    rows, d = cvec.shape
    n = w.shape[1]
    return pl.pallas_call(
        _ada_kernel,
        grid=(n // tn,),
        in_specs=[pl.BlockSpec((rows, d), lambda j: (0, 0)),
                  pl.BlockSpec((d, tn), lambda j: (0, j)),
                  pl.BlockSpec((1, tn), lambda j: (0, j))],
        out_specs=pl.BlockSpec((rows, tn), lambda j: (0, j)),
        out_shape=jax.ShapeDtypeStruct((rows, n), F32),
        compiler_params=_params("arbitrary"),
        name="ada",
    )(cvec, w, b)


def _for_row_chunks(rows, body):
    def step(r, carry):
        body(pl.ds(pl.multiple_of(r * ROW_CHUNK, ROW_CHUNK), ROW_CHUNK))
        return carry
    lax.fori_loop(0, rows // ROW_CHUNK, step, 0)


def _ffn_kernel(x_ref, pg_ref, pb_ref, sh_ref, sc_ref, gt_ref, wg_ref, wu_ref, wo_ref,
                lg_ref, lb_ref, o_ref, h_ref, *, pre, alpha):
    f = pl.program_id(1)
    tm = x_ref.shape[0]

    def stream(rs):
        x = x_ref[rs, :]
        if pre == "none":
            return x
        x = _ln(x)
        return x * pg_ref[...] + pb_ref[...] if pre == "ln_affine" else x

    @pl.when(f == 0)
    def _():
        def body(rs):
            h_ref[rs, :] = (stream(rs) * (1.0 + sc_ref[...]) + sh_ref[...]).astype(BF16)
            o_ref[rs, :] = jnp.zeros((ROW_CHUNK, o_ref.shape[1]), F32)
        _for_row_chunks(tm, body)

    h = h_ref[...]
    g = jnp.dot(h, wg_ref[...], preferred_element_type=F32)
    u = jnp.dot(h, wu_ref[...], preferred_element_type=F32)
    a = (_silu(g) * u).astype(BF16)
    o_ref[...] += jnp.dot(a, wo_ref[...], preferred_element_type=F32)

    @pl.when(f == pl.num_programs(1) - 1)
    def _():
        def body(rs):
            y = alpha * stream(rs) + (0.5 * gt_ref[...]) * o_ref[rs, :]
            o_ref[rs, :] = _ln(y) * lg_ref[...] + lb_ref[...]
        _for_row_chunks(tm, body)


def _ffn(x, pre_g, pre_b, shift, scale, gate, wi, wo, ln_g, ln_b, *, rows_per_mod, pre, alpha,
         tm, tf):
    n, d = x.shape
    ff = wo.shape[0]
    nf = ff // tf
    per = rows_per_mod // tm
    mod_spec = pl.BlockSpec((None, 1, d), lambda i, f: (i // per, 0, 0))
    vec_spec = pl.BlockSpec((1, d), lambda i, f: (0, 0))
    return pl.pallas_call(
        functools.partial(_ffn_kernel, pre=pre, alpha=alpha),
        grid=(n // tm, nf),
        in_specs=[pl.BlockSpec((tm, d), lambda i, f: (i, 0), pipeline_mode=pl.Buffered(1)),
                  vec_spec, vec_spec,
                  mod_spec, mod_spec, mod_spec,
                  pl.BlockSpec((d, tf), lambda i, f: (0, f)),
                  pl.BlockSpec((d, tf), lambda i, f: (0, nf + f)),
                  pl.BlockSpec((tf, d), lambda i, f: (f, 0)),
                  vec_spec, vec_spec],
        out_specs=pl.BlockSpec((tm, d), lambda i, f: (i, 0), pipeline_mode=pl.Buffered(1)),
        out_shape=jax.ShapeDtypeStruct((n, d), F32),
        scratch_shapes=[pltpu.VMEM((tm, d), BF16)],
        compiler_params=_params("arbitrary", "arbitrary"),
        name="ffn",
    )(x, pre_g, pre_b, shift, scale, gate, wi, wi, wo, ln_g, ln_b)


def _proj_kernel(x_ref, sh_ref, sc_ref, w_ref, wgate_ref, cw_ref, cb_ref, o_ref, og_ref, h_ref,
                 *, period):
    @pl.when(pl.program_id(1) == 0)
    def _():
        def body(rs):
            h_ref[rs, :] = (x_ref[rs, :] * (1.0 + sc_ref[...]) + sh_ref[...]).astype(BF16)
        _for_row_chunks(x_ref.shape[0], body)
        og_ref[...] = jnp.dot(h_ref[...], wgate_ref[...], preferred_element_type=F32)

    r = jnp.dot(h_ref[...], w_ref[...], preferred_element_type=F32)
    tm = r.shape[0]
    pos = lax.broadcasted_iota(jnp.int32, (tm, 1), 0) % period
    prev = jnp.where(pos == 0, 0.0, pltpu.roll(r, 1, 0))
    nxt = jnp.where(pos == period - 1, 0.0, pltpu.roll(r, tm - 1, 0))
    cw = cw_ref[...]
    o_ref[...] = prev * cw[0:1] + r * cw[1:2] + nxt * cw[2:3] + cb_ref[...]


def _proj(x, shift, scale, w, w_gate, conv_w, conv_b, *, rows_per_mod, period, tm, tn):
    n, d = x.shape
    p = w.shape[1]
    per = rows_per_mod // tm
    mod_spec = pl.BlockSpec((None, 1, d), lambda i, j: (i // per, 0, 0))
    return pl.pallas_call(
        functools.partial(_proj_kernel, period=period),
        grid=(n // tm, p // tn),
        in_specs=[pl.BlockSpec((tm, d), lambda i, j: (i, 0), pipeline_mode=pl.Buffered(1)),
                  mod_spec, mod_spec,
                  pl.BlockSpec((d, tn), lambda i, j: (0, j)),
                  pl.BlockSpec((d, V7X_LANES), lambda i, j: (0, 0)),
                  pl.BlockSpec((V7X_SUBLANES, tn), lambda i, j: (0, j)),
                  pl.BlockSpec((1, tn), lambda i, j: (0, j))],
        out_specs=[pl.BlockSpec((tm, tn), lambda i, j: (i, j)),
                   pl.BlockSpec((tm, V7X_LANES), lambda i, j: (i, 0))],
        out_shape=[jax.ShapeDtypeStruct((n, p), F32),
                   jax.ShapeDtypeStruct((n, V7X_LANES), F32)],
        scratch_shapes=[pltpu.VMEM((tm, d), BF16)],
        compiler_params=_params("arbitrary", "arbitrary"),
        name="proj",
    )(x, shift, scale, w, w_gate, conv_w, conv_b)


def _filt_kernel(z_ref, w1_ref, b1_ref, f1_ref, w2_ref, b2_ref, f2_ref, w3_ref, dl_ref,
                 h_ref, asum_ref):
    hi = lax.Precision.HIGHEST
    z = z_ref[...]
    a = jnp.sin(f1_ref[...] * (jnp.dot(z, w1_ref[...], precision=hi, preferred_element_type=F32)
                               + b1_ref[...]))
    a = jnp.sin(f2_ref[...] * (jnp.dot(a, w2_ref[...], precision=hi, preferred_element_type=F32)
                               + b2_ref[...]))
    h = jnp.dot(a, w3_ref[...], precision=hi, preferred_element_type=F32)
    h = h * jnp.exp(-z[:, 0:1] * dl_ref[...])

    @pl.when(pl.program_id(0) == 0)
    def _():
        asum_ref[...] = jnp.zeros_like(asum_ref)

    h_ref[...] = h
    asum_ref[...] += jnp.sum(jnp.abs(h), axis=0, keepdims=True)


def _filters(z, w1, b1, f1, w2, b2, f2, w3, deltas, tl):
    l, ze = z.shape
    fh = w2.shape[0]
    n = w3.shape[1]
    full = lambda shape: pl.BlockSpec(shape, lambda i: (0, 0))
    return pl.pallas_call(
        _filt_kernel,
        grid=(l // tl,),
        in_specs=[pl.BlockSpec((tl, ze), lambda i: (i, 0)),
                  full((ze, fh)), full((1, fh)), full((1, fh)),
                  full((fh, fh)), full((1, fh)), full((1, fh)),
                  full((fh, n)), full((1, n))],
        out_specs=[pl.BlockSpec((tl, n), lambda i: (i, 0)), full((1, n))],
        out_shape=[jax.ShapeDtypeStruct((l, n), F32), jax.ShapeDtypeStruct((1, n), F32)],
        compiler_params=_params("arbitrary"),
        name="filt",
    )(z, w1, b1, f1, w2, b2, f2, w3, deltas)


@functools.lru_cache(maxsize=None)
def _dft_constants(seq_len):
    n = 2 * seq_len
    n1 = n // DFT_N2
    t1 = np.arange(n1 // 2)[None, :]
    k1 = np.arange(n1)[:, None]
    th = 2.0 * np.pi * (t1 * k1) / n1
    fwd = np.stack([np.cos(th), -np.sin(th)], axis=1).reshape(2 * n1, n1 // 2)
    inv = np.stack([np.cos(th), -np.sin(th)], axis=1).reshape(2 * n1, n1 // 2).T / n
    eye = np.eye(V7X_SUBLANES)
    fa = np.kron(fwd, eye)
    fa_inv = np.kron(inv, eye)
    t2 = np.arange(DFT_N2)[None, None, :]
    k2 = np.arange(DFT_N2)[None, :, None]
    ph = 2.0 * np.pi * (t2 * k2 / DFT_N2 + t2 * np.arange(n1)[:, None, None] / n)
    mr, mi = np.cos(ph), -np.sin(ph)
    mb = np.concatenate([np.concatenate([mr, -mi], axis=2),
                         np.concatenate([mi, mr], axis=2)], axis=1)
    as_bf16 = lambda a: jnp.asarray(a, dtype=F32).astype(BF16)
    return as_bf16(fa), as_bf16(fa_inv), as_bf16(mb), as_bf16(np.swapaxes(mb, 1, 2))


def _stage_a(fa_ref, u, s_ref):
    def body(g, carry):
        rhs = u(g)
        rows = rhs.shape[0] * rhs.shape[1]
        out = jnp.dot(fa_ref[...], rhs.reshape(rows, rhs.shape[2]).astype(BF16),
                      preferred_element_type=F32)
        s_ref[:, g, :, :] = out.reshape(out.shape[0] // V7X_SUBLANES, V7X_SUBLANES, out.shape[1])
        return carry
    lax.fori_loop(0, DFT_GROUPS, body, 0)


def _spec_kernel(hf_ref, hb_ref, nf_ref, nb_ref, fa_ref, mb_ref, g_ref, s_ref, *, kb):
    s = pl.program_id(2)
    cb = hf_ref.shape[-1]

    @pl.when(s == 0)
    def _():
        def both(g):
            hf, hb = hf_ref[:, g, :, :], hb_ref[:, g, :, :]
            return jnp.concatenate([hf + hb, hf - hb], axis=-1)
        _stage_a(fa_ref, both, s_ref)

    @pl.when(s > 0)
    def _():
        inv_norm = 1.0 / (nf_ref[...] + nb_ref[...])
        half = DFT_N2
        for i in range(kb):
            k1 = (s - 1) * kb + i
            a = s_ref[pl.ds(2 * k1, 2), :, :, :].reshape(2 * half, 2 * cb).astype(BF16)
            z = jnp.dot(mb_ref[i], a, preferred_element_type=F32)
            g_ref[i, 0:half, :] = z[0:half, 0:cb] * inv_norm
            g_ref[i, half:, :] = z[half:, cb:] * inv_norm


def _spectra(hraw, asum, consts, hy_w, cb):
    fa, _, mb, _ = consts
    l = hraw.shape[0]
    n1 = mb.shape[0]
    kb = min(n1, V7X_SUBLANES)
    ncb = hy_w // cb
    h5 = hraw.reshape(n1 // 2, DFT_GROUPS, V7X_SUBLANES, hraw.shape[1])
    blk = (n1 // 2, DFT_GROUPS, V7X_SUBLANES, cb)
    kmap = lambda s: jnp.maximum(s - 1, 0)
    return pl.pallas_call(
        functools.partial(_spec_kernel, kb=kb),
        grid=(HY_ORDER, ncb, 1 + n1 // kb),
        in_specs=[pl.BlockSpec(blk, lambda o, c, s: (0, 0, 0, (2 * o) * ncb + c)),
                  pl.BlockSpec(blk, lambda o, c, s: (0, 0, 0, (2 * o + 1) * ncb + c)),
                  pl.BlockSpec((1, cb), lambda o, c, s: (0, (2 * o) * ncb + c)),
                  pl.BlockSpec((1, cb), lambda o, c, s: (0, (2 * o + 1) * ncb + c)),
                  pl.BlockSpec(fa.shape, lambda o, c, s: (0, 0)),
                  pl.BlockSpec((kb, 2 * DFT_N2, 2 * DFT_N2), lambda o, c, s: (kmap(s), 0, 0))],
        out_specs=pl.BlockSpec((None, kb, 2 * DFT_N2, cb), lambda o, c, s: (o, kmap(s), 0, c)),
        out_shape=jax.ShapeDtypeStruct((HY_ORDER, n1, 2 * DFT_N2, hy_w), F32),
        scratch_shapes=[pltpu.VMEM((2 * n1, DFT_GROUPS, V7X_SUBLANES, 2 * cb), F32)],
        compiler_params=_params("arbitrary", "arbitrary", "arbitrary"),
        name="spec",
    )(h5, h5, asum, asum, fa, mb)


def _conv_kernel(u_ref, x_ref, bias_ref, fa_ref, fai_ref, mb_ref, mbt_ref, g_ref, o_ref, s_ref,
                 *, kb):
    s = pl.program_id(2)
    last = pl.num_programs(2) - 1
    lanes = s_ref.shape[-1]
    half = DFT_N2

    @pl.when(s == 0)
    def _():
        _stage_a(fa_ref, lambda g: u_ref[:, g, :, :], s_ref)

    @pl.when(jnp.logical_and(s > 0, s < last))
    def _():
        for i in range(kb):
            k1 = (s - 1) * kb + i
            a = s_ref[pl.ds(2 * k1, 2), :, :, :].reshape(2 * half, lanes).astype(BF16)
            z = jnp.dot(mb_ref[i], a, preferred_element_type=F32)
            zr, zi = z[0:half], z[half:]
            gr, gi = g_ref[i, 0:half, :], g_ref[i, half:, :]
            p = jnp.concatenate([zr * gr - zi * gi, zr * gi + zi * gr], axis=0).astype(BF16)
            y = jnp.dot(mbt_ref[i], p, preferred_element_type=F32)
            s_ref[pl.ds(2 * k1, 2), :, :, :] = y.reshape(2, DFT_GROUPS, V7X_SUBLANES, lanes)

    @pl.when(s == last)
    def _():
        def body(g, carry):
            rhs = s_ref[:, g, :, :]
            rows = rhs.shape[0] * V7X_SUBLANES
            conv = jnp.dot(fai_ref[...], rhs.reshape(rows, lanes).astype(BF16),
                           preferred_element_type=F32)
            conv = conv.reshape(conv.shape[0] // V7X_SUBLANES, V7X_SUBLANES, lanes)
            u = u_ref[:, g, :, :]
            o_ref[:, g, :, :] = x_ref[:, g, :, :] * (conv + bias_ref[...] * u)
            return carry
        lax.fori_loop(0, DFT_GROUPS, body, 0)


def _longconv_gate(u_src, u_col, x_src, x_col, bias, spectrum, consts, hy_w, cb):
    fa, fa_inv, mb, mbt = consts
    b, l = u_src.shape[:2]
    n1 = mb.shape[0]
    kb = min(n1, V7X_SUBLANES)
    ncb = hy_w // cb
    view = lambda a: a.reshape(b, n1 // 2, DFT_GROUPS, V7X_SUBLANES, a.shape[2])
    blk = (None, n1 // 2, DFT_GROUPS, V7X_SUBLANES, cb)
    kmap = lambda s: jnp.clip(s - 1, 0, n1 // kb - 1)
    mat_spec = pl.BlockSpec((kb, 2 * DFT_N2, 2 * DFT_N2), lambda c, i, s: (kmap(s), 0, 0))
    out = pl.pallas_call(
        functools.partial(_conv_kernel, kb=kb),
        grid=(ncb, b, 2 + n1 // kb),
        in_specs=[pl.BlockSpec(blk, lambda c, i, s: (i, 0, 0, 0, u_col // cb + c)),
                  pl.BlockSpec(blk, lambda c, i, s: (i, 0, 0, 0, x_col // cb + c)),
                  pl.BlockSpec((1, cb), lambda c, i, s: (0, c)),
                  pl.BlockSpec(fa.shape, lambda c, i, s: (0, 0)),
                  pl.BlockSpec(fa_inv.shape, lambda c, i, s: (0, 0)),
                  mat_spec, mat_spec,
                  pl.BlockSpec((kb, 2 * DFT_N2, cb), lambda c, i, s: (kmap(s), 0, c))],
        out_specs=pl.BlockSpec(blk, lambda c, i, s: (i, 0, 0, 0, c)),
        out_shape=jax.ShapeDtypeStruct((b, n1 // 2, DFT_GROUPS, V7X_SUBLANES, hy_w), F32),
        scratch_shapes=[pltpu.VMEM((2 * n1, DFT_GROUPS, V7X_SUBLANES, cb), F32)],
        compiler_params=_params("arbitrary", "arbitrary", "arbitrary"),
        name="conv",
    )(view(u_src), view(x_src), bias, fa, fa_inv, mb, mbt, spectrum)
    return out.reshape(b, l, hy_w)


def _mlstm_dir(q, k, v_aug, ig_col, b_col, ig_row, b_row, b_end, keep, ct_ref, m_ref, idx):
    m = m_ref[idx]
    ct = ct_ref[idx]
    end_row = b_end - b_row + ig_row
    end_col = b_end - b_col + ig_col
    m_new = jnp.maximum(b_end + m, jnp.max(end_row, axis=1, keepdims=True))
    h = None
    if q is not None:
        dlog = jnp.where(keep, b_col - b_row + ig_row, -jnp.inf)
        inter = b_col + m
        m_j = jnp.maximum(inter, jnp.max(dlog, axis=1, keepdims=True))
        s = lax.dot_general(q, k, (((1,), (1,)), ((), ())), preferred_element_type=F32)
        p = (s * jnp.exp(dlog - m_j)).astype(BF16)
        nd = (jnp.dot(p, v_aug, preferred_element_type=F32)
              + jnp.exp(inter - m_j) * jnp.dot(q, ct.astype(BF16), preferred_element_type=F32))
        dv = nd.shape[1] - V7X_LANES
        den = nd[:, dv:dv + 1]
        h = nd[:, :dv] / jnp.maximum(jnp.abs(den), jnp.exp(-m_j))
    w_state = jnp.exp(end_col - m_new)
    decay = jnp.exp(b_end + m - m_new)
    vw = (v_aug.astype(F32) * w_state).astype(BF16)
    ct_ref[idx] = decay * ct + lax.dot_general(k, vw, (((0,), (0,)), ((), ())),
                                               preferred_element_type=F32)
    m_ref[idx] = m_new
    return h


def _mlstm_kernel(qf_ref, kf_ref, vf_ref, gf_ref, gtf_ref,
                  qb_ref, kb_ref, vb_ref, gb_ref, gtb_ref,
                  vc_ref, kc_ref, gc_ref, gtc_ref, bias_col_ref, bias_row_ref,
                  hf_ref, hb_ref, ct_ref, m_ref, *, heads):
    i = pl.program_id(1)
    hi = lax.Precision.HIGHEST
    lc = kc_ref.shape[0]
    dqk = kc_ref.shape[1] // heads
    dv = vc_ref.shape[1] // heads
    ng = 4 * heads
    row = lax.broadcasted_iota(jnp.int32, (lc, lc), 0)
    col = lax.broadcasted_iota(jnp.int32, (lc, lc), 1)
    lower = row >= col
    upper = row <= col
    tri_lo = lower.astype(F32)
    tri_up = upper.astype(F32)
    ones = jnp.ones((lc, V7X_LANES), BF16)

    def run(direction, q_ref, k_ref, v_ref, g_ref, gt_ref, out_ref):
        gates = g_ref[...][:, :ng] + bias_col_ref[...]
        gates_t = gt_ref[...] + bias_row_ref[...]
        tri_col, tri_row, keep = (tri_lo, tri_up, lower) if direction == 0 else (tri_up, tri_lo, upper)
        lf_t = jax.nn.log_sigmoid(gates_t)
        b_cols = jnp.dot(tri_col, jax.nn.log_sigmoid(gates), precision=hi, preferred_element_type=F32)
        b_rows = jnp.dot(lf_t, tri_row, precision=hi, preferred_element_type=F32)
        b_ends = jnp.sum(lf_t, axis=1, keepdims=True)
        for h in range(heads):
            ci = direction * 2 * heads + h
            cf = ci + heads
            k = k_ref[:, h * dqk:(h + 1) * dqk].astype(BF16)
            v_aug = jnp.concatenate([v_ref[:, h * dv:(h + 1) * dv].astype(BF16), ones], axis=1)
            q = None if q_ref is None else q_ref[:, h * dqk:(h + 1) * dqk].astype(BF16)
            out = _mlstm_dir(q, k, v_aug, gates[:, ci:ci + 1], b_cols[:, cf:cf + 1],
                             gates_t[ci:ci + 1, :], b_rows[cf:cf + 1, :], b_ends[cf:cf + 1, :],
                             keep, ct_ref, m_ref, direction * heads + h)
            if out_ref is not None:
                out_ref[:, h * dv:(h + 1) * dv] = out

    @pl.when(i == 0)
    def _():
        ct_ref[...] = jnp.zeros_like(ct_ref)
        m_ref[...] = jnp.zeros_like(m_ref)
        run(0, None, kc_ref, vc_ref, gc_ref, gtc_ref, None)
        run(1, None, kc_ref, vc_ref, gc_ref, gtc_ref, None)

    @pl.when(i > 0)
    def _():
        run(0, qf_ref, kf_ref, vf_ref, gf_ref, gtf_ref, hf_ref)
        run(1, qb_ref, kb_ref, vb_ref, gb_ref, gtb_ref, hb_ref)


def _mlstm(p_lat, g_lat, gt_lat, p_ctx, g_ctx, gt_ctx, gate_b, *, q_col, k_col, v_col,
           kc_col, vc_col, heads, dqk, dv):
    b, l = p_lat.shape[:2]
    lc = p_ctx.shape[1]
    nc = l // lc
    qk_w, v_w = heads * dqk, heads * dv
    ng = 4 * heads
    fwd = lambda i: jnp.maximum(i - 1, 0)
    bwd = lambda i: nc - jnp.maximum(i, 1)

    def lat_specs(chunk):
        return [pl.BlockSpec((None, lc, qk_w), lambda bi, i: (bi, chunk(i), q_col // qk_w)),
                pl.BlockSpec((None, lc, qk_w), lambda bi, i: (bi, chunk(i), k_col // qk_w)),
                pl.BlockSpec((None, lc, v_w), lambda bi, i: (bi, chunk(i), v_col // v_w)),
                pl.BlockSpec((None, lc, V7X_LANES), lambda bi, i: (bi, chunk(i), 0)),
                pl.BlockSpec((None, ng, lc), lambda bi, i: (bi, 0, chunk(i)))]

    ctx_specs = [pl.BlockSpec((None, lc, v_w), lambda bi, i: (bi, 0, vc_col // v_w)),
                 pl.BlockSpec((None, lc, qk_w), lambda bi, i: (bi, 0, kc_col // qk_w)),
                 pl.BlockSpec((None, lc, V7X_LANES), lambda bi, i: (bi, 0, 0)),
                 pl.BlockSpec((None, ng, lc), lambda bi, i: (bi, 0, 0))]
    bias_specs = [pl.BlockSpec((1, ng), lambda bi, i: (0, 0)),
                  pl.BlockSpec((ng, 1), lambda bi, i: (0, 0))]
    out_sds = jax.ShapeDtypeStruct((b, l, v_w), F32)
    return pl.pallas_call(
        functools.partial(_mlstm_kernel, heads=heads),
        grid=(b, nc + 1),
        in_specs=lat_specs(fwd) + lat_specs(bwd) + ctx_specs + bias_specs,
        out_specs=[pl.BlockSpec((None, lc, v_w), lambda bi, i: (bi, fwd(i), 0)),
                   pl.BlockSpec((None, lc, v_w), lambda bi, i: (bi, bwd(i), 0))],
        out_shape=[out_sds, out_sds],
        scratch_shapes=[pltpu.VMEM((2 * heads, dqk, dv + V7X_LANES), F32),
                        pltpu.VMEM((2 * heads, 1, 1), F32)],
        compiler_params=_params("arbitrary", "arbitrary"),
        name="mlstm",
    )(p_lat, p_lat, p_lat, g_lat, gt_lat, p_lat, p_lat, p_lat, g_lat, gt_lat,
      p_ctx, p_ctx, g_ctx, gt_ctx, gate_b.reshape(1, ng), gate_b.reshape(ng, 1))


def _merge_kernel(hf_ref, hb_ref, o_ref, w_ref, out_ref):
    h = _ln(hf_ref[...] + hb_ref[...])
    out_ref[...] = (h * w_ref[...] * jax.nn.sigmoid(o_ref[...])).astype(out_ref.dtype)


def _merge(hf, hb, p_lat, o_col, norm_w, heads, tm):
    n, w = hf.shape
    dv = w // heads
    return pl.pallas_call(
        _merge_kernel,
        grid=(n // tm, heads),
        in_specs=[pl.BlockSpec((tm, dv), lambda i, h: (i, h)),
                  pl.BlockSpec((tm, dv), lambda i, h: (i, h)),
                  pl.BlockSpec((tm, dv), lambda i, h: (i, o_col // dv + h)),
                  pl.BlockSpec((1, dv), lambda i, h: (0, h))],
        out_specs=pl.BlockSpec((tm, dv), lambda i, h: (i, h)),
        out_shape=jax.ShapeDtypeStruct((n, w), BF16),
        compiler_params=_params("arbitrary", "arbitrary"),
        name="merge",
    )(hf, hb, p_lat, norm_w)


def _outp_kernel(hy_ref, ml_ref, w_ref, x_ref, gt_ref, o_ref, a_ref, *, alpha):
    @pl.when(pl.program_id(1) == 0)
    def _():
        ka = hy_ref.shape[1]
        a_ref[:, :ka] = hy_ref[...].astype(BF16)
        a_ref[:, ka:] = ml_ref[...]

    y = jnp.dot(a_ref[...], w_ref[...], preferred_element_type=F32)
    o_ref[...] = alpha * x_ref[...] + gt_ref[...] * y


def _outp(hy, ml, w, x, gate, *, rows_per_mod, alpha, tm, tn):
    n, d = x.shape
    ka, kb = hy.shape[1], ml.shape[1]
    per = rows_per_mod // tm
    return pl.pallas_call(
        functools.partial(_outp_kernel, alpha=alpha),
        grid=(n // tm, d // tn),
        in_specs=[pl.BlockSpec((tm, ka), lambda i, j: (i, 0)),
                  pl.BlockSpec((tm, kb), lambda i, j: (i, 0)),
                  pl.BlockSpec((ka + kb, tn), lambda i, j: (0, j)),
                  pl.BlockSpec((tm, tn), lambda i, j: (i, j)),
                  pl.BlockSpec((None, 1, tn), lambda i, j: (i // per, 0, j))],
        out_specs=pl.BlockSpec((tm, tn), lambda i, j: (i, j)),
        out_shape=jax.ShapeDtypeStruct((n, d), F32),
        scratch_shapes=[pltpu.VMEM((tm, ka + kb), BF16)],
        compiler_params=_params("arbitrary", "arbitrary"),
        name="outp",
    )(hy, ml, w, x, gate)


def kernel(x, c, ctx, c_ctx, ada_w, ada_b, ln_g, ln_b, ffn1_wi, ffn1_wo, ffn2_wi, ffn2_wo,
           w_in, hy_conv_w, hy_conv_b, hy_filt_w1, hy_filt_b1, hy_filt_f1, hy_filt_w2,
           hy_filt_b2, hy_filt_f2, hy_filt_w3, hy_bias, ml_conv_w, ml_conv_b, ml_gate_b,
           ml_norm_w, w_out):
    depth = ada_w.shape[0]
    assert depth == 1, "only the depth-1 block is implemented"
    b, l, d = x.shape
    lc = ctx.shape[1]
    heads = ML_HEADS
    hy_w = hy_bias.shape[2]
    ml_w = ml_norm_w.shape[1]
    dv = ml_w // heads
    ml_qk = ml_conv_w.shape[2] // 2
    dqk = ml_qk // heads
    p_hy = 3 * hy_w
    p_state0 = p_hy + ml_qk + ml_w
    p_main = p_state0 + ml_qk + ml_w
    n_gates = 4 * heads
    alpha = (2.0 * depth) ** 0.25
    assert l % GRID_W == 0 and l % lc == 0 and lc % GRID_W == 0

    tm = min(512, l)
    tm_ctx = min(512, b * lc)
    tm_ffn = min(1024, l)
    tm_ffn_ctx = min(1024, b * lc)

    cvec = jnp.concatenate([c, c_ctx[None], jnp.zeros((V7X_SUBLANES - (b + 1) % V7X_SUBLANES, d), F32)])
    mods = _ada(cvec, ada_w.reshape(d, 9 * d), ada_b.reshape(1, 9 * d), tn=min(512, d))
    mods = mods.reshape(cvec.shape[0], 9, 1, d)
    m_lat = [mods[:b, k] for k in range(9)]
    m_ctx = [mods[b:b + 1, k] for k in range(9)]

    lg = [ln_g[0, k][None] for k in range(3)]
    lb = [ln_b[0, k][None] for k in range(3)]
    bf = lambda a: a.astype(BF16)
    x2 = x.reshape(b * l, d)
    ctx2 = ctx.reshape(b * lc, d)

    wi1, wo1 = bf(ffn1_wi[0]), bf(ffn1_wo[0])
    tf = 256
    x1 = _ffn(x2, lg[0], lb[0], m_lat[0], m_lat[1], m_lat[2], wi1, wo1, lg[0], lb[0],
              rows_per_mod=l, pre="ln", alpha=alpha, tm=tm_ffn, tf=tf)
    c1 = _ffn(ctx2, lg[0], lb[0], m_ctx[0], m_ctx[1], m_ctx[2], wi1, wo1, lg[0], lb[0],
              rows_per_mod=b * lc, pre="ln", alpha=alpha, tm=tm_ffn_ctx, tf=tf)

    w_in0 = w_in[0]
    w_main = bf(w_in0[:, :p_main])
    w_gate = bf(jnp.pad(w_in0[:, p_main:], ((0, 0), (0, V7X_LANES - n_gates))))
    q_scale = dqk ** -0.5
    ident = jnp.array([0.0, 1.0, 0.0], F32)[:, None]
    taps = jnp.concatenate([
        hy_conv_w[0], ml_conv_w[0][:, :ml_qk] * q_scale, jnp.tile(ident, (1, ml_w)),
        ml_conv_w[0][:, ml_qk:], jnp.tile(ident, (1, ml_w))], axis=1)
    taps = jnp.pad(taps, ((0, V7X_SUBLANES - 3), (0, 0)))
    tap_b = jnp.concatenate([
        hy_conv_b[0], ml_conv_b[0][:ml_qk] * q_scale, jnp.zeros((ml_w,), F32),
        ml_conv_b[0][ml_qk:], jnp.zeros((ml_w,), F32)])[None]
    tn = min(512, ml_qk)
    p_lat, g_lat = _proj(x1, m_lat[3], m_lat[4], w_main, w_gate, taps, tap_b,
                         rows_per_mod=l, period=GRID_W, tm=tm_ffn, tn=tn)
    v0 = p_state0 + ml_qk
    ctx_cols = lambda a: jnp.concatenate([a[:, v0:p_main], a[:, p_state0:v0]], axis=1)
    p_ctx, g_ctx = _proj(c1, m_ctx[3], m_ctx[4], ctx_cols(w_main), w_gate,
                         ctx_cols(taps), ctx_cols(tap_b),
                         rows_per_mod=b * lc, period=lc, tm=tm_ctx, tn=tn)
    p_lat3 = p_lat.reshape(b, l, p_main)
    p_ctx3 = p_ctx.reshape(b, lc, p_main - p_state0)

    t = jnp.linspace(0.0, 1.0, l, dtype=F32)[:, None]
    w = (2.0 * math.pi / l) * jnp.arange(l, dtype=F32)[:, None]
    bands = jnp.linspace(1e-4, HY_BANDS - 1, HY_BANDS, dtype=F32)[None, :]
    z = jnp.concatenate([t, jnp.cos(bands * w), -jnp.sin(bands * w)], axis=-1)
    ze = V7X_LANES
    z = jnp.pad(z, ((0, 0), (0, ze - z.shape[1])))
    w1 = jnp.pad(hy_filt_w1[0], ((0, ze - hy_filt_w1.shape[1]), (0, 0)))
    max_decay = math.log(HY_DECAY_TARGET) / HY_FAST_PCT
    min_decay = math.log(HY_DECAY_TARGET) / HY_SLOW_PCT
    deltas = jnp.abs(jnp.linspace(min_decay, max_decay, hy_w, dtype=F32))
    deltas = jnp.tile(deltas, 2 * HY_ORDER)[None]
    hraw, asum = _filters(z, w1, hy_filt_b1[0][None], hy_filt_f1[0][None], hy_filt_w2[0],
                          hy_filt_b2[0][None], hy_filt_f2[0][None], hy_filt_w3[0], deltas,
                          tl=min(256, l))
    consts = _dft_constants(l)
    spectra = _spectra(hraw, asum, consts, hy_w, cb=V7X_LANES)
    cb = min(V7X_MXU_DIM, hy_w)
    z1 = _longconv_gate(p_lat3, 0, p_lat3, hy_w, hy_bias[0, 0][None], spectra[0], consts, hy_w, cb)
    hy = _longconv_gate(z1, 0, p_lat3, 2 * hy_w, hy_bias[0, 1][None], spectra[1], consts, hy_w, cb)

    gt_lat = jnp.swapaxes(g_lat.reshape(b, l, V7X_LANES)[:, :, :n_gates], 1, 2)
    gt_ctx = jnp.swapaxes(g_ctx.reshape(b, lc, V7X_LANES)[:, :, :n_gates], 1, 2)
    hf, hb = _mlstm(p_lat3, g_lat.reshape(b, l, V7X_LANES), gt_lat,
                    p_ctx3, g_ctx.reshape(b, lc, V7X_LANES), gt_ctx, ml_gate_b[0],
                    q_col=p_hy, k_col=p_state0, v_col=p_state0 + ml_qk,
                    kc_col=ml_w, vc_col=0, heads=heads, dqk=dqk, dv=dv)
    ml = _merge(hf.reshape(b * l, ml_w), hb.reshape(b * l, ml_w), p_lat, p_hy + ml_qk,
                ml_norm_w[0][None], heads, tm)

    y2 = _outp(hy.reshape(b * l, hy_w), ml, bf(w_out[0]), x1, m_lat[5],
               rows_per_mod=l, alpha=alpha, tm=tm, tn=min(1024, d))

    out = _ffn(y2, lg[1], lb[1], m_lat[6], m_lat[7], m_lat[8], bf(ffn2_wi[0]), bf(ffn2_wo[0]),
               lg[2], lb[2], rows_per_mod=l, pre="ln_affine", alpha=alpha, tm=tm_ffn, tf=tf)
    return out.reshape(b, l, d)
```

```python
import functools
import math

import numpy as np
import jax
import jax.numpy as jnp
from jax import lax
from jax.experimental import pallas as pl
from jax.experimental.pallas import tpu as pltpu

GRID_W = 64
ML_HEADS = 4
HY_ORDER = 2
HY_BANDS = 16
HY_DECAY_TARGET = 1e-2
HY_FAST_PCT = 0.3
HY_SLOW_PCT = 1.5
LN_EPS = 1e-5

V7X_VMEM_BYTES = 64 * 1024 * 1024
V7X_LANES = 128
V7X_SUBLANES = 8
V7X_MXU_DIM = 256
VMEM_LIMIT_BYTES = V7X_VMEM_BYTES - 4 * 1024 * 1024

DFT_N2 = 128
DFT_GROUPS = DFT_N2 // V7X_SUBLANES
ROW_CHUNK = 64

F32 = jnp.float32
BF16 = jnp.bfloat16


def _params(*semantics):
    return pltpu.CompilerParams(dimension_semantics=semantics, vmem_limit_bytes=VMEM_LIMIT_BYTES)


def _ln(x):
    mu = jnp.mean(x, axis=-1, keepdims=True)
    xc = x - mu
    var = jnp.mean(xc * xc, axis=-1, keepdims=True)
    return xc * lax.rsqrt(var + LN_EPS)


def _silu(x):
    return x * jax.nn.sigmoid(x)


def _ada_kernel(c_ref, w_ref, b_ref, o_ref):
    s = _silu(c_ref[...]).astype(BF16)
    o_ref[...] = jnp.dot(s, w_ref[...].astype(BF16), preferred_element_type=F32) + b_ref[...]


def _ada(cvec, w, b, tn):
    rows, d = cvec.shape
    n = w.shape[1]
    return pl.pallas_call(
        _ada_kernel,
        grid=(n // tn,),
        in_specs=[pl.BlockSpec((rows, d), lambda j: (0, 0)),
                  pl.BlockSpec((d, tn), lambda j: (0, j)),
                  pl.BlockSpec((1, tn), lambda j: (0, j))],
        out_specs=pl.BlockSpec((rows, tn), lambda j: (0, j)),
        out_shape=jax.ShapeDtypeStruct((rows, n), F32),
        compiler_params=_params("arbitrary"),
        name="ada",
    )(cvec, w, b)


def _for_row_chunks(rows, body):
    def step(r, carry):
        body(pl.ds(pl.multiple_of(r * ROW_CHUNK, ROW_CHUNK), ROW_CHUNK))
        return carry
    lax.fori_loop(0, rows // ROW_CHUNK, step, 0)


def _ffn_kernel(x_ref, pg_ref, pb_ref, sh_ref, sc_ref, gt_ref, wg_ref, wu_ref, wo_ref,
                lg_ref, lb_ref, o_ref, h_ref, *, pre, alpha):
    f = pl.program_id(1)
    tm = x_ref.shape[0]

    def stream(rs):
        x = x_ref[rs, :]
        if pre == "none":
            return x
        x = _ln(x)
        return x * pg_ref[...] + pb_ref[...] if pre == "ln_affine" else x

    @pl.when(f == 0)
    def _():
        def body(rs):
            h_ref[rs, :] = (stream(rs) * (1.0 + sc_ref[...]) + sh_ref[...]).astype(BF16)
            o_ref[rs, :] = jnp.zeros((ROW_CHUNK, o_ref.shape[1]), F32)
        _for_row_chunks(tm, body)

    h = h_ref[...]
    g = jnp.dot(h, wg_ref[...], preferred_element_type=F32)
    u = jnp.dot(h, wu_ref[...], preferred_element_type=F32)
    a = (_silu(g) * u).astype(BF16)
    o_ref[...] += jnp.dot(a, wo_ref[...], preferred_element_type=F32)

    @pl.when(f == pl.num_programs(1) - 1)
    def _():
        def body(rs):
            y = alpha * stream(rs) + (0.5 * gt_ref[...]) * o_ref[rs, :]
            o_ref[rs, :] = _ln(y) * lg_ref[...] + lb_ref[...]
        _for_row_chunks(tm, body)


def _ffn(x, pre_g, pre_b, shift, scale, gate, wi, wo, ln_g, ln_b, *, rows_per_mod, pre, alpha,
         tm, tf):
    n, d = x.shape
    ff = wo.shape[0]
    nf = ff // tf
    per = rows_per_mod // tm
    mod_spec = pl.BlockSpec((None, 1, d), lambda i, f: (i // per, 0, 0))
    vec_spec = pl.BlockSpec((1, d), lambda i, f: (0, 0))
    return pl.pallas_call(
        functools.partial(_ffn_kernel, pre=pre, alpha=alpha),
        grid=(n // tm, nf),
        in_specs=[pl.BlockSpec((tm, d), lambda i, f: (i, 0), pipeline_mode=pl.Buffered(1)),
                  vec_spec, vec_spec,
                  mod_spec, mod_spec, mod_spec,
                  pl.BlockSpec((d, tf), lambda i, f: (0, f)),
                  pl.BlockSpec((d, tf), lambda i, f: (0, nf + f)),
                  pl.BlockSpec((tf, d), lambda i, f: (f, 0)),
                  vec_spec, vec_spec],
        out_specs=pl.BlockSpec((tm, d), lambda i, f: (i, 0), pipeline_mode=pl.Buffered(1)),
        out_shape=jax.ShapeDtypeStruct((n, d), F32),
        scratch_shapes=[pltpu.VMEM((tm, d), BF16)],
        compiler_params=_params("arbitrary", "arbitrary"),
        name="ffn",
    )(x, pre_g, pre_b, shift, scale, gate, wi, wi, wo, ln_g, ln_b)


def _proj_kernel(x_ref, sh_ref, sc_ref, w_ref, wgate_ref, cw_ref, cb_ref, o_ref, og_ref, h_ref,
                 *, period):
    @pl.when(pl.program_id(1) == 0)
    def _():
        def body(rs):
            h_ref[rs, :] = (x_ref[rs, :] * (1.0 + sc_ref[...]) + sh_ref[...]).astype(BF16)
        _for_row_chunks(x_ref.shape[0], body)
        og_ref[...] = jnp.dot(h_ref[...], wgate_ref[...], preferred_element_type=F32)

    r = jnp.dot(h_ref[...], w_ref[...], preferred_element_type=F32)
    tm = r.shape[0]
    pos = lax.broadcasted_iota(jnp.int32, (tm, 1), 0) % period
    prev = jnp.where(pos == 0, 0.0, pltpu.roll(r, 1, 0))
    nxt = jnp.where(pos == period - 1, 0.0, pltpu.roll(r, tm - 1, 0))
    cw = cw_ref[...]
    o_ref[...] = prev * cw[0:1] + r * cw[1:2] + nxt * cw[2:3] + cb_ref[...]


def _proj(x, shift, scale, w, w_gate, conv_w, conv_b, *, rows_per_mod, period, tm, tn):
    n, d = x.shape
    p = conv_w.shape[1]
    per = rows_per_mod // tm
    mod_spec = pl.BlockSpec((None, 1, d), lambda i, j: (i // per, 0, 0))
    return pl.pallas_call(
        functools.partial(_proj_kernel, period=period),
        grid=(n // tm, p // tn),
        in_specs=[pl.BlockSpec((tm, d), lambda i, j: (i, 0), pipeline_mode=pl.Buffered(1)),
                  mod_spec, mod_spec,
                  pl.BlockSpec((d, tn), lambda i, j: (0, j)),
                  pl.BlockSpec((d, V7X_LANES), lambda i, j: (0, 0)),
                  pl.BlockSpec((V7X_SUBLANES, tn), lambda i, j: (0, j)),
                  pl.BlockSpec((1, tn), lambda i, j: (0, j))],
        out_specs=[pl.BlockSpec((tm, tn), lambda i, j: (i, j)),
                   pl.BlockSpec((tm, V7X_LANES), lambda i, j: (i, 0))],
        out_shape=[jax.ShapeDtypeStruct((n, p), F32),
                   jax.ShapeDtypeStruct((n, V7X_LANES), F32)],
        scratch_shapes=[pltpu.VMEM((tm, d), BF16)],
        compiler_params=_params("arbitrary", "arbitrary"),
        name="proj",
    )(x, shift, scale, w, w_gate, conv_w, conv_b)


def _filt_kernel(z_ref, w1_ref, b1_ref, f1_ref, w2_ref, b2_ref, f2_ref, w3_ref, dl_ref,
                 h_ref, asum_ref):
    hi = lax.Precision.HIGHEST
    z = z_ref[...]
    a = jnp.sin(f1_ref[...] * (jnp.dot(z, w1_ref[...], precision=hi, preferred_element_type=F32)
                               + b1_ref[...]))
    a = jnp.sin(f2_ref[...] * (jnp.dot(a, w2_ref[...], precision=hi, preferred_element_type=F32)
                               + b2_ref[...]))
    h = jnp.dot(a, w3_ref[...], precision=hi, preferred_element_type=F32)
    h = h * jnp.exp(-z[:, 0:1] * dl_ref[...])

    @pl.when(pl.program_id(0) == 0)
    def _():
        asum_ref[...] = jnp.zeros_like(asum_ref)

    h_ref[...] = h
    asum_ref[...] += jnp.sum(jnp.abs(h), axis=0, keepdims=True)


def _filters(z, w1, b1, f1, w2, b2, f2, w3, deltas, tl):
    l, ze = z.shape
    fh = w2.shape[0]
    n = w3.shape[1]
    full = lambda shape: pl.BlockSpec(shape, lambda i: (0, 0))
    return pl.pallas_call(
        _filt_kernel,
        grid=(l // tl,),
        in_specs=[pl.BlockSpec((tl, ze), lambda i: (i, 0)),
                  full((ze, fh)), full((1, fh)), full((1, fh)),
                  full((fh, fh)), full((1, fh)), full((1, fh)),
                  full((fh, n)), full((1, n))],
        out_specs=[pl.BlockSpec((tl, n), lambda i: (i, 0)), full((1, n))],
        out_shape=[jax.ShapeDtypeStruct((l, n), F32), jax.ShapeDtypeStruct((1, n), F32)],
        compiler_params=_params("arbitrary"),
        name="filt",
    )(z, w1, b1, f1, w2, b2, f2, w3, deltas)


@functools.lru_cache(maxsize=None)
def _dft_constants(seq_len):
    n = 2 * seq_len
    nh = n // DFT_N2 // 2
    t1 = np.arange(nh)[None, :]
    k1 = np.arange(nh)[:, None]
    th = 2.0 * np.pi * (t1 * k1) / (2 * nh)
    fwd = np.stack([np.cos(th), -np.sin(th)], axis=1).reshape(2 * nh, nh)
    pair = np.where(k1 == 0, 1.0, 2.0) / n
    inv = np.stack([pair * np.cos(th), -pair * np.sin(th)], axis=1).reshape(2 * nh, nh).T
    eye = np.eye(V7X_SUBLANES)
    fa = np.kron(fwd, eye)
    fa_inv = np.kron(inv, eye)
    t2 = np.arange(DFT_N2)[None, None, :]
    k2 = np.arange(DFT_N2)[None, :, None]
    ph = 2.0 * np.pi * (t2 * k2 / DFT_N2 + t2 * np.arange(nh + 1)[:, None, None] / n)
    mr, mi = np.cos(ph), -np.sin(ph)
    mb = np.concatenate([np.concatenate([mr, -mi], axis=2),
                         np.concatenate([mi, mr], axis=2)], axis=1)
    as_bf16 = lambda a: jnp.asarray(a, dtype=F32).astype(BF16)
    return as_bf16(fa), as_bf16(fa_inv), as_bf16(mb)


def _units_per_step(units):
    return max(k for k in range(1, 12) if units % k == 0)


def _nyquist_sign(nh):
    t1 = lax.broadcasted_iota(jnp.int32, (nh, 1, 1), 0)
    return jnp.where(t1 % 2 == 0, 1.0, -1.0).astype(F32)


def _stage_a(fa_ref, u, s_ref):
    nh = fa_ref.shape[1] // V7X_SUBLANES
    lanes = s_ref.shape[-1]
    sign = _nyquist_sign(nh)

    def body(g, carry):
        rhs = u(g)
        out = jnp.dot(fa_ref[...], rhs.reshape(nh * V7X_SUBLANES, lanes).astype(BF16),
                      preferred_element_type=F32)
        s_ref[0:2 * nh, g, :, :] = out.reshape(2 * nh, V7X_SUBLANES, lanes)
        s_ref[2 * nh, g, :, :] = jnp.sum(rhs * sign, axis=0)
        s_ref[2 * nh + 1, g, :, :] = jnp.zeros((V7X_SUBLANES, lanes), F32)
        return carry
    lax.fori_loop(0, DFT_GROUPS, body, 0, unroll=2)


def _spec_kernel(hf_ref, hb_ref, nf_ref, nb_ref, fa_ref, mb_ref, g_ref, s_ref, *, kb):
    s = pl.program_id(2)
    cb = hf_ref.shape[-1]

    @pl.when(s == 0)
    def _():
        def both(g):
            hf, hb = hf_ref[:, g, :, :], hb_ref[:, g, :, :]
            return jnp.concatenate([hf + hb, hf - hb], axis=-1)
        _stage_a(fa_ref, both, s_ref)

    @pl.when(s > 0)
    def _():
        inv_norm = 1.0 / (nf_ref[...] + nb_ref[...])
        half = DFT_N2
        for i in range(kb):
            k1 = (s - 1) * kb + i
            a = s_ref[pl.ds(2 * k1, 2), :, :, :].reshape(2 * half, 2 * cb).astype(BF16)
            z = jnp.dot(mb_ref[i], a, preferred_element_type=F32)
            g_ref[i, 0:half, :] = (z[0:half, 0:cb] * inv_norm).astype(g_ref.dtype)
            g_ref[i, half:, :] = (z[half:, cb:] * inv_norm).astype(g_ref.dtype)


def _spectra(hraw, asum, consts, hy_w, cb):
    fa, _, mb = consts
    units = mb.shape[0]
    nh = units - 1
    kb = _units_per_step(units)
    ncb = hy_w // cb
    h5 = hraw.reshape(nh, DFT_GROUPS, V7X_SUBLANES, hraw.shape[1])
    blk = (nh, DFT_GROUPS, V7X_SUBLANES, cb)
    kmap = lambda s: jnp.maximum(s - 1, 0)
    return pl.pallas_call(
        functools.partial(_spec_kernel, kb=kb),
        grid=(HY_ORDER, ncb, 1 + units // kb),
        in_specs=[pl.BlockSpec(blk, lambda o, c, s: (0, 0, 0, (2 * o) * ncb + c)),
                  pl.BlockSpec(blk, lambda o, c, s: (0, 0, 0, (2 * o + 1) * ncb + c)),
                  pl.BlockSpec((1, cb), lambda o, c, s: (0, (2 * o) * ncb + c)),
                  pl.BlockSpec((1, cb), lambda o, c, s: (0, (2 * o + 1) * ncb + c)),
                  pl.BlockSpec(fa.shape, lambda o, c, s: (0, 0)),
                  pl.BlockSpec((kb, 2 * DFT_N2, 2 * DFT_N2), lambda o, c, s: (kmap(s), 0, 0))],
        out_specs=pl.BlockSpec((None, kb, 2 * DFT_N2, cb), lambda o, c, s: (o, kmap(s), 0, c)),
        out_shape=jax.ShapeDtypeStruct((HY_ORDER, units, 2 * DFT_N2, hy_w), BF16),
        scratch_shapes=[pltpu.VMEM((2 * units, DFT_GROUPS, V7X_SUBLANES, 2 * cb), F32)],
        compiler_params=_params("arbitrary", "arbitrary", "arbitrary"),
        name="spec",
    )(h5, h5, asum, asum, fa, mb)


def _conv_kernel(u_ref, x_ref, bias_ref, fa_ref, fai_ref, mb_ref, g_ref, o_ref, s_ref, *, kb):
    s = pl.program_id(2)
    last = pl.num_programs(2) - 1
    lanes = s_ref.shape[-1]
    half = DFT_N2
    nh = fa_ref.shape[1] // V7X_SUBLANES

    @pl.when(s == 0)
    def _():
        _stage_a(fa_ref, lambda g: u_ref[:, g, :, :], s_ref)

    @pl.when(jnp.logical_and(s > 0, s < last))
    def _():
        for i in range(kb):
            k1 = (s - 1) * kb + i
            a = s_ref[pl.ds(2 * k1, 2), :, :, :].reshape(2 * half, lanes).astype(BF16)
            z = jnp.dot(mb_ref[i], a, preferred_element_type=F32)
            zr, zi = z[0:half], z[half:]
            gr = g_ref[i, 0:half, :].astype(F32)
            gi = g_ref[i, half:, :].astype(F32)
            p = jnp.concatenate([zr * gr - zi * gi, zr * gi + zi * gr], axis=0).astype(BF16)
            y = lax.dot_general(mb_ref[i], p, (((0,), (0,)), ((), ())), preferred_element_type=F32)
            s_ref[pl.ds(2 * k1, 2), :, :, :] = y.reshape(2, DFT_GROUPS, V7X_SUBLANES, lanes)

    @pl.when(s == last)
    def _():
        n = 2 * nh * DFT_N2
        sign = _nyquist_sign(nh) * (1.0 / n)

        def body(g, carry):
            rhs = s_ref[0:2 * nh, g, :, :].reshape(2 * nh * V7X_SUBLANES, lanes)
            conv = jnp.dot(fai_ref[...], rhs.astype(BF16), preferred_element_type=F32)
            conv = conv.reshape(nh, V7X_SUBLANES, lanes) + sign * s_ref[2 * nh, g, :, :]
            u = u_ref[:, g, :, :]
            o_ref[:, g, :, :] = x_ref[:, g, :, :] * (conv + bias_ref[...] * u)
            return carry
        lax.fori_loop(0, DFT_GROUPS, body, 0, unroll=2)


def _longconv_gate(u_src, u_col, x_src, x_col, bias, spectra, order, consts, hy_w, cb):
    fa, fa_inv, mb = consts
    b, l = u_src.shape[:2]
    units = mb.shape[0]
    nh = units - 1
    kb = _units_per_step(units)
    ncb = hy_w // cb
    view = lambda a: a.reshape(b, nh, DFT_GROUPS, V7X_SUBLANES, a.shape[2])
    blk = (None, nh, DFT_GROUPS, V7X_SUBLANES, cb)
    kmap = lambda s: jnp.clip(s - 1, 0, units // kb - 1)
    out = pl.pallas_call(
        functools.partial(_conv_kernel, kb=kb),
        grid=(ncb, b, 2 + units // kb),
        in_specs=[pl.BlockSpec(blk, lambda c, i, s: (i, 0, 0, 0, u_col // cb + c)),
                  pl.BlockSpec(blk, lambda c, i, s: (i, 0, 0, 0, x_col // cb + c)),
                  pl.BlockSpec((1, cb), lambda c, i, s: (0, c)),
                  pl.BlockSpec(fa.shape, lambda c, i, s: (0, 0)),
                  pl.BlockSpec(fa_inv.shape, lambda c, i, s: (0, 0)),
                  pl.BlockSpec((kb, 2 * DFT_N2, 2 * DFT_N2), lambda c, i, s: (kmap(s), 0, 0)),
                  pl.BlockSpec((None, kb, 2 * DFT_N2, cb), lambda c, i, s: (order, kmap(s), 0, c))],
        out_specs=pl.BlockSpec(blk, lambda c, i, s: (i, 0, 0, 0, c)),
        out_shape=jax.ShapeDtypeStruct((b, nh, DFT_GROUPS, V7X_SUBLANES, hy_w), F32),
        scratch_shapes=[pltpu.VMEM((2 * units, DFT_GROUPS, V7X_SUBLANES, cb), F32)],
        compiler_params=_params("arbitrary", "arbitrary", "arbitrary"),
        name="conv",
    )(view(u_src), view(x_src), bias, fa, fa_inv, mb, spectra)
    return out.reshape(b, l, hy_w)


def _mlstm_dir(q, k, v_aug, ig_col, b_col, ig_row, b_row, b_end, keep, ct_ref, m_ref, idx):
    m = m_ref[idx]
    ct = ct_ref[idx]
    end_row = b_end - b_row + ig_row
    end_col = b_end - b_col + ig_col
    m_new = jnp.maximum(b_end + m, jnp.max(end_row, axis=1, keepdims=True))
    h = None
    if q is not None:
        dlog = jnp.where(keep, b_col - b_row + ig_row, -jnp.inf)
        inter = b_col + m
        m_j = jnp.maximum(inter, jnp.max(dlog, axis=1, keepdims=True))
        s = lax.dot_general(q, k, (((1,), (1,)), ((), ())), preferred_element_type=F32)
        p = (s * jnp.exp(dlog - m_j)).astype(BF16)
        nd = (jnp.dot(p, v_aug, preferred_element_type=F32)
              + jnp.exp(inter - m_j) * jnp.dot(q, ct.astype(BF16), preferred_element_type=F32))
        dv = nd.shape[1] - V7X_LANES
        den = nd[:, dv:dv + 1]
        h = nd[:, :dv] / jnp.maximum(jnp.abs(den), jnp.exp(-m_j))
    w_state = jnp.exp(end_col - m_new)
    decay = jnp.exp(b_end + m - m_new)
    vw = (v_aug.astype(F32) * w_state).astype(BF16)
    ct_ref[idx] = decay * ct + lax.dot_general(k, vw, (((0,), (0,)), ((), ())),
                                               preferred_element_type=F32)
    m_ref[idx] = m_new
    return h


def _mlstm_kernel(qf_ref, kf_ref, vf_ref, gf_ref, gtf_ref,
                  qb_ref, kb_ref, vb_ref, gb_ref, gtb_ref,
                  vc_ref, kc_ref, gc_ref, gtc_ref, bias_col_ref, bias_row_ref,
                  hf_ref, hb_ref, ct_ref, m_ref, *, heads):
    i = pl.program_id(1)
    hi = lax.Precision.HIGHEST
    lc = kc_ref.shape[0]
    dqk = kc_ref.shape[1] // heads
    dv = vc_ref.shape[1] // heads
    ng = 4 * heads
    row = lax.broadcasted_iota(jnp.int32, (lc, lc), 0)
    col = lax.broadcasted_iota(jnp.int32, (lc, lc), 1)
    lower = row >= col
    upper = row <= col
    tri_lo = lower.astype(F32)
    tri_up = upper.astype(F32)
    ones = jnp.ones((lc, V7X_LANES), BF16)

    def run(direction, q_ref, k_ref, v_ref, g_ref, gt_ref, out_ref):
        gates = g_ref[...][:, :ng] + bias_col_ref[...]
        gates_t = gt_ref[...] + bias_row_ref[...]
        tri_col, tri_row, keep = (tri_lo, tri_up, lower) if direction == 0 else (tri_up, tri_lo, upper)
        lf_t = jax.nn.log_sigmoid(gates_t)
        b_cols = jnp.dot(tri_col, jax.nn.log_sigmoid(gates), precision=hi, preferred_element_type=F32)
        b_rows = jnp.dot(lf_t, tri_row, precision=hi, preferred_element_type=F32)
        b_ends = jnp.sum(lf_t, axis=1, keepdims=True)
        for h in range(heads):
            ci = direction * 2 * heads + h
            cf = ci + heads
            k = k_ref[:, h * dqk:(h + 1) * dqk].astype(BF16)
            v_aug = jnp.concatenate([v_ref[:, h * dv:(h + 1) * dv].astype(BF16), ones], axis=1)
            q = None if q_ref is None else q_ref[:, h * dqk:(h + 1) * dqk].astype(BF16)
            out = _mlstm_dir(q, k, v_aug, gates[:, ci:ci + 1], b_cols[:, cf:cf + 1],
                             gates_t[ci:ci + 1, :], b_rows[cf:cf + 1, :], b_ends[cf:cf + 1, :],
                             keep, ct_ref, m_ref, direction * heads + h)
            if out_ref is not None:
                out_ref[:, h * dv:(h + 1) * dv] = out

    @pl.when(i == 0)
    def _():
        ct_ref[...] = jnp.zeros_like(ct_ref)
        m_ref[...] = jnp.zeros_like(m_ref)
        run(0, None, kc_ref, vc_ref, gc_ref, gtc_ref, None)
        run(1, None, kc_ref, vc_ref, gc_ref, gtc_ref, None)

    @pl.when(i > 0)
    def _():
        run(0, qf_ref, kf_ref, vf_ref, gf_ref, gtf_ref, hf_ref)
        run(1, qb_ref, kb_ref, vb_ref, gb_ref, gtb_ref, hb_ref)


def _mlstm(p_lat, g_lat, gt_lat, p_ctx, g_ctx, gt_ctx, gate_b, *, q_col, k_col, v_col,
           kc_col, vc_col, heads, dqk, dv):
    b, l = p_lat.shape[:2]
    lc = p_ctx.shape[1]
    nc = l // lc
    qk_w, v_w = heads * dqk, heads * dv
    ng = 4 * heads
    fwd = lambda i: jnp.maximum(i - 1, 0)
    bwd = lambda i: nc - jnp.maximum(i, 1)

    def lat_specs(chunk):
        return [pl.BlockSpec((None, lc, qk_w), lambda bi, i: (bi, chunk(i), q_col // qk_w)),
                pl.BlockSpec((None, lc, qk_w), lambda bi, i: (bi, chunk(i), k_col // qk_w)),
                pl.BlockSpec((None, lc, v_w), lambda bi, i: (bi, chunk(i), v_col // v_w)),
                pl.BlockSpec((None, lc, V7X_LANES), lambda bi, i: (bi, chunk(i), 0)),
                pl.BlockSpec((None, ng, lc), lambda bi, i: (bi, 0, chunk(i)))]

    ctx_specs = [pl.BlockSpec((None, lc, v_w), lambda bi, i: (bi, 0, vc_col // v_w)),
                 pl.BlockSpec((None, lc, qk_w), lambda bi, i: (bi, 0, kc_col // qk_w)),
                 pl.BlockSpec((None, lc, V7X_LANES), lambda bi, i: (bi, 0, 0)),
                 pl.BlockSpec((None, ng, lc), lambda bi, i: (bi, 0, 0))]
    bias_specs = [pl.BlockSpec((1, ng), lambda bi, i: (0, 0)),
                  pl.BlockSpec((ng, 1), lambda bi, i: (0, 0))]
    out_sds = jax.ShapeDtypeStruct((b, l, v_w), F32)
    return pl.pallas_call(
        functools.partial(_mlstm_kernel, heads=heads),
        grid=(b, nc + 1),
        in_specs=lat_specs(fwd) + lat_specs(bwd) + ctx_specs + bias_specs,
        out_specs=[pl.BlockSpec((None, lc, v_w), lambda bi, i: (bi, fwd(i), 0)),
                   pl.BlockSpec((None, lc, v_w), lambda bi, i: (bi, bwd(i), 0))],
        out_shape=[out_sds, out_sds],
        scratch_shapes=[pltpu.VMEM((2 * heads, dqk, dv + V7X_LANES), F32),
                        pltpu.VMEM((2 * heads, 1, 1), F32)],
        compiler_params=_params("arbitrary", "arbitrary"),
        name="mlstm",
    )(p_lat, p_lat, p_lat, g_lat, gt_lat, p_lat, p_lat, p_lat, g_lat, gt_lat,
      p_ctx, p_ctx, g_ctx, gt_ctx, gate_b.reshape(1, ng), gate_b.reshape(ng, 1))


def _merge_kernel(hf_ref, hb_ref, o_ref, w_ref, out_ref):
    h = _ln(hf_ref[...] + hb_ref[...])
    out_ref[...] = (h * w_ref[...] * jax.nn.sigmoid(o_ref[...])).astype(out_ref.dtype)


def _merge(hf, hb, p_lat, o_col, norm_w, heads, tm):
    n, w = hf.shape
    dv = w // heads
    return pl.pallas_call(
        _merge_kernel,
        grid=(n // tm, heads),
        in_specs=[pl.BlockSpec((tm, dv), lambda i, h: (i, h)),
                  pl.BlockSpec((tm, dv), lambda i, h: (i, h)),
                  pl.BlockSpec((tm, dv), lambda i, h: (i, o_col // dv + h)),
                  pl.BlockSpec((1, dv), lambda i, h: (0, h))],
        out_specs=pl.BlockSpec((tm, dv), lambda i, h: (i, h)),
        out_shape=jax.ShapeDtypeStruct((n, w), BF16),
        compiler_params=_params("arbitrary", "arbitrary"),
        name="merge",
    )(hf, hb, p_lat, norm_w)


def _outp_kernel(hy_ref, ml_ref, w_ref, x_ref, gt_ref, o_ref, a_ref, *, alpha):
    @pl.when(pl.program_id(1) == 0)
    def _():
        ka = hy_ref.shape[1]
        a_ref[:, :ka] = hy_ref[...].astype(BF16)
        a_ref[:, ka:] = ml_ref[...]

    y = jnp.dot(a_ref[...], w_ref[...], preferred_element_type=F32)
    o_ref[...] = alpha * x_ref[...] + gt_ref[...] * y


def _outp(hy, ml, w, x, gate, *, rows_per_mod, alpha, tm, tn):
    n, d = x.shape
    ka, kb = hy.shape[1], ml.shape[1]
    per = rows_per_mod // tm
    return pl.pallas_call(
        functools.partial(_outp_kernel, alpha=alpha),
        grid=(n // tm, d // tn),
        in_specs=[pl.BlockSpec((tm, ka), lambda i, j: (i, 0)),
                  pl.BlockSpec((tm, kb), lambda i, j: (i, 0)),
                  pl.BlockSpec((ka + kb, tn), lambda i, j: (0, j)),
                  pl.BlockSpec((tm, tn), lambda i, j: (i, j)),
                  pl.BlockSpec((None, 1, tn), lambda i, j: (i // per, 0, j))],
        out_specs=pl.BlockSpec((tm, tn), lambda i, j: (i, j)),
        out_shape=jax.ShapeDtypeStruct((n, d), F32),
        scratch_shapes=[pltpu.VMEM((tm, ka + kb), BF16)],
        compiler_params=_params("arbitrary", "arbitrary"),
        name="outp",
    )(hy, ml, w, x, gate)


def kernel(x, c, ctx, c_ctx, ada_w, ada_b, ln_g, ln_b, ffn1_wi, ffn1_wo, ffn2_wi, ffn2_wo,
           w_in, hy_conv_w, hy_conv_b, hy_filt_w1, hy_filt_b1, hy_filt_f1, hy_filt_w2,
           hy_filt_b2, hy_filt_f2, hy_filt_w3, hy_bias, ml_conv_w, ml_conv_b, ml_gate_b,
           ml_norm_w, w_out):
    depth = ada_w.shape[0]
    assert depth == 1, "only the depth-1 block is implemented"
    b, l, d = x.shape
    lc = ctx.shape[1]
    heads = ML_HEADS
    hy_w = hy_bias.shape[2]
    ml_w = ml_norm_w.shape[1]
    dv = ml_w // heads
    ml_qk = ml_conv_w.shape[2] // 2
    dqk = ml_qk // heads
    p_hy = 3 * hy_w
    p_state0 = p_hy + ml_qk + ml_w
    p_main = p_state0 + ml_qk + ml_w
    n_gates = 4 * heads
    alpha = (2.0 * depth) ** 0.25
    assert l % GRID_W == 0 and l % lc == 0 and lc % GRID_W == 0

    tm = min(512, l)
    tm_ctx = min(512, b * lc)
    tm_ffn = min(1024, l)
    tm_ffn_ctx = min(1024, b * lc)

    cvec = jnp.concatenate([c, c_ctx[None], jnp.zeros((V7X_SUBLANES - (b + 1) % V7X_SUBLANES, d), F32)])
    mods = _ada(cvec, ada_w.reshape(d, 9 * d), ada_b.reshape(1, 9 * d), tn=min(512, d))
    mods = mods.reshape(cvec.shape[0], 9, 1, d)
    m_lat = [mods[:b, k] for k in range(9)]
    m_ctx = [mods[b:b + 1, k] for k in range(9)]

    lg = [ln_g[0, k][None] for k in range(3)]
    lb = [ln_b[0, k][None] for k in range(3)]
    bf = lambda a: a.astype(BF16)
    x2 = x.reshape(b * l, d)
    ctx2 = ctx.reshape(b * lc, d)

    wi1, wo1 = bf(ffn1_wi[0]), bf(ffn1_wo[0])
    tf = 256
    x1 = _ffn(x2, lg[0], lb[0], m_lat[0], m_lat[1], m_lat[2], wi1, wo1, lg[0], lb[0],
              rows_per_mod=l, pre="ln", alpha=alpha, tm=tm_ffn, tf=tf)
    c1 = _ffn(ctx2, lg[0], lb[0], m_ctx[0], m_ctx[1], m_ctx[2], wi1, wo1, lg[0], lb[0],
              rows_per_mod=b * lc, pre="ln", alpha=alpha, tm=tm_ffn_ctx, tf=tf)

    w_in0 = w_in[0]
    w_main = bf(w_in0)
    w_gate = bf(jnp.pad(w_in0[:, p_main:], ((0, 0), (0, V7X_LANES - n_gates))))
    q_scale = dqk ** -0.5
    ident = jnp.array([0.0, 1.0, 0.0], F32)[:, None]
    taps = jnp.concatenate([
        hy_conv_w[0], ml_conv_w[0][:, :ml_qk] * q_scale, jnp.tile(ident, (1, ml_w)),
        ml_conv_w[0][:, ml_qk:], jnp.tile(ident, (1, ml_w))], axis=1)
    taps = jnp.pad(taps, ((0, V7X_SUBLANES - 3), (0, 0)))
    tap_b = jnp.concatenate([
        hy_conv_b[0], ml_conv_b[0][:ml_qk] * q_scale, jnp.zeros((ml_w,), F32),
        ml_conv_b[0][ml_qk:], jnp.zeros((ml_w,), F32)])[None]
    tn = min(512, ml_qk)
    p_lat, g_lat = _proj(x1, m_lat[3], m_lat[4], w_main, w_gate, taps, tap_b,
                         rows_per_mod=l, period=GRID_W, tm=tm_ffn, tn=tn)
    v0 = p_state0 + ml_qk
    ctx_cols = lambda a: jnp.concatenate([a[:, v0:p_main], a[:, p_state0:v0]], axis=1)
    p_ctx, g_ctx = _proj(c1, m_ctx[3], m_ctx[4], ctx_cols(w_main), w_gate,
                         ctx_cols(taps), ctx_cols(tap_b),
                         rows_per_mod=b * lc, period=lc, tm=tm_ctx, tn=tn)
    p_lat3 = p_lat.reshape(b, l, p_main)
    p_ctx3 = p_ctx.reshape(b, lc, p_main - p_state0)

    t = jnp.linspace(0.0, 1.0, l, dtype=F32)[:, None]
    w = (2.0 * math.pi / l) * jnp.arange(l, dtype=F32)[:, None]
    bands = jnp.linspace(1e-4, HY_BANDS - 1, HY_BANDS, dtype=F32)[None, :]
    z = jnp.concatenate([t, jnp.cos(bands * w), -jnp.sin(bands * w)], axis=-1)
    ze = V7X_LANES
    z = jnp.pad(z, ((0, 0), (0, ze - z.shape[1])))
    w1 = jnp.pad(hy_filt_w1[0], ((0, ze - hy_filt_w1.shape[1]), (0, 0)))
    max_decay = math.log(HY_DECAY_TARGET) / HY_FAST_PCT
    min_decay = math.log(HY_DECAY_TARGET) / HY_SLOW_PCT
    deltas = jnp.abs(jnp.linspace(min_decay, max_decay, hy_w, dtype=F32))
    deltas = jnp.tile(deltas, 2 * HY_ORDER)[None]
    hraw, asum = _filters(z, w1, hy_filt_b1[0][None], hy_filt_f1[0][None], hy_filt_w2[0],
                          hy_filt_b2[0][None], hy_filt_f2[0][None], hy_filt_w3[0], deltas,
                          tl=min(256, l))
    consts = _dft_constants(l)
    spectra = _spectra(hraw, asum, consts, hy_w, cb=V7X_LANES)
    cb = min(V7X_MXU_DIM, hy_w)
    z1 = _longconv_gate(p_lat3, 0, p_lat3, hy_w, hy_bias[0, 0][None], spectra, 0, consts, hy_w, cb)
    hy = _longconv_gate(z1, 0, p_lat3, 2 * hy_w, hy_bias[0, 1][None], spectra, 1, consts, hy_w, cb)

    gt_lat = jnp.swapaxes(g_lat.reshape(b, l, V7X_LANES)[:, :, :n_gates], 1, 2)
    gt_ctx = jnp.swapaxes(g_ctx.reshape(b, lc, V7X_LANES)[:, :, :n_gates], 1, 2)
    hf, hb = _mlstm(p_lat3, g_lat.reshape(b, l, V7X_LANES), gt_lat,
                    p_ctx3, g_ctx.reshape(b, lc, V7X_LANES), gt_ctx, ml_gate_b[0],
                    q_col=p_hy, k_col=p_state0, v_col=p_state0 + ml_qk,
                    kc_col=ml_w, vc_col=0, heads=heads, dqk=dqk, dv=dv)
    ml = _merge(hf.reshape(b * l, ml_w), hb.reshape(b * l, ml_w), p_lat, p_hy + ml_qk,
                ml_norm_w[0][None], heads, tm)

    y2 = _outp(hy.reshape(b * l, hy_w), ml, bf(w_out[0]), x1, m_lat[5],
               rows_per_mod=l, alpha=alpha, tm=tm, tn=min(1024, d))

    out = _ffn(y2, lg[1], lb[1], m_lat[6], m_lat[7], m_lat[8], bf(ffn2_wi[0]), bf(ffn2_wo[0]),
               lg[2], lb[2], rows_per_mod=l, pre="ln_affine", alpha=alpha, tm=tm_ffn, tf=tf)
    return out.reshape(b, l, d)
```

```python
import functools
import math

import numpy as np
import jax
import jax.numpy as jnp
from jax import lax
from jax.experimental import pallas as pl
from jax.experimental.pallas import tpu as pltpu

GRID_W = 64
ML_HEADS = 4
HY_ORDER = 2
HY_BANDS = 16
HY_DECAY_TARGET = 1e-2
HY_FAST_PCT = 0.3
HY_SLOW_PCT = 1.5
LN_EPS = 1e-5

V7X_VMEM_BYTES = 64 * 1024 * 1024
V7X_LANES = 128
V7X_SUBLANES = 8
V7X_MXU_DIM = 256
VMEM_LIMIT_BYTES = V7X_VMEM_BYTES - 4 * 1024 * 1024

DFT_N2 = 128
DFT_GROUPS = DFT_N2 // V7X_SUBLANES
ROW_CHUNK = 64
FFN_X_DMA_CHUNKS = 4

F32 = jnp.float32
BF16 = jnp.bfloat16


def _params(*semantics):
    return pltpu.CompilerParams(dimension_semantics=semantics, vmem_limit_bytes=VMEM_LIMIT_BYTES)


def _ln(x):
    mu = jnp.mean(x, axis=-1, keepdims=True)
    xc = x - mu
    var = jnp.mean(xc * xc, axis=-1, keepdims=True)
    return xc * lax.rsqrt(var + LN_EPS)


def _silu(x):
    return x * jax.nn.sigmoid(x)


def _ada_kernel(c_ref, w_ref, b_ref, o_ref):
    s = _silu(c_ref[...]).astype(BF16)
    o_ref[...] = jnp.dot(s, w_ref[...].astype(BF16), preferred_element_type=F32) + b_ref[...]


def _ada(cvec, w, b, tn):
    rows, d = cvec.shape
    n = w.shape[1]
    return pl.pallas_call(
        _ada_kernel,
        grid=(n // tn,),
        in_specs=[pl.BlockSpec((rows, d), lambda j: (0, 0)),
                  pl.BlockSpec((d, tn), lambda j: (0, j)),
                  pl.BlockSpec((1, tn), lambda j: (0, j))],
        out_specs=pl.BlockSpec((rows, tn), lambda j: (0, j)),
        out_shape=jax.ShapeDtypeStruct((rows, n), F32),
        compiler_params=_params("arbitrary"),
        name="ada",
    )(cvec, w, b)


def _for_row_chunks(rows, body):
    def step(r, carry):
        body(pl.ds(pl.multiple_of(r * ROW_CHUNK, ROW_CHUNK), ROW_CHUNK))
        return carry
    lax.fori_loop(0, rows // ROW_CHUNK, step, 0)


def _ffn_kernel(x_hbm, pg_ref, pb_ref, sh_ref, sc_ref, gt_ref, wg_ref, wu_ref, wo_ref,
                lg_ref, lb_ref, o_ref, h_ref, sem, *, pre, alpha):
    i = pl.program_id(0)
    f = pl.program_id(1)
    tm = o_ref.shape[0]
    rows = tm // FFN_X_DMA_CHUNKS

    def x_copy(c):
        return pltpu.make_async_copy(x_hbm.at[pl.ds(i * tm + c * rows, rows), :],
                                     o_ref.at[pl.ds(c * rows, rows), :], sem.at[c])

    @pl.when(f == 0)
    def _():
        for c in range(FFN_X_DMA_CHUNKS):
            x_copy(c).start()
        for c in range(FFN_X_DMA_CHUNKS):
            x_copy(c).wait()

            def body(r, carry):
                rs = pl.ds(pl.multiple_of(c * rows + r * ROW_CHUNK, ROW_CHUNK), ROW_CHUNK)
                x = o_ref[rs, :]
                if pre != "none":
                    x = _ln(x)
                if pre == "ln_affine":
                    x = x * pg_ref[...] + pb_ref[...]
                h_ref[rs, :] = (x * (1.0 + sc_ref[...]) + sh_ref[...]).astype(BF16)
                o_ref[rs, :] = alpha * x
                return carry
            lax.fori_loop(0, rows // ROW_CHUNK, body, 0)

    h = h_ref[...]
    g = jnp.dot(h, wg_ref[...].astype(BF16), preferred_element_type=F32)
    u = jnp.dot(h, wu_ref[...].astype(BF16), preferred_element_type=F32)
    a = (_silu(g) * u).astype(BF16)
    o_ref[...] += (0.5 * gt_ref[...]) * jnp.dot(a, wo_ref[...].astype(BF16),
                                                preferred_element_type=F32)

    @pl.when(f == pl.num_programs(1) - 1)
    def _():
        def body(rs):
            o_ref[rs, :] = _ln(o_ref[rs, :]) * lg_ref[...] + lb_ref[...]
        _for_row_chunks(tm, body)


def _ffn(x, pre_g, pre_b, shift, scale, gate, wi, wo, ln_g, ln_b, *, rows_per_mod, pre, alpha,
         tm, tf):
    n, d = x.shape
    ff = wo.shape[0]
    nf = ff // tf
    per = rows_per_mod // tm
    mod_spec = pl.BlockSpec((None, 1, d), lambda i, f: (i // per, 0, 0))
    vec_spec = pl.BlockSpec((1, d), lambda i, f: (0, 0))
    return pl.pallas_call(
        functools.partial(_ffn_kernel, pre=pre, alpha=alpha),
        grid=(n // tm, nf),
        in_specs=[pl.BlockSpec(memory_space=pl.ANY),
                  vec_spec, vec_spec,
                  mod_spec, mod_spec, mod_spec,
                  pl.BlockSpec((d, tf), lambda i, f: (0, f)),
                  pl.BlockSpec((d, tf), lambda i, f: (0, nf + f)),
                  pl.BlockSpec((tf, d), lambda i, f: (f, 0)),
                  vec_spec, vec_spec],
        out_specs=pl.BlockSpec((tm, d), lambda i, f: (i, 0), pipeline_mode=pl.Buffered(1)),
        out_shape=jax.ShapeDtypeStruct((n, d), F32),
        scratch_shapes=[pltpu.VMEM((tm, d), BF16),
                        pltpu.SemaphoreType.DMA((FFN_X_DMA_CHUNKS,))],
        compiler_params=_params("arbitrary", "arbitrary"),
        name="ffn",
    )(x, pre_g, pre_b, shift, scale, gate, wi, wi, wo, ln_g, ln_b)


def _proj_kernel(x_ref, sh_ref, sc_ref, w_ref, wgate_ref, cw_ref, cb_ref, o_ref, og_ref, h_ref,
                 *, period):
    @pl.when(pl.program_id(1) == 0)
    def _():
        def body(rs):
            h_ref[rs, :] = (x_ref[rs, :] * (1.0 + sc_ref[...]) + sh_ref[...]).astype(BF16)
        _for_row_chunks(x_ref.shape[0], body)
        og_ref[...] = jnp.dot(h_ref[...], wgate_ref[...], preferred_element_type=F32)

    r = jnp.dot(h_ref[...], w_ref[...].astype(BF16), preferred_element_type=F32)
    tm = r.shape[0]
    pos = lax.broadcasted_iota(jnp.int32, (tm, 1), 0) % period
    prev = jnp.where(pos == 0, 0.0, pltpu.roll(r, 1, 0))
    nxt = jnp.where(pos == period - 1, 0.0, pltpu.roll(r, tm - 1, 0))
    cw = cw_ref[...]
    o_ref[...] = prev * cw[0:1] + r * cw[1:2] + nxt * cw[2:3] + cb_ref[...]


def _proj(x, shift, scale, w, w_gate, conv_w, conv_b, *, rows_per_mod, period, tm, tn):
    n, d = x.shape
    p = conv_w.shape[1]
    per = rows_per_mod // tm
    mod_spec = pl.BlockSpec((None, 1, d), lambda i, j: (i // per, 0, 0))
    return pl.pallas_call(
        functools.partial(_proj_kernel, period=period),
        grid=(n // tm, p // tn),
        in_specs=[pl.BlockSpec((tm, d), lambda i, j: (i, 0), pipeline_mode=pl.Buffered(1)),
                  mod_spec, mod_spec,
                  pl.BlockSpec((d, tn), lambda i, j: (0, j)),
                  pl.BlockSpec((d, V7X_LANES), lambda i, j: (0, 0)),
                  pl.BlockSpec((V7X_SUBLANES, tn), lambda i, j: (0, j)),
                  pl.BlockSpec((1, tn), lambda i, j: (0, j))],
        out_specs=[pl.BlockSpec((tm, tn), lambda i, j: (i, j)),
                   pl.BlockSpec((tm, V7X_LANES), lambda i, j: (i, 0))],
        out_shape=[jax.ShapeDtypeStruct((n, p), F32),
                   jax.ShapeDtypeStruct((n, V7X_LANES), F32)],
        scratch_shapes=[pltpu.VMEM((tm, d), BF16)],
        compiler_params=_params("arbitrary", "arbitrary"),
        name="proj",
    )(x, shift, scale, w, w_gate, conv_w, conv_b)


def _filt_kernel(z_ref, w1_ref, b1_ref, f1_ref, w2_ref, b2_ref, f2_ref, w3_ref, dl_ref,
                 h_ref, asum_ref):
    hi = lax.Precision.HIGHEST
    z = z_ref[...]
    a = jnp.sin(f1_ref[...] * (jnp.dot(z, w1_ref[...], precision=hi, preferred_element_type=F32)
                               + b1_ref[...]))
    a = jnp.sin(f2_ref[...] * (jnp.dot(a, w2_ref[...], precision=hi, preferred_element_type=F32)
                               + b2_ref[...]))
    h = jnp.dot(a, w3_ref[...], precision=hi, preferred_element_type=F32)
    h = h * jnp.exp(-z[:, 0:1] * dl_ref[...])

    @pl.when(pl.program_id(0) == 0)
    def _():
        asum_ref[...] = jnp.zeros_like(asum_ref)

    h_ref[...] = h
    asum_ref[...] += jnp.sum(jnp.abs(h), axis=0, keepdims=True)


def _filters(z, w1, b1, f1, w2, b2, f2, w3, deltas, tl):
    l, ze = z.shape
    fh = w2.shape[0]
    n = w3.shape[1]
    full = lambda shape: pl.BlockSpec(shape, lambda i: (0, 0))
    return pl.pallas_call(
        _filt_kernel,
        grid=(l // tl,),
        in_specs=[pl.BlockSpec((tl, ze), lambda i: (i, 0)),
                  full((ze, fh)), full((1, fh)), full((1, fh)),
                  full((fh, fh)), full((1, fh)), full((1, fh)),
                  full((fh, n)), full((1, n))],
        out_specs=[pl.BlockSpec((tl, n), lambda i: (i, 0)), full((1, n))],
        out_shape=[jax.ShapeDtypeStruct((l, n), F32), jax.ShapeDtypeStruct((1, n), F32)],
        compiler_params=_params("arbitrary"),
        name="filt",
    )(z, w1, b1, f1, w2, b2, f2, w3, deltas)


@functools.lru_cache(maxsize=None)
def _dft_constants(seq_len):
    n = 2 * seq_len
    nh = n // DFT_N2 // 2
    t1 = np.arange(nh)[None, :]
    k1 = np.arange(nh)[:, None]
    th = 2.0 * np.pi * (t1 * k1) / (2 * nh)
    fwd = np.stack([np.cos(th), -np.sin(th)], axis=1).reshape(2 * nh, nh)
    pair = np.where(k1 == 0, 1.0, 2.0) / n
    inv = np.stack([pair * np.cos(th), -pair * np.sin(th)], axis=1).reshape(2 * nh, nh).T
    eye = np.eye(V7X_SUBLANES)
    fa = np.kron(fwd, eye)
    fa_inv = np.kron(inv, eye)
    t2 = np.arange(DFT_N2)[None, None, :]
    k2 = np.arange(DFT_N2)[None, :, None]
    ph = 2.0 * np.pi * (t2 * k2 / DFT_N2 + t2 * np.arange(nh + 1)[:, None, None] / n)
    mr, mi = np.cos(ph), -np.sin(ph)
    mb = np.concatenate([np.concatenate([mr, -mi], axis=2),
                         np.concatenate([mi, mr], axis=2)], axis=1)
    as_bf16 = lambda a: jnp.asarray(a, dtype=F32).astype(BF16)
    return as_bf16(fa), as_bf16(fa_inv), as_bf16(mb)


def _units_per_step(units):
    return max(k for k in range(1, 12) if units % k == 0)


def _nyquist_sign(nh):
    t1 = lax.broadcasted_iota(jnp.int32, (nh, 1, 1), 0)
    return jnp.where(t1 % 2 == 0, 1.0, -1.0).astype(F32)


def _stage_a(fa_ref, u, s_ref):
    nh = fa_ref.shape[1] // V7X_SUBLANES
    lanes = s_ref.shape[-1]
    sign = _nyquist_sign(nh)

    def body(g, carry):
        rhs = u(g)
        out = jnp.dot(fa_ref[...], rhs.reshape(nh * V7X_SUBLANES, lanes).astype(BF16),
                      preferred_element_type=F32)
        s_ref[0:2 * nh, g, :, :] = out.reshape(2 * nh, V7X_SUBLANES, lanes)
        s_ref[2 * nh, g, :, :] = jnp.sum(rhs * sign, axis=0)
        s_ref[2 * nh + 1, g, :, :] = jnp.zeros((V7X_SUBLANES, lanes), F32)
        return carry
    lax.fori_loop(0, DFT_GROUPS, body, 0, unroll=2)


def _spec_kernel(hf_ref, hb_ref, nf_ref, nb_ref, fa_ref, mb_ref, g_ref, s_ref, *, kb):
    s = pl.program_id(2)
    cb = hf_ref.shape[-1]

    @pl.when(s == 0)
    def _():
        def both(g):
            hf, hb = hf_ref[:, g, :, :], hb_ref[:, g, :, :]
            return jnp.concatenate([hf + hb, hf - hb], axis=-1)
        _stage_a(fa_ref, both, s_ref)

    @pl.when(s > 0)
    def _():
        inv_norm = 1.0 / (nf_ref[...] + nb_ref[...])
        half = DFT_N2
        for i in range(kb):
            k1 = (s - 1) * kb + i
            a = s_ref[pl.ds(2 * k1, 2), :, :, :].reshape(2 * half, 2 * cb).astype(BF16)
            z = jnp.dot(mb_ref[i], a, preferred_element_type=F32)
            g_ref[i, 0:half, :] = (z[0:half, 0:cb] * inv_norm).astype(g_ref.dtype)
            g_ref[i, half:, :] = (z[half:, cb:] * inv_norm).astype(g_ref.dtype)


def _spectra(hraw, asum, consts, hy_w, cb):
    fa, _, mb = consts
    units = mb.shape[0]
    nh = units - 1
    kb = _units_per_step(units)
    ncb = hy_w // cb
    h5 = hraw.reshape(nh, DFT_GROUPS, V7X_SUBLANES, hraw.shape[1])
    blk = (nh, DFT_GROUPS, V7X_SUBLANES, cb)
    kmap = lambda s: jnp.maximum(s - 1, 0)
    return pl.pallas_call(
        functools.partial(_spec_kernel, kb=kb),
        grid=(HY_ORDER, ncb, 1 + units // kb),
        in_specs=[pl.BlockSpec(blk, lambda o, c, s: (0, 0, 0, (2 * o) * ncb + c)),
                  pl.BlockSpec(blk, lambda o, c, s: (0, 0, 0, (2 * o + 1) * ncb + c)),
                  pl.BlockSpec((1, cb), lambda o, c, s: (0, (2 * o) * ncb + c)),
                  pl.BlockSpec((1, cb), lambda o, c, s: (0, (2 * o + 1) * ncb + c)),
                  pl.BlockSpec(fa.shape, lambda o, c, s: (0, 0)),
                  pl.BlockSpec((kb, 2 * DFT_N2, 2 * DFT_N2), lambda o, c, s: (kmap(s), 0, 0))],
        out_specs=pl.BlockSpec((None, kb, 2 * DFT_N2, cb), lambda o, c, s: (o, kmap(s), 0, c)),
        out_shape=jax.ShapeDtypeStruct((HY_ORDER, units, 2 * DFT_N2, hy_w), BF16),
        scratch_shapes=[pltpu.VMEM((2 * units, DFT_GROUPS, V7X_SUBLANES, 2 * cb), F32)],
        compiler_params=_params("arbitrary", "arbitrary", "arbitrary"),
        name="spec",
    )(h5, h5, asum, asum, fa, mb)


def _conv_kernel(u_ref, x_ref, bias_ref, fa_ref, fai_ref, mb_ref, g_ref, o_ref, s_ref, *, kb):
    s = pl.program_id(2)
    last = pl.num_programs(2) - 1
    lanes = s_ref.shape[-1]
    half = DFT_N2
    nh = fa_ref.shape[1] // V7X_SUBLANES

    @pl.when(s == 0)
    def _():
        _stage_a(fa_ref, lambda g: u_ref[:, g, :, :], s_ref)

    @pl.when(jnp.logical_and(s > 0, s < last))
    def _():
        for i in range(kb):
            k1 = (s - 1) * kb + i
            a = s_ref[pl.ds(2 * k1, 2), :, :, :].reshape(2 * half, lanes).astype(BF16)
            z = jnp.dot(mb_ref[i], a, preferred_element_type=F32)
            zr, zi = z[0:half], z[half:]
            gr = g_ref[i, 0:half, :].astype(F32)
            gi = g_ref[i, half:, :].astype(F32)
            p = jnp.concatenate([zr * gr - zi * gi, zr * gi + zi * gr], axis=0).astype(BF16)
            y = lax.dot_general(mb_ref[i], p, (((0,), (0,)), ((), ())), preferred_element_type=F32)
            s_ref[pl.ds(2 * k1, 2), :, :, :] = y.reshape(2, DFT_GROUPS, V7X_SUBLANES, lanes)

    @pl.when(s == last)
    def _():
        n = 2 * nh * DFT_N2
        sign = _nyquist_sign(nh) * (1.0 / n)

        def body(g, carry):
            rhs = s_ref[0:2 * nh, g, :, :].reshape(2 * nh * V7X_SUBLANES, lanes)
            conv = jnp.dot(fai_ref[...], rhs.astype(BF16), preferred_element_type=F32)
            conv = conv.reshape(nh, V7X_SUBLANES, lanes) + sign * s_ref[2 * nh, g, :, :]
            u = u_ref[:, g, :, :]
            o_ref[:, g, :, :] = x_ref[:, g, :, :] * (conv + bias_ref[...] * u)
            return carry
        lax.fori_loop(0, DFT_GROUPS, body, 0, unroll=2)


def _longconv_gate(u_src, u_col, x_src, x_col, bias, spectra, order, consts, hy_w, cb):
    fa, fa_inv, mb = consts
    b, l = u_src.shape[:2]
    units = mb.shape[0]
    nh = units - 1
    kb = _units_per_step(units)
    ncb = hy_w // cb
    view = lambda a: a.reshape(b, nh, DFT_GROUPS, V7X_SUBLANES, a.shape[2])
    blk = (None, nh, DFT_GROUPS, V7X_SUBLANES, cb)
    kmap = lambda s: jnp.clip(s - 1, 0, units // kb - 1)
    out = pl.pallas_call(
        functools.partial(_conv_kernel, kb=kb),
        grid=(ncb, b, 2 + units // kb),
        in_specs=[pl.BlockSpec(blk, lambda c, i, s: (i, 0, 0, 0, u_col // cb + c)),
                  pl.BlockSpec(blk, lambda c, i, s: (i, 0, 0, 0, x_col // cb + c)),
                  pl.BlockSpec((1, cb), lambda c, i, s: (0, c)),
                  pl.BlockSpec(fa.shape, lambda c, i, s: (0, 0)),
                  pl.BlockSpec(fa_inv.shape, lambda c, i, s: (0, 0)),
                  pl.BlockSpec((kb, 2 * DFT_N2, 2 * DFT_N2), lambda c, i, s: (kmap(s), 0, 0)),
                  pl.BlockSpec((None, kb, 2 * DFT_N2, cb), lambda c, i, s: (order, kmap(s), 0, c))],
        out_specs=pl.BlockSpec(blk, lambda c, i, s: (i, 0, 0, 0, c)),
        out_shape=jax.ShapeDtypeStruct((b, nh, DFT_GROUPS, V7X_SUBLANES, hy_w), F32),
        scratch_shapes=[pltpu.VMEM((2 * units, DFT_GROUPS, V7X_SUBLANES, cb), F32)],
        compiler_params=_params("arbitrary", "arbitrary", "arbitrary"),
        name="conv",
    )(view(u_src), view(x_src), bias, fa, fa_inv, mb, spectra)
    return out.reshape(b, l, hy_w)


def _mlstm_dir(q, k, v_aug, ig_col, b_col, ig_row, b_row, b_end, keep, ct_ref, m_ref, idx):
    m = m_ref[idx]
    ct = ct_ref[idx]
    end_row = b_end - b_row + ig_row
    end_col = b_end - b_col + ig_col
    m_new = jnp.maximum(b_end + m, jnp.max(end_row, axis=1, keepdims=True))
    h = None
    if q is not None:
        dlog = jnp.where(keep, b_col - b_row + ig_row, -jnp.inf)
        inter = b_col + m
        m_j = jnp.maximum(inter, jnp.max(dlog, axis=1, keepdims=True))
        s = lax.dot_general(q, k, (((1,), (1,)), ((), ())), preferred_element_type=F32)
        p = (s * jnp.exp(dlog - m_j)).astype(BF16)
        nd = (jnp.dot(p, v_aug, preferred_element_type=F32)
              + jnp.exp(inter - m_j) * jnp.dot(q, ct.astype(BF16), preferred_element_type=F32))
        dv = nd.shape[1] - V7X_LANES
        den = nd[:, dv:dv + 1]
        h = nd[:, :dv] / jnp.maximum(jnp.abs(den), jnp.exp(-m_j))
    w_state = jnp.exp(end_col - m_new)
    decay = jnp.exp(b_end + m - m_new)
    vw = (v_aug.astype(F32) * w_state).astype(BF16)
    ct_ref[idx] = decay * ct + lax.dot_general(k, vw, (((0,), (0,)), ((), ())),
                                               preferred_element_type=F32)
    m_ref[idx] = m_new
    return h


def _mlstm_kernel(qf_ref, kf_ref, vf_ref, gf_ref, gtf_ref,
                  qb_ref, kb_ref, vb_ref, gb_ref, gtb_ref,
                  vc_ref, kc_ref, gc_ref, gtc_ref, bias_col_ref, bias_row_ref,
                  hf_ref, hb_ref, ct_ref, m_ref, *, heads):
    i = pl.program_id(1)
    hi = lax.Precision.HIGHEST
    lc = kc_ref.shape[0]
    dqk = kc_ref.shape[1] // heads
    dv = vc_ref.shape[1] // heads
    ng = 4 * heads
    row = lax.broadcasted_iota(jnp.int32, (lc, lc), 0)
    col = lax.broadcasted_iota(jnp.int32, (lc, lc), 1)
    lower = row >= col
    upper = row <= col
    tri_lo = lower.astype(F32)
    tri_up = upper.astype(F32)
    ones = jnp.ones((lc, V7X_LANES), BF16)

    def run(direction, q_ref, k_ref, v_ref, g_ref, gt_ref, out_ref):
        gates = g_ref[...][:, :ng] + bias_col_ref[...]
        gates_t = gt_ref[...] + bias_row_ref[...]
        tri_col, tri_row, keep = (tri_lo, tri_up, lower) if direction == 0 else (tri_up, tri_lo, upper)
        lf_t = jax.nn.log_sigmoid(gates_t)
        b_cols = jnp.dot(tri_col, jax.nn.log_sigmoid(gates), precision=hi, preferred_element_type=F32)
        b_rows = jnp.dot(lf_t, tri_row, precision=hi, preferred_element_type=F32)
        b_ends = jnp.sum(lf_t, axis=1, keepdims=True)
        for h in range(heads):
            ci = direction * 2 * heads + h
            cf = ci + heads
            k = k_ref[:, h * dqk:(h + 1) * dqk].astype(BF16)
            v_aug = jnp.concatenate([v_ref[:, h * dv:(h + 1) * dv].astype(BF16), ones], axis=1)
            q = None if q_ref is None else q_ref[:, h * dqk:(h + 1) * dqk].astype(BF16)
            out = _mlstm_dir(q, k, v_aug, gates[:, ci:ci + 1], b_cols[:, cf:cf + 1],
                             gates_t[ci:ci + 1, :], b_rows[cf:cf + 1, :], b_ends[cf:cf + 1, :],
                             keep, ct_ref, m_ref, direction * heads + h)
            if out_ref is not None:
                out_ref[:, h * dv:(h + 1) * dv] = out

    @pl.when(i == 0)
    def _():
        ct_ref[...] = jnp.zeros_like(ct_ref)
        m_ref[...] = jnp.zeros_like(m_ref)
        run(0, None, kc_ref, vc_ref, gc_ref, gtc_ref, None)
        run(1, None, kc_ref, vc_ref, gc_ref, gtc_ref, None)

    @pl.when(i > 0)
    def _():
        run(0, qf_ref, kf_ref, vf_ref, gf_ref, gtf_ref, hf_ref)
        run(1, qb_ref, kb_ref, vb_ref, gb_ref, gtb_ref, hb_ref)


def _mlstm(p_lat, g_lat, gt_lat, p_ctx, g_ctx, gt_ctx, gate_b, *, q_col, k_col, v_col,
           kc_col, vc_col, heads, dqk, dv):
    b, l = p_lat.shape[:2]
    lc = p_ctx.shape[1]
    nc = l // lc
    qk_w, v_w = heads * dqk, heads * dv
    ng = 4 * heads
    fwd = lambda i: jnp.maximum(i - 1, 0)
    bwd = lambda i: nc - jnp.maximum(i, 1)

    def lat_specs(chunk):
        return [pl.BlockSpec((None, lc, qk_w), lambda bi, i: (bi, chunk(i), q_col // qk_w)),
                pl.BlockSpec((None, lc, qk_w), lambda bi, i: (bi, chunk(i), k_col // qk_w)),
                pl.BlockSpec((None, lc, v_w), lambda bi, i: (bi, chunk(i), v_col // v_w)),
                pl.BlockSpec((None, lc, V7X_LANES), lambda bi, i: (bi, chunk(i), 0)),
                pl.BlockSpec((None, ng, lc), lambda bi, i: (bi, 0, chunk(i)))]

    ctx_specs = [pl.BlockSpec((None, lc, v_w), lambda bi, i: (bi, 0, vc_col // v_w)),
                 pl.BlockSpec((None, lc, qk_w), lambda bi, i: (bi, 0, kc_col // qk_w)),
                 pl.BlockSpec((None, lc, V7X_LANES), lambda bi, i: (bi, 0, 0)),
                 pl.BlockSpec((None, ng, lc), lambda bi, i: (bi, 0, 0))]
    bias_specs = [pl.BlockSpec((1, ng), lambda bi, i: (0, 0)),
                  pl.BlockSpec((ng, 1), lambda bi, i: (0, 0))]
    out_sds = jax.ShapeDtypeStruct((b, l, v_w), F32)
    return pl.pallas_call(
        functools.partial(_mlstm_kernel, heads=heads),
        grid=(b, nc + 1),
        in_specs=lat_specs(fwd) + lat_specs(bwd) + ctx_specs + bias_specs,
        out_specs=[pl.BlockSpec((None, lc, v_w), lambda bi, i: (bi, fwd(i), 0)),
                   pl.BlockSpec((None, lc, v_w), lambda bi, i: (bi, bwd(i), 0))],
        out_shape=[out_sds, out_sds],
        scratch_shapes=[pltpu.VMEM((2 * heads, dqk, dv + V7X_LANES), F32),
                        pltpu.VMEM((2 * heads, 1, 1), F32)],
        compiler_params=_params("arbitrary", "arbitrary"),
        name="mlstm",
    )(p_lat, p_lat, p_lat, g_lat, gt_lat, p_lat, p_lat, p_lat, g_lat, gt_lat,
      p_ctx, p_ctx, g_ctx, gt_ctx, gate_b.reshape(1, ng), gate_b.reshape(ng, 1))


def _merge_kernel(hf_ref, hb_ref, o_ref, w_ref, out_ref):
    h = _ln(hf_ref[...] + hb_ref[...])
    out_ref[...] = (h * w_ref[...] * jax.nn.sigmoid(o_ref[...])).astype(out_ref.dtype)


def _merge(hf, hb, p_lat, o_col, norm_w, heads, tm):
    n, w = hf.shape
    dv = w // heads
    return pl.pallas_call(
        _merge_kernel,
        grid=(n // tm, heads),
        in_specs=[pl.BlockSpec((tm, dv), lambda i, h: (i, h)),
                  pl.BlockSpec((tm, dv), lambda i, h: (i, h)),
                  pl.BlockSpec((tm, dv), lambda i, h: (i, o_col // dv + h)),
                  pl.BlockSpec((1, dv), lambda i, h: (0, h))],
        out_specs=pl.BlockSpec((tm, dv), lambda i, h: (i, h)),
        out_shape=jax.ShapeDtypeStruct((n, w), BF16),
        compiler_params=_params("arbitrary", "arbitrary"),
        name="merge",
    )(hf, hb, p_lat, norm_w)


def _outp_kernel(hy_ref, ml_ref, w_ref, x_ref, gt_ref, o_ref, a_ref, *, alpha):
    @pl.when(pl.program_id(1) == 0)
    def _():
        ka = hy_ref.shape[1]
        a_ref[:, :ka] = hy_ref[...].astype(BF16)
        a_ref[:, ka:] = ml_ref[...]

    y = jnp.dot(a_ref[...], w_ref[...], preferred_element_type=F32)
    o_ref[...] = alpha * x_ref[...] + gt_ref[...] * y


def _outp(hy, ml, w, x, gate, *, rows_per_mod, alpha, tm, tn):
    n, d = x.shape
    ka, kb = hy.shape[1], ml.shape[1]
    per = rows_per_mod // tm
    return pl.pallas_call(
        functools.partial(_outp_kernel, alpha=alpha),
        grid=(n // tm, d // tn),
        in_specs=[pl.BlockSpec((tm, ka), lambda i, j: (i, 0)),
                  pl.BlockSpec((tm, kb), lambda i, j: (i, 0)),
                  pl.BlockSpec((ka + kb, tn), lambda i, j: (0, j)),
                  pl.BlockSpec((tm, tn), lambda i, j: (i, j)),
                  pl.BlockSpec((None, 1, tn), lambda i, j: (i // per, 0, j))],
        out_specs=pl.BlockSpec((tm, tn), lambda i, j: (i, j)),
        out_shape=jax.ShapeDtypeStruct((n, d), F32),
        scratch_shapes=[pltpu.VMEM((tm, ka + kb), BF16)],
        compiler_params=_params("arbitrary", "arbitrary"),
        name="outp",
    )(hy, ml, w, x, gate)


def kernel(x, c, ctx, c_ctx, ada_w, ada_b, ln_g, ln_b, ffn1_wi, ffn1_wo, ffn2_wi, ffn2_wo,
           w_in, hy_conv_w, hy_conv_b, hy_filt_w1, hy_filt_b1, hy_filt_f1, hy_filt_w2,
           hy_filt_b2, hy_filt_f2, hy_filt_w3, hy_bias, ml_conv_w, ml_conv_b, ml_gate_b,
           ml_norm_w, w_out):
    depth = ada_w.shape[0]
    assert depth == 1, "only the depth-1 block is implemented"
    b, l, d = x.shape
    lc = ctx.shape[1]
    heads = ML_HEADS
    hy_w = hy_bias.shape[2]
    ml_w = ml_norm_w.shape[1]
    dv = ml_w // heads
    ml_qk = ml_conv_w.shape[2] // 2
    dqk = ml_qk // heads
    p_hy = 3 * hy_w
    p_state0 = p_hy + ml_qk + ml_w
    p_main = p_state0 + ml_qk + ml_w
    n_gates = 4 * heads
    alpha = (2.0 * depth) ** 0.25
    assert l % GRID_W == 0 and l % lc == 0 and lc % GRID_W == 0

    tm = min(512, l)
    tm_ctx = min(512, b * lc)
    tm_ffn = min(1024, l)
    tm_ffn_ctx = min(1024, b * lc)

    cvec = jnp.concatenate([c, c_ctx[None], jnp.zeros((V7X_SUBLANES - (b + 1) % V7X_SUBLANES, d), F32)])
    mods = _ada(cvec, ada_w.reshape(d, 9 * d), ada_b.reshape(1, 9 * d), tn=min(512, d))
    mods = mods.reshape(cvec.shape[0], 9, 1, d)
    m_lat = [mods[:b, k] for k in range(9)]
    m_ctx = [mods[b:b + 1, k] for k in range(9)]

    lg = [ln_g[0, k][None] for k in range(3)]
    lb = [ln_b[0, k][None] for k in range(3)]
    bf = lambda a: a.astype(BF16)
    x2 = x.reshape(b * l, d)
    ctx2 = ctx.reshape(b * lc, d)

    wi1, wo1 = ffn1_wi[0], ffn1_wo[0]
    tf = 256
    x1 = _ffn(x2, lg[0], lb[0], m_lat[0], m_lat[1], m_lat[2], wi1, wo1, lg[0], lb[0],
              rows_per_mod=l, pre="ln", alpha=alpha, tm=tm_ffn, tf=tf)
    c1 = _ffn(ctx2, lg[0], lb[0], m_ctx[0], m_ctx[1], m_ctx[2], wi1, wo1, lg[0], lb[0],
              rows_per_mod=b * lc, pre="ln", alpha=alpha, tm=tm_ffn_ctx, tf=tf)

    w_in0 = w_in[0]
    w_main = w_in0
    w_gate = bf(jnp.pad(w_in0[:, p_main:], ((0, 0), (0, V7X_LANES - n_gates))))
    q_scale = dqk ** -0.5
    ident = jnp.array([0.0, 1.0, 0.0], F32)[:, None]
    taps = jnp.concatenate([
        hy_conv_w[0], ml_conv_w[0][:, :ml_qk] * q_scale, jnp.tile(ident, (1, ml_w)),
        ml_conv_w[0][:, ml_qk:], jnp.tile(ident, (1, ml_w))], axis=1)
    taps = jnp.pad(taps, ((0, V7X_SUBLANES - 3), (0, 0)))
    tap_b = jnp.concatenate([
        hy_conv_b[0], ml_conv_b[0][:ml_qk] * q_scale, jnp.zeros((ml_w,), F32),
        ml_conv_b[0][ml_qk:], jnp.zeros((ml_w,), F32)])[None]
    tn = min(512, ml_qk)
    p_lat, g_lat = _proj(x1, m_lat[3], m_lat[4], w_main, w_gate, taps, tap_b,
                         rows_per_mod=l, period=GRID_W, tm=tm_ffn, tn=tn)
    v0 = p_state0 + ml_qk
    ctx_cols = lambda a: jnp.concatenate([a[:, v0:p_main], a[:, p_state0:v0]], axis=1)
    p_ctx, g_ctx = _proj(c1, m_ctx[3], m_ctx[4], ctx_cols(w_main), w_gate,
                         ctx_cols(taps), ctx_cols(tap_b),
                         rows_per_mod=b * lc, period=lc, tm=tm_ctx, tn=tn)
    p_lat3 = p_lat.reshape(b, l, p_main)
    p_ctx3 = p_ctx.reshape(b, lc, p_main - p_state0)

    t = jnp.linspace(0.0, 1.0, l, dtype=F32)[:, None]
    w = (2.0 * math.pi / l) * jnp.arange(l, dtype=F32)[:, None]
    bands = jnp.linspace(1e-4, HY_BANDS - 1, HY_BANDS, dtype=F32)[None, :]
    z = jnp.concatenate([t, jnp.cos(bands * w), -jnp.sin(bands * w)], axis=-1)
    ze = V7X_LANES
    z = jnp.pad(z, ((0, 0), (0, ze - z.shape[1])))
    w1 = jnp.pad(hy_filt_w1[0], ((0, ze - hy_filt_w1.shape[1]), (0, 0)))
    max_decay = math.log(HY_DECAY_TARGET) / HY_FAST_PCT
    min_decay = math.log(HY_DECAY_TARGET) / HY_SLOW_PCT
    deltas = jnp.abs(jnp.linspace(min_decay, max_decay, hy_w, dtype=F32))
    deltas = jnp.tile(deltas, 2 * HY_ORDER)[None]
    hraw, asum = _filters(z, w1, hy_filt_b1[0][None], hy_filt_f1[0][None], hy_filt_w2[0],
                          hy_filt_b2[0][None], hy_filt_f2[0][None], hy_filt_w3[0], deltas,
                          tl=min(256, l))
    consts = _dft_constants(l)
    spectra = _spectra(hraw, asum, consts, hy_w, cb=V7X_LANES)
    cb = min(V7X_MXU_DIM, hy_w)
    z1 = _longconv_gate(p_lat3, 0, p_lat3, hy_w, hy_bias[0, 0][None], spectra, 0, consts, hy_w, cb)
    hy = _longconv_gate(z1, 0, p_lat3, 2 * hy_w, hy_bias[0, 1][None], spectra, 1, consts, hy_w, cb)

    gt_lat = jnp.swapaxes(g_lat.reshape(b, l, V7X_LANES)[:, :, :n_gates], 1, 2)
    gt_ctx = jnp.swapaxes(g_ctx.reshape(b, lc, V7X_LANES)[:, :, :n_gates], 1, 2)
    hf, hb = _mlstm(p_lat3, g_lat.reshape(b, l, V7X_LANES), gt_lat,
                    p_ctx3, g_ctx.reshape(b, lc, V7X_LANES), gt_ctx, ml_gate_b[0],
                    q_col=p_hy, k_col=p_state0, v_col=p_state0 + ml_qk,
                    kc_col=ml_w, vc_col=0, heads=heads, dqk=dqk, dv=dv)
    ml = _merge(hf.reshape(b * l, ml_w), hb.reshape(b * l, ml_w), p_lat, p_hy + ml_qk,
                ml_norm_w[0][None], heads, tm)

    y2 = _outp(hy.reshape(b * l, hy_w), ml, bf(w_out[0]), x1, m_lat[5],
               rows_per_mod=l, alpha=alpha, tm=tm, tn=min(1024, d))

    out = _ffn(y2, lg[1], lb[1], m_lat[6], m_lat[7], m_lat[8], ffn2_wi[0], ffn2_wo[0],
               lg[2], lb[2], rows_per_mod=l, pre="ln_affine", alpha=alpha, tm=tm_ffn, tf=tf)
    return out.reshape(b, l, d)
```

```python
import functools
import math

import numpy as np
import jax
import jax.numpy as jnp
from jax import lax
from jax.experimental import pallas as pl
from jax.experimental.pallas import tpu as pltpu

GRID_W = 64
ML_HEADS = 4
HY_ORDER = 2
HY_BANDS = 16
HY_DECAY_TARGET = 1e-2
HY_FAST_PCT = 0.3
HY_SLOW_PCT = 1.5
LN_EPS = 1e-5

V7X_VMEM_BYTES = 64 * 1024 * 1024
V7X_LANES = 128
V7X_SUBLANES = 8
V7X_MXU_DIM = 256
VMEM_LIMIT_BYTES = V7X_VMEM_BYTES - 4 * 1024 * 1024

DFT_N2 = 128
DFT_GROUPS = DFT_N2 // V7X_SUBLANES
ROW_CHUNK = 64
FFN_X_DMA_CHUNKS = 4

F32 = jnp.float32
BF16 = jnp.bfloat16


def _params(*semantics):
    return pltpu.CompilerParams(dimension_semantics=semantics, vmem_limit_bytes=VMEM_LIMIT_BYTES)


def _ln(x):
    mu = jnp.mean(x, axis=-1, keepdims=True)
    xc = x - mu
    var = jnp.mean(xc * xc, axis=-1, keepdims=True)
    return xc * lax.rsqrt(var + LN_EPS)


def _silu(x):
    return x * jax.nn.sigmoid(x)


def _ada_kernel(c_ref, w_ref, b_ref, o_ref):
    s = _silu(c_ref[...]).astype(BF16)
    o_ref[...] = jnp.dot(s, w_ref[...].astype(BF16), preferred_element_type=F32) + b_ref[...]


def _ada(cvec, w, b, tn):
    rows, d = cvec.shape
    n = w.shape[1]
    return pl.pallas_call(
        _ada_kernel,
        grid=(n // tn,),
        in_specs=[pl.BlockSpec((rows, d), lambda j: (0, 0)),
                  pl.BlockSpec((d, tn), lambda j: (0, j)),
                  pl.BlockSpec((1, tn), lambda j: (0, j))],
        out_specs=pl.BlockSpec((rows, tn), lambda j: (0, j)),
        out_shape=jax.ShapeDtypeStruct((rows, n), F32),
        compiler_params=_params("arbitrary"),
        name="ada",
    )(cvec, w, b)


def _for_row_chunks(rows, body):
    def step(r, carry):
        body(pl.ds(pl.multiple_of(r * ROW_CHUNK, ROW_CHUNK), ROW_CHUNK))
        return carry
    lax.fori_loop(0, rows // ROW_CHUNK, step, 0)


def _ffn_kernel(x_hbm, pg_ref, pb_ref, sh_ref, sc_ref, gt_ref, wg_ref, wu_ref, wo_ref,
                lg_ref, lb_ref, o_ref, h_ref, sem, *, pre, alpha):
    i = pl.program_id(0)
    f = pl.program_id(1)
    tm = o_ref.shape[0]
    rows = tm // FFN_X_DMA_CHUNKS

    def x_copy(c):
        return pltpu.make_async_copy(x_hbm.at[pl.ds(i * tm + c * rows, rows), :],
                                     o_ref.at[pl.ds(c * rows, rows), :], sem.at[c])

    @pl.when(f == 0)
    def _():
        for c in range(FFN_X_DMA_CHUNKS):
            x_copy(c).start()
        for c in range(FFN_X_DMA_CHUNKS):
            x_copy(c).wait()

            def body(r, carry):
                rs = pl.ds(pl.multiple_of(c * rows + r * ROW_CHUNK, ROW_CHUNK), ROW_CHUNK)
                x = o_ref[rs, :]
                if pre != "none":
                    x = _ln(x)
                if pre == "ln_affine":
                    x = x * pg_ref[...] + pb_ref[...]
                h_ref[rs, :] = (x * (1.0 + sc_ref[...]) + sh_ref[...]).astype(BF16)
                o_ref[rs, :] = alpha * x
                return carry
            lax.fori_loop(0, rows // ROW_CHUNK, body, 0)

    h = h_ref[...]
    g = jnp.dot(h, wg_ref[...].astype(BF16), preferred_element_type=F32)
    u = jnp.dot(h, wu_ref[...].astype(BF16), preferred_element_type=F32)
    a = (_silu(g) * u).astype(BF16)
    o_ref[...] += (0.5 * gt_ref[...]) * jnp.dot(a, wo_ref[...].astype(BF16),
                                                preferred_element_type=F32)

    @pl.when(f == pl.num_programs(1) - 1)
    def _():
        def body(rs):
            o_ref[rs, :] = _ln(o_ref[rs, :]) * lg_ref[...] + lb_ref[...]
        _for_row_chunks(tm, body)


def _ffn(x, pre_g, pre_b, shift, scale, gate, wi, wo, ln_g, ln_b, *, rows_per_mod, pre, alpha,
         tm, tf):
    n, d = x.shape
    ff = wo.shape[0]
    nf = ff // tf
    per = rows_per_mod // tm
    mod_spec = pl.BlockSpec((None, 1, d), lambda i, f: (i // per, 0, 0))
    vec_spec = pl.BlockSpec((1, d), lambda i, f: (0, 0))
    return pl.pallas_call(
        functools.partial(_ffn_kernel, pre=pre, alpha=alpha),
        grid=(n // tm, nf),
        in_specs=[pl.BlockSpec(memory_space=pl.ANY),
                  vec_spec, vec_spec,
                  mod_spec, mod_spec, mod_spec,
                  pl.BlockSpec((d, tf), lambda i, f: (0, f)),
                  pl.BlockSpec((d, tf), lambda i, f: (0, nf + f)),
                  pl.BlockSpec((tf, d), lambda i, f: (f, 0)),
                  vec_spec, vec_spec],
        out_specs=pl.BlockSpec((tm, d), lambda i, f: (i, 0), pipeline_mode=pl.Buffered(1)),
        out_shape=jax.ShapeDtypeStruct((n, d), F32),
        scratch_shapes=[pltpu.VMEM((tm, d), BF16),
                        pltpu.SemaphoreType.DMA((FFN_X_DMA_CHUNKS,))],
        compiler_params=_params("arbitrary", "arbitrary"),
        name="ffn",
    )(x, pre_g, pre_b, shift, scale, gate, wi, wi, wo, ln_g, ln_b)


def _proj_kernel(x_ref, sh_ref, sc_ref, w_ref, wgate_ref, cw_ref, cb_ref, o_ref, og_ref, h_ref,
                 *, period):
    @pl.when(pl.program_id(1) == 0)
    def _():
        def body(rs):
            h_ref[rs, :] = (x_ref[rs, :] * (1.0 + sc_ref[...]) + sh_ref[...]).astype(BF16)
        _for_row_chunks(x_ref.shape[0], body)
        og_ref[...] = jnp.dot(h_ref[...], wgate_ref[...], preferred_element_type=F32)

    r = jnp.dot(h_ref[...], w_ref[...].astype(BF16), preferred_element_type=F32)
    tm = r.shape[0]
    pos = lax.broadcasted_iota(jnp.int32, (tm, 1), 0) % period
    prev = jnp.where(pos == 0, 0.0, pltpu.roll(r, 1, 0))
    nxt = jnp.where(pos == period - 1, 0.0, pltpu.roll(r, tm - 1, 0))
    cw = cw_ref[...]
    o_ref[...] = prev * cw[0:1] + r * cw[1:2] + nxt * cw[2:3] + cb_ref[...]


def _proj(x, shift, scale, w, w_gate, conv_w, conv_b, *, rows_per_mod, period, tm, tn):
    n, d = x.shape
    p = conv_w.shape[1]
    per = rows_per_mod // tm
    mod_spec = pl.BlockSpec((None, 1, d), lambda i, j: (i // per, 0, 0))
    return pl.pallas_call(
        functools.partial(_proj_kernel, period=period),
        grid=(n // tm, p // tn),
        in_specs=[pl.BlockSpec((tm, d), lambda i, j: (i, 0), pipeline_mode=pl.Buffered(1)),
                  mod_spec, mod_spec,
                  pl.BlockSpec((None, d, tn), lambda i, j: (0, 0, j)),
                  pl.BlockSpec((d, V7X_LANES), lambda i, j: (0, 0)),
                  pl.BlockSpec((V7X_SUBLANES, tn), lambda i, j: (0, j)),
                  pl.BlockSpec((1, tn), lambda i, j: (0, j))],
        out_specs=[pl.BlockSpec((tm, tn), lambda i, j: (i, j)),
                   pl.BlockSpec((tm, V7X_LANES), lambda i, j: (i, 0))],
        out_shape=[jax.ShapeDtypeStruct((n, p), F32),
                   jax.ShapeDtypeStruct((n, V7X_LANES), F32)],
        scratch_shapes=[pltpu.VMEM((tm, d), BF16)],
        compiler_params=_params("arbitrary", "arbitrary"),
        name="proj",
    )(x, shift, scale, w, w_gate, conv_w, conv_b)


def _filt_kernel(z_ref, w1_ref, b1_ref, f1_ref, w2_ref, b2_ref, f2_ref, w3_ref, dl_ref,
                 h_ref, asum_ref):
    hi = lax.Precision.HIGHEST
    z = z_ref[...]
    a = jnp.sin(f1_ref[...] * (jnp.dot(z, w1_ref[...], precision=hi, preferred_element_type=F32)
                               + b1_ref[...]))
    a = jnp.sin(f2_ref[...] * (jnp.dot(a, w2_ref[...], precision=hi, preferred_element_type=F32)
                               + b2_ref[...]))
    h = jnp.dot(a, w3_ref[...], precision=hi, preferred_element_type=F32)
    h = h * jnp.exp(-z[:, 0:1] * dl_ref[...])

    @pl.when(pl.program_id(0) == 0)
    def _():
        asum_ref[...] = jnp.zeros_like(asum_ref)

    h_ref[...] = h
    asum_ref[...] += jnp.sum(jnp.abs(h), axis=0, keepdims=True)


def _filters(z, w1, b1, f1, w2, b2, f2, w3, deltas, tl):
    l, ze = z.shape
    fh = w2.shape[0]
    n = w3.shape[1]
    full = lambda shape: pl.BlockSpec(shape, lambda i: (0, 0))
    return pl.pallas_call(
        _filt_kernel,
        grid=(l // tl,),
        in_specs=[pl.BlockSpec((tl, ze), lambda i: (i, 0)),
                  full((ze, fh)), full((1, fh)), full((1, fh)),
                  full((fh, fh)), full((1, fh)), full((1, fh)),
                  full((fh, n)), full((1, n))],
        out_specs=[pl.BlockSpec((tl, n), lambda i: (i, 0)), full((1, n))],
        out_shape=[jax.ShapeDtypeStruct((l, n), F32), jax.ShapeDtypeStruct((1, n), F32)],
        compiler_params=_params("arbitrary"),
        name="filt",
    )(z, w1, b1, f1, w2, b2, f2, w3, deltas)


@functools.lru_cache(maxsize=None)
def _dft_constants(seq_len):
    n = 2 * seq_len
    nh = n // DFT_N2 // 2
    t1 = np.arange(nh)[None, :]
    k1 = np.arange(nh)[:, None]
    th = 2.0 * np.pi * (t1 * k1) / (2 * nh)
    fwd = np.stack([np.cos(th), -np.sin(th)], axis=1).reshape(2 * nh, nh)
    pair = np.where(k1 == 0, 1.0, 2.0) / n
    inv = np.stack([pair * np.cos(th), -pair * np.sin(th)], axis=1).reshape(2 * nh, nh).T
    eye = np.eye(V7X_SUBLANES)
    fa = np.kron(fwd, eye)
    fa_inv = np.kron(inv, eye)
    t2 = np.arange(DFT_N2)[None, None, :]
    k2 = np.arange(DFT_N2)[None, :, None]
    ph = 2.0 * np.pi * (t2 * k2 / DFT_N2 + t2 * np.arange(nh + 1)[:, None, None] / n)
    mr, mi = np.cos(ph), -np.sin(ph)
    mb = np.concatenate([np.concatenate([mr, -mi], axis=2),
                         np.concatenate([mi, mr], axis=2)], axis=1)
    as_bf16 = lambda a: jnp.asarray(a, dtype=F32).astype(BF16)
    return as_bf16(fa), as_bf16(fa_inv), as_bf16(mb)


def _units_per_step(units):
    return max(k for k in range(1, 12) if units % k == 0)


def _nyquist_sign(nh):
    t1 = lax.broadcasted_iota(jnp.int32, (nh, 1, 1), 0)
    return jnp.where(t1 % 2 == 0, 1.0, -1.0).astype(F32)


def _stage_a(fa_ref, u, s_ref):
    nh = fa_ref.shape[1] // V7X_SUBLANES
    lanes = s_ref.shape[-1]
    sign = _nyquist_sign(nh)

    def body(g, carry):
        rhs = u(g)
        out = jnp.dot(fa_ref[...], rhs.reshape(nh * V7X_SUBLANES, lanes).astype(BF16),
                      preferred_element_type=F32)
        s_ref[0:2 * nh, g, :, :] = out.reshape(2 * nh, V7X_SUBLANES, lanes)
        s_ref[2 * nh, g, :, :] = jnp.sum(rhs * sign, axis=0)
        s_ref[2 * nh + 1, g, :, :] = jnp.zeros((V7X_SUBLANES, lanes), F32)
        return carry
    lax.fori_loop(0, DFT_GROUPS, body, 0, unroll=True)


def _spec_kernel(hf_ref, hb_ref, nf_ref, nb_ref, fa_ref, mb_ref, g_ref, s_ref, *, kb):
    s = pl.program_id(2)
    cb = hf_ref.shape[-1]

    @pl.when(s == 0)
    def _():
        def both(g):
            hf, hb = hf_ref[:, g, :, :], hb_ref[:, g, :, :]
            return jnp.concatenate([hf + hb, hf - hb], axis=-1)
        _stage_a(fa_ref, both, s_ref)

    @pl.when(s > 0)
    def _():
        inv_norm = 1.0 / (nf_ref[...] + nb_ref[...])
        half = DFT_N2
        for i in range(kb):
            k1 = (s - 1) * kb + i
            a = s_ref[pl.ds(2 * k1, 2), :, :, :].reshape(2 * half, 2 * cb).astype(BF16)
            z = jnp.dot(mb_ref[i], a, preferred_element_type=F32)
            g_ref[i, 0:half, :] = (z[0:half, 0:cb] * inv_norm).astype(g_ref.dtype)
            g_ref[i, half:, :] = (z[half:, cb:] * inv_norm).astype(g_ref.dtype)


def _spectra(hraw, asum, consts, hy_w, cb):
    fa, _, mb = consts
    units = mb.shape[0]
    nh = units - 1
    kb = _units_per_step(units)
    ncb = hy_w // cb
    h5 = hraw.reshape(nh, DFT_GROUPS, V7X_SUBLANES, hraw.shape[1])
    blk = (nh, DFT_GROUPS, V7X_SUBLANES, cb)
    kmap = lambda s: jnp.maximum(s - 1, 0)
    return pl.pallas_call(
        functools.partial(_spec_kernel, kb=kb),
        grid=(HY_ORDER, ncb, 1 + units // kb),
        in_specs=[pl.BlockSpec(blk, lambda o, c, s: (0, 0, 0, (2 * o) * ncb + c)),
                  pl.BlockSpec(blk, lambda o, c, s: (0, 0, 0, (2 * o + 1) * ncb + c)),
                  pl.BlockSpec((1, cb), lambda o, c, s: (0, (2 * o) * ncb + c)),
                  pl.BlockSpec((1, cb), lambda o, c, s: (0, (2 * o + 1) * ncb + c)),
                  pl.BlockSpec(fa.shape, lambda o, c, s: (0, 0)),
                  pl.BlockSpec((kb, 2 * DFT_N2, 2 * DFT_N2), lambda o, c, s: (kmap(s), 0, 0))],
        out_specs=pl.BlockSpec((None, kb, 2 * DFT_N2, cb), lambda o, c, s: (o, kmap(s), 0, c)),
        out_shape=jax.ShapeDtypeStruct((HY_ORDER, units, 2 * DFT_N2, hy_w), BF16),
        scratch_shapes=[pltpu.VMEM((2 * units, DFT_GROUPS, V7X_SUBLANES, 2 * cb), F32)],
        compiler_params=_params("arbitrary", "arbitrary", "arbitrary"),
        name="spec",
    )(h5, h5, asum, asum, fa, mb)


def _conv_kernel(u_ref, x_ref, bias_ref, fa_ref, fai_ref, mb_ref, g_ref, o_ref, s_ref, *, kb):
    s = pl.program_id(2)
    last = pl.num_programs(2) - 1
    lanes = s_ref.shape[-1]
    half = DFT_N2
    nh = fa_ref.shape[1] // V7X_SUBLANES

    @pl.when(s == 0)
    def _():
        _stage_a(fa_ref, lambda g: u_ref[:, g, :, :], s_ref)

    @pl.when(jnp.logical_and(s > 0, s < last))
    def _():
        for i in range(kb):
            k1 = (s - 1) * kb + i
            a = s_ref[pl.ds(2 * k1, 2), :, :, :].reshape(2 * half, lanes).astype(BF16)
            z = jnp.dot(mb_ref[i], a, preferred_element_type=F32)
            zr, zi = z[0:half], z[half:]
            gr = g_ref[i, 0:half, :].astype(F32)
            gi = g_ref[i, half:, :].astype(F32)
            p = jnp.concatenate([zr * gr - zi * gi, zr * gi + zi * gr], axis=0).astype(BF16)
            y = lax.dot_general(mb_ref[i], p, (((0,), (0,)), ((), ())), preferred_element_type=F32)
            s_ref[pl.ds(2 * k1, 2), :, :, :] = y.reshape(2, DFT_GROUPS, V7X_SUBLANES, lanes)

    @pl.when(s == last)
    def _():
        n = 2 * nh * DFT_N2
        sign = _nyquist_sign(nh) * (1.0 / n)

        def body(g, carry):
            rhs = s_ref[0:2 * nh, g, :, :].reshape(2 * nh * V7X_SUBLANES, lanes)
            conv = jnp.dot(fai_ref[...], rhs.astype(BF16), preferred_element_type=F32)
            conv = conv.reshape(nh, V7X_SUBLANES, lanes) + sign * s_ref[2 * nh, g, :, :]
            u = u_ref[:, g, :, :]
            o_ref[:, g, :, :] = x_ref[:, g, :, :] * (conv + bias_ref[...] * u)
            return carry
        lax.fori_loop(0, DFT_GROUPS, body, 0, unroll=True)


def _longconv_gate(u_src, u_col, x_src, x_col, bias, spectra, order, consts, hy_w, cb):
    fa, fa_inv, mb = consts
    b, l = u_src.shape[:2]
    units = mb.shape[0]
    nh = units - 1
    kb = _units_per_step(units)
    ncb = hy_w // cb
    view = lambda a: a.reshape(b, nh, DFT_GROUPS, V7X_SUBLANES, a.shape[2])
    blk = (None, nh, DFT_GROUPS, V7X_SUBLANES, cb)
    kmap = lambda s: jnp.clip(s - 1, 0, units // kb - 1)
    out = pl.pallas_call(
        functools.partial(_conv_kernel, kb=kb),
        grid=(ncb, b, 2 + units // kb),
        in_specs=[pl.BlockSpec(blk, lambda c, i, s: (i, 0, 0, 0, u_col // cb + c)),
                  pl.BlockSpec(blk, lambda c, i, s: (i, 0, 0, 0, x_col // cb + c)),
                  pl.BlockSpec((1, cb), lambda c, i, s: (0, c)),
                  pl.BlockSpec(fa.shape, lambda c, i, s: (0, 0)),
                  pl.BlockSpec(fa_inv.shape, lambda c, i, s: (0, 0)),
                  pl.BlockSpec((kb, 2 * DFT_N2, 2 * DFT_N2), lambda c, i, s: (kmap(s), 0, 0)),
                  pl.BlockSpec((None, kb, 2 * DFT_N2, cb), lambda c, i, s: (order, kmap(s), 0, c))],
        out_specs=pl.BlockSpec(blk, lambda c, i, s: (i, 0, 0, 0, c)),
        out_shape=jax.ShapeDtypeStruct((b, nh, DFT_GROUPS, V7X_SUBLANES, hy_w), F32),
        scratch_shapes=[pltpu.VMEM((2 * units, DFT_GROUPS, V7X_SUBLANES, cb), F32)],
        compiler_params=_params("arbitrary", "arbitrary", "arbitrary"),
        name="conv",
    )(view(u_src), view(x_src), bias, fa, fa_inv, mb, spectra)
    return out.reshape(b, l, hy_w)


def _mlstm_dir(q, k, v_aug, ig_col, b_col, ig_row, b_row, b_end, keep, ct_ref, m_ref, idx):
    m = m_ref[idx]
    ct = ct_ref[idx]
    end_row = b_end - b_row + ig_row
    end_col = b_end - b_col + ig_col
    m_new = jnp.maximum(b_end + m, jnp.max(end_row, axis=1, keepdims=True))
    h = None
    if q is not None:
        dlog = jnp.where(keep, b_col - b_row + ig_row, -jnp.inf)
        inter = b_col + m
        m_j = jnp.maximum(inter, jnp.max(dlog, axis=1, keepdims=True))
        s = lax.dot_general(q, k, (((1,), (1,)), ((), ())), preferred_element_type=F32)
        p = (s * jnp.exp(dlog - m_j)).astype(BF16)
        nd = (jnp.dot(p, v_aug, preferred_element_type=F32)
              + jnp.exp(inter - m_j) * jnp.dot(q, ct.astype(BF16), preferred_element_type=F32))
        dv = nd.shape[1] - V7X_LANES
        den = nd[:, dv:dv + 1]
        h = nd[:, :dv] / jnp.maximum(jnp.abs(den), jnp.exp(-m_j))
    w_state = jnp.exp(end_col - m_new)
    decay = jnp.exp(b_end + m - m_new)
    vw = (v_aug.astype(F32) * w_state).astype(BF16)
    ct_ref[idx] = decay * ct + lax.dot_general(k, vw, (((0,), (0,)), ((), ())),
                                               preferred_element_type=F32)
    m_ref[idx] = m_new
    return h


def _mlstm_kernel(qf_ref, kf_ref, vf_ref, gf_ref, gtf_ref,
                  qb_ref, kb_ref, vb_ref, gb_ref, gtb_ref,
                  vc_ref, kc_ref, gc_ref, gtc_ref, bias_col_ref, bias_row_ref,
                  hf_ref, hb_ref, ct_ref, m_ref, *, heads):
    i = pl.program_id(1)
    hi = lax.Precision.HIGHEST
    lc = kc_ref.shape[0]
    dqk = kc_ref.shape[1] // heads
    dv = vc_ref.shape[1] // heads
    ng = 4 * heads
    row = lax.broadcasted_iota(jnp.int32, (lc, lc), 0)
    col = lax.broadcasted_iota(jnp.int32, (lc, lc), 1)
    lower = row >= col
    upper = row <= col
    tri_lo = lower.astype(F32)
    tri_up = upper.astype(F32)
    ones = jnp.ones((lc, V7X_LANES), BF16)

    def run(direction, q_ref, k_ref, v_ref, g_ref, gt_ref, out_ref):
        gates = g_ref[...][:, :ng] + bias_col_ref[...]
        gates_t = gt_ref[...] + bias_row_ref[...]
        tri_col, tri_row, keep = (tri_lo, tri_up, lower) if direction == 0 else (tri_up, tri_lo, upper)
        lf_t = jax.nn.log_sigmoid(gates_t)
        b_cols = jnp.dot(tri_col, jax.nn.log_sigmoid(gates), precision=hi, preferred_element_type=F32)
        b_rows = jnp.dot(lf_t, tri_row, precision=hi, preferred_element_type=F32)
        b_ends = jnp.sum(lf_t, axis=1, keepdims=True)
        for h in range(heads):
            ci = direction * 2 * heads + h
            cf = ci + heads
            k = k_ref[:, h * dqk:(h + 1) * dqk].astype(BF16)
            v_aug = jnp.concatenate([v_ref[:, h * dv:(h + 1) * dv].astype(BF16), ones], axis=1)
            q = None if q_ref is None else q_ref[:, h * dqk:(h + 1) * dqk].astype(BF16)
            out = _mlstm_dir(q, k, v_aug, gates[:, ci:ci + 1], b_cols[:, cf:cf + 1],
                             gates_t[ci:ci + 1, :], b_rows[cf:cf + 1, :], b_ends[cf:cf + 1, :],
                             keep, ct_ref, m_ref, direction * heads + h)
            if out_ref is not None:
                out_ref[:, h * dv:(h + 1) * dv] = out

    @pl.when(i == 0)
    def _():
        ct_ref[...] = jnp.zeros_like(ct_ref)
        m_ref[...] = jnp.zeros_like(m_ref)
        run(0, None, kc_ref, vc_ref, gc_ref, gtc_ref, None)
        run(1, None, kc_ref, vc_ref, gc_ref, gtc_ref, None)

    @pl.when(i > 0)
    def _():
        run(0, qf_ref, kf_ref, vf_ref, gf_ref, gtf_ref, hf_ref)
        run(1, qb_ref, kb_ref, vb_ref, gb_ref, gtb_ref, hb_ref)


def _mlstm(p_lat, g_lat, gt_lat, p_ctx, g_ctx, gt_ctx, gate_b, *, q_col, k_col, v_col,
           kc_col, vc_col, heads, dqk, dv):
    b, l = p_lat.shape[:2]
    lc = p_ctx.shape[1]
    nc = l // lc
    qk_w, v_w = heads * dqk, heads * dv
    ng = 4 * heads
    fwd = lambda i: jnp.maximum(i - 1, 0)
    bwd = lambda i: nc - jnp.maximum(i, 1)

    def lat_specs(chunk):
        return [pl.BlockSpec((None, lc, qk_w), lambda bi, i: (bi, chunk(i), q_col // qk_w)),
                pl.BlockSpec((None, lc, qk_w), lambda bi, i: (bi, chunk(i), k_col // qk_w)),
                pl.BlockSpec((None, lc, v_w), lambda bi, i: (bi, chunk(i), v_col // v_w)),
                pl.BlockSpec((None, lc, V7X_LANES), lambda bi, i: (bi, chunk(i), 0)),
                pl.BlockSpec((None, ng, lc), lambda bi, i: (bi, 0, chunk(i)))]

    ctx_specs = [pl.BlockSpec((None, lc, v_w), lambda bi, i: (bi, 0, vc_col // v_w)),
                 pl.BlockSpec((None, lc, qk_w), lambda bi, i: (bi, 0, kc_col // qk_w)),
                 pl.BlockSpec((None, lc, V7X_LANES), lambda bi, i: (bi, 0, 0)),
                 pl.BlockSpec((None, ng, lc), lambda bi, i: (bi, 0, 0))]
    bias_specs = [pl.BlockSpec((1, ng), lambda bi, i: (0, 0)),
                  pl.BlockSpec((ng, 1), lambda bi, i: (0, 0))]
    out_sds = jax.ShapeDtypeStruct((b, l, v_w), F32)
    return pl.pallas_call(
        functools.partial(_mlstm_kernel, heads=heads),
        grid=(b, nc + 1),
        in_specs=lat_specs(fwd) + lat_specs(bwd) + ctx_specs + bias_specs,
        out_specs=[pl.BlockSpec((None, lc, v_w), lambda bi, i: (bi, fwd(i), 0)),
                   pl.BlockSpec((None, lc, v_w), lambda bi, i: (bi, bwd(i), 0))],
        out_shape=[out_sds, out_sds],
        scratch_shapes=[pltpu.VMEM((2 * heads, dqk, dv + V7X_LANES), F32),
                        pltpu.VMEM((2 * heads, 1, 1), F32)],
        compiler_params=_params("arbitrary", "arbitrary"),
        name="mlstm",
    )(p_lat, p_lat, p_lat, g_lat, gt_lat, p_lat, p_lat, p_lat, g_lat, gt_lat,
      p_ctx, p_ctx, g_ctx, gt_ctx, gate_b.reshape(1, ng), gate_b.reshape(ng, 1))


def _merge_kernel(hf_ref, hb_ref, o_ref, w_ref, out_ref):
    h = _ln(hf_ref[...] + hb_ref[...])
    out_ref[...] = (h * w_ref[...] * jax.nn.sigmoid(o_ref[...])).astype(out_ref.dtype)


def _merge(hf, hb, p_lat, o_col, norm_w, heads, tm):
    n, w = hf.shape
    dv = w // heads
    return pl.pallas_call(
        _merge_kernel,
        grid=(n // tm, heads),
        in_specs=[pl.BlockSpec((tm, dv), lambda i, h: (i, h)),
                  pl.BlockSpec((tm, dv), lambda i, h: (i, h)),
                  pl.BlockSpec((tm, dv), lambda i, h: (i, o_col // dv + h)),
                  pl.BlockSpec((1, dv), lambda i, h: (0, h))],
        out_specs=pl.BlockSpec((tm, dv), lambda i, h: (i, h)),
        out_shape=jax.ShapeDtypeStruct((n, w), BF16),
        compiler_params=_params("arbitrary", "arbitrary"),
        name="merge",
    )(hf, hb, p_lat, norm_w)


def _outp_kernel(hy_ref, ml_ref, w_ref, x_ref, gt_ref, o_ref, a_ref, *, alpha):
    @pl.when(pl.program_id(1) == 0)
    def _():
        ka = hy_ref.shape[1]
        a_ref[:, :ka] = hy_ref[...].astype(BF16)
        a_ref[:, ka:] = ml_ref[...]

    y = jnp.dot(a_ref[...], w_ref[...], preferred_element_type=F32)
    o_ref[...] = alpha * x_ref[...] + gt_ref[...] * y


def _outp(hy, ml, w, x, gate, *, rows_per_mod, alpha, tm, tn):
    n, d = x.shape
    ka, kb = hy.shape[1], ml.shape[1]
    per = rows_per_mod // tm
    return pl.pallas_call(
        functools.partial(_outp_kernel, alpha=alpha),
        grid=(n // tm, d // tn),
        in_specs=[pl.BlockSpec((tm, ka), lambda i, j: (i, 0)),
                  pl.BlockSpec((tm, kb), lambda i, j: (i, 0)),
                  pl.BlockSpec((ka + kb, tn), lambda i, j: (0, j)),
                  pl.BlockSpec((tm, tn), lambda i, j: (i, j)),
                  pl.BlockSpec((None, 1, tn), lambda i, j: (i // per, 0, j))],
        out_specs=pl.BlockSpec((tm, tn), lambda i, j: (i, j)),
        out_shape=jax.ShapeDtypeStruct((n, d), F32),
        scratch_shapes=[pltpu.VMEM((tm, ka + kb), BF16)],
        compiler_params=_params("arbitrary", "arbitrary"),
        name="outp",
    )(hy, ml, w, x, gate)


def kernel(x, c, ctx, c_ctx, ada_w, ada_b, ln_g, ln_b, ffn1_wi, ffn1_wo, ffn2_wi, ffn2_wo,
           w_in, hy_conv_w, hy_conv_b, hy_filt_w1, hy_filt_b1, hy_filt_f1, hy_filt_w2,
           hy_filt_b2, hy_filt_f2, hy_filt_w3, hy_bias, ml_conv_w, ml_conv_b, ml_gate_b,
           ml_norm_w, w_out):
    depth = ada_w.shape[0]
    assert depth == 1, "only the depth-1 block is implemented"
    b, l, d = x.shape
    lc = ctx.shape[1]
    heads = ML_HEADS
    hy_w = hy_bias.shape[2]
    ml_w = ml_norm_w.shape[1]
    dv = ml_w // heads
    ml_qk = ml_conv_w.shape[2] // 2
    dqk = ml_qk // heads
    p_hy = 3 * hy_w
    p_state0 = p_hy + ml_qk + ml_w
    p_main = p_state0 + ml_qk + ml_w
    n_gates = 4 * heads
    alpha = (2.0 * depth) ** 0.25
    assert l % GRID_W == 0 and l % lc == 0 and lc % GRID_W == 0

    tm = min(512, l)
    tm_ctx = min(512, b * lc)
    tm_ffn = min(1024, l)
    tm_ffn_ctx = min(1024, b * lc)

    cvec = jnp.concatenate([c, c_ctx[None], jnp.zeros((V7X_SUBLANES - (b + 1) % V7X_SUBLANES, d), F32)])
    mods = _ada(cvec, ada_w.reshape(d, 9 * d), ada_b.reshape(1, 9 * d), tn=min(512, d))
    mods = mods.reshape(cvec.shape[0], 9, 1, d)
    m_lat = [mods[:b, k] for k in range(9)]
    m_ctx = [mods[b:b + 1, k] for k in range(9)]

    lg = [ln_g[0, k][None] for k in range(3)]
    lb = [ln_b[0, k][None] for k in range(3)]
    bf = lambda a: a.astype(BF16)
    x2 = x.reshape(b * l, d)
    ctx2 = ctx.reshape(b * lc, d)

    wi1, wo1 = ffn1_wi[0], ffn1_wo[0]
    tf = 256
    x1 = _ffn(x2, lg[0], lb[0], m_lat[0], m_lat[1], m_lat[2], wi1, wo1, lg[0], lb[0],
              rows_per_mod=l, pre="ln", alpha=alpha, tm=tm_ffn, tf=tf)
    c1 = _ffn(ctx2, lg[0], lb[0], m_ctx[0], m_ctx[1], m_ctx[2], wi1, wo1, lg[0], lb[0],
              rows_per_mod=b * lc, pre="ln", alpha=alpha, tm=tm_ffn_ctx, tf=tf)

    w_gate = bf(jnp.pad(w_in[0, :, p_main:], ((0, 0), (0, V7X_LANES - n_gates))))
    q_scale = dqk ** -0.5
    ident = jnp.array([0.0, 1.0, 0.0], F32)[:, None]
    taps = jnp.concatenate([
        hy_conv_w[0], ml_conv_w[0][:, :ml_qk] * q_scale, jnp.tile(ident, (1, ml_w)),
        ml_conv_w[0][:, ml_qk:], jnp.tile(ident, (1, ml_w))], axis=1)
    taps = jnp.pad(taps, ((0, V7X_SUBLANES - 3), (0, 0)))
    tap_b = jnp.concatenate([
        hy_conv_b[0], ml_conv_b[0][:ml_qk] * q_scale, jnp.zeros((ml_w,), F32),
        ml_conv_b[0][ml_qk:], jnp.zeros((ml_w,), F32)])[None]
    tn = min(512, ml_qk)
    p_lat, g_lat = _proj(x1, m_lat[3], m_lat[4], w_in, w_gate, taps, tap_b,
                         rows_per_mod=l, period=GRID_W, tm=tm_ffn, tn=tn)
    v0 = p_state0 + ml_qk
    ctx_cols = lambda a: jnp.concatenate([a[:, v0:p_main], a[:, p_state0:v0]], axis=1)
    p_ctx, g_ctx = _proj(c1, m_ctx[3], m_ctx[4], ctx_cols(w_in[0])[None], w_gate,
                         ctx_cols(taps), ctx_cols(tap_b),
                         rows_per_mod=b * lc, period=lc, tm=tm_ctx, tn=tn)
    p_lat3 = p_lat.reshape(b, l, p_main)
    p_ctx3 = p_ctx.reshape(b, lc, p_main - p_state0)

    t = jnp.linspace(0.0, 1.0, l, dtype=F32)[:, None]
    w = (2.0 * math.pi / l) * jnp.arange(l, dtype=F32)[:, None]
    bands = jnp.linspace(1e-4, HY_BANDS - 1, HY_BANDS, dtype=F32)[None, :]
    z = jnp.concatenate([t, jnp.cos(bands * w), -jnp.sin(bands * w)], axis=-1)
    ze = V7X_LANES
    z = jnp.pad(z, ((0, 0), (0, ze - z.shape[1])))
    w1 = jnp.pad(hy_filt_w1[0], ((0, ze - hy_filt_w1.shape[1]), (0, 0)))
    max_decay = math.log(HY_DECAY_TARGET) / HY_FAST_PCT
    min_decay = math.log(HY_DECAY_TARGET) / HY_SLOW_PCT
    deltas = jnp.abs(jnp.linspace(min_decay, max_decay, hy_w, dtype=F32))
    deltas = jnp.tile(deltas, 2 * HY_ORDER)[None]
    hraw, asum = _filters(z, w1, hy_filt_b1[0][None], hy_filt_f1[0][None], hy_filt_w2[0],
                          hy_filt_b2[0][None], hy_filt_f2[0][None], hy_filt_w3[0], deltas,
                          tl=min(256, l))
    consts = _dft_constants(l)
    spectra = _spectra(hraw, asum, consts, hy_w, cb=V7X_LANES)
    cb = min(V7X_MXU_DIM, hy_w)
    z1 = _longconv_gate(p_lat3, 0, p_lat3, hy_w, hy_bias[0, 0][None], spectra, 0, consts, hy_w, cb)
    hy = _longconv_gate(z1, 0, p_lat3, 2 * hy_w, hy_bias[0, 1][None], spectra, 1, consts, hy_w, cb)

    gt_lat = jnp.swapaxes(g_lat.reshape(b, l, V7X_LANES)[:, :, :n_gates], 1, 2)
    gt_ctx = jnp.swapaxes(g_ctx.reshape(b, lc, V7X_LANES)[:, :, :n_gates], 1, 2)
    hf, hb = _mlstm(p_lat3, g_lat.reshape(b, l, V7X_LANES), gt_lat,
                    p_ctx3, g_ctx.reshape(b, lc, V7X_LANES), gt_ctx, ml_gate_b[0],
                    q_col=p_hy, k_col=p_state0, v_col=p_state0 + ml_qk,
                    kc_col=ml_w, vc_col=0, heads=heads, dqk=dqk, dv=dv)
    ml = _merge(hf.reshape(b * l, ml_w), hb.reshape(b * l, ml_w), p_lat, p_hy + ml_qk,
                ml_norm_w[0][None], heads, tm)

    y2 = _outp(hy.reshape(b * l, hy_w), ml, bf(w_out[0]), x1, m_lat[5],
               rows_per_mod=l, alpha=alpha, tm=tm, tn=min(1024, d))

    out = _ffn(y2, lg[1], lb[1], m_lat[6], m_lat[7], m_lat[8], ffn2_wi[0], ffn2_wo[0],
               lg[2], lb[2], rows_per_mod=l, pre="ln_affine", alpha=alpha, tm=tm_ffn, tf=tf)
    return out.reshape(b, l, d)
```

```python
import functools
import math

import numpy as np
import jax
import jax.numpy as jnp
from jax import lax
from jax.experimental import pallas as pl
from jax.experimental.pallas import tpu as pltpu

GRID_W = 64
ML_HEADS = 4
HY_ORDER = 2
HY_BANDS = 16
HY_DECAY_TARGET = 1e-2
HY_FAST_PCT = 0.3
HY_SLOW_PCT = 1.5
LN_EPS = 1e-5

V7X_VMEM_BYTES = 64 * 1024 * 1024
V7X_LANES = 128
V7X_SUBLANES = 8
V7X_MXU_DIM = 256
VMEM_LIMIT_BYTES = V7X_VMEM_BYTES - 4 * 1024 * 1024

DFT_N2 = 128
DFT_GROUPS = DFT_N2 // V7X_SUBLANES
ROW_CHUNK = 64
ROW_DMA_CHUNKS = 4

F32 = jnp.float32
BF16 = jnp.bfloat16


def _params(*semantics):
    return pltpu.CompilerParams(dimension_semantics=semantics, vmem_limit_bytes=VMEM_LIMIT_BYTES)


def _ln(x):
    mu = jnp.mean(x, axis=-1, keepdims=True)
    xc = x - mu
    var = jnp.mean(xc * xc, axis=-1, keepdims=True)
    return xc * lax.rsqrt(var + LN_EPS)


def _silu(x):
    return x * jax.nn.sigmoid(x)


def _ada_kernel(c_ref, w_ref, b_ref, o_ref):
    s = _silu(c_ref[...]).astype(BF16)
    o_ref[...] = jnp.dot(s, w_ref[...].astype(BF16), preferred_element_type=F32) + b_ref[...]


def _ada(cvec, w, b, tn):
    rows, d = cvec.shape
    n = w.shape[1]
    return pl.pallas_call(
        _ada_kernel,
        grid=(n // tn,),
        in_specs=[pl.BlockSpec((rows, d), lambda j: (0, 0)),
                  pl.BlockSpec((d, tn), lambda j: (0, j)),
                  pl.BlockSpec((1, tn), lambda j: (0, j))],
        out_specs=pl.BlockSpec((rows, tn), lambda j: (0, j)),
        out_shape=jax.ShapeDtypeStruct((rows, n), F32),
        compiler_params=_params("arbitrary"),
        name="ada",
    )(cvec, w, b)


def _for_row_chunks(rows, body):
    def step(r, carry):
        body(pl.ds(pl.multiple_of(r * ROW_CHUNK, ROW_CHUNK), ROW_CHUNK))
        return carry
    lax.fori_loop(0, rows // ROW_CHUNK, step, 0)


def _for_chunk_rows(c, rows, body):
    def step(r, carry):
        body(pl.ds(pl.multiple_of(c * rows + r * ROW_CHUNK, ROW_CHUNK), ROW_CHUNK))
        return carry
    lax.fori_loop(0, rows // ROW_CHUNK, step, 0)


def _ffn_kernel(x_hbm, pg_ref, pb_ref, sh_ref, sc_ref, gt_ref, wg_ref, wu_ref, wo_ref,
                lg_ref, lb_ref, out_hbm, acc_ref, h_ref, in_sem, out_sem, *, pre, alpha):
    i = pl.program_id(0)
    f = pl.program_id(1)
    chunks = ROW_DMA_CHUNKS
    tm = acc_ref.shape[0]
    rows = tm // chunks

    def x_copy(c):
        return pltpu.make_async_copy(x_hbm.at[pl.ds(i * tm + c * rows, rows), :],
                                     acc_ref.at[pl.ds(c * rows, rows), :], in_sem.at[c])

    def out_copy(tile, c):
        return pltpu.make_async_copy(acc_ref.at[pl.ds(c * rows, rows), :],
                                     out_hbm.at[pl.ds(tile * tm + c * rows, rows), :], out_sem.at[c])

    def request(c):
        @pl.when(i > 0)
        def _():
            out_copy(i - 1, c).wait()
        x_copy(c).start()

    def prologue(c):
        x_copy(c).wait()

        def body(rs):
            x = acc_ref[rs, :]
            if pre != "none":
                x = _ln(x)
            if pre == "ln_affine":
                x = x * pg_ref[...] + pb_ref[...]
            h_ref[rs, :] = (x * (1.0 + sc_ref[...]) + sh_ref[...]).astype(BF16)
            acc_ref[rs, :] = alpha * x
        _for_chunk_rows(c, rows, body)

    @pl.when(f == 0)
    def _():
        for c in range(chunks - 1):
            request(c)
        prologue(0)
        request(chunks - 1)
        for c in range(1, chunks):
            prologue(c)

    h = h_ref[...]
    g = jnp.dot(h, wg_ref[...].astype(BF16), preferred_element_type=F32)
    u = jnp.dot(h, wu_ref[...].astype(BF16), preferred_element_type=F32)
    a = (_silu(g) * u).astype(BF16)
    acc_ref[...] += (0.5 * gt_ref[...]) * jnp.dot(a, wo_ref[...].astype(BF16),
                                                  preferred_element_type=F32)

    @pl.when(f == pl.num_programs(1) - 1)
    def _():
        def body(rs):
            acc_ref[rs, :] = _ln(acc_ref[rs, :]) * lg_ref[...] + lb_ref[...]
        for c in range(chunks):
            _for_chunk_rows(c, rows, body)
            out_copy(i, c).start()

        @pl.when(i == pl.num_programs(0) - 1)
        def _():
            for c in range(chunks):
                out_copy(i, c).wait()


def _ffn(x, pre_g, pre_b, shift, scale, gate, wi, wo, ln_g, ln_b, *, rows_per_mod, pre, alpha,
         tm, tf):
    n, d = x.shape
    ff = wo.shape[0]
    nf = ff // tf
    per = rows_per_mod // tm
    mod_spec = pl.BlockSpec((None, 1, d), lambda i, f: (i // per, 0, 0))
    vec_spec = pl.BlockSpec((1, d), lambda i, f: (0, 0))
    return pl.pallas_call(
        functools.partial(_ffn_kernel, pre=pre, alpha=alpha),
        grid=(n // tm, nf),
        in_specs=[pl.BlockSpec(memory_space=pl.ANY),
                  vec_spec, vec_spec,
                  mod_spec, mod_spec, mod_spec,
                  pl.BlockSpec((d, tf), lambda i, f: (0, f)),
                  pl.BlockSpec((d, tf), lambda i, f: (0, nf + f)),
                  pl.BlockSpec((tf, d), lambda i, f: (f, 0)),
                  vec_spec, vec_spec],
        out_specs=pl.BlockSpec(memory_space=pl.ANY),
        out_shape=jax.ShapeDtypeStruct((n, d), F32),
        scratch_shapes=[pltpu.VMEM((tm, d), F32),
                        pltpu.VMEM((tm, d), BF16),
                        pltpu.SemaphoreType.DMA((ROW_DMA_CHUNKS,)),
                        pltpu.SemaphoreType.DMA((ROW_DMA_CHUNKS,))],
        compiler_params=_params("arbitrary", "arbitrary"),
        name="ffn",
    )(x, pre_g, pre_b, shift, scale, gate, wi, wi, wo, ln_g, ln_b)


def _dot_nt(a, b_t):
    return lax.dot_general(a, b_t, (((1,), (1,)), ((), ())), preferred_element_type=F32)


def _proj_kernel(x_hbm, sh_ref, sc_ref, w_ref, wgate_ref, cw_ref, cb_ref, o_ref, og_ref,
                 x_ref, h_ref, sem, *, period):
    i = pl.program_id(0)
    tm = x_ref.shape[0]
    rows = tm // ROW_DMA_CHUNKS

    def x_copy(c):
        return pltpu.make_async_copy(x_hbm.at[pl.ds(i * tm + c * rows, rows), :],
                                     x_ref.at[pl.ds(c * rows, rows), :], sem.at[c])

    @pl.when(pl.program_id(1) == 0)
    def _():
        for c in range(ROW_DMA_CHUNKS):
            x_copy(c).start()

        def body(rs):
            h_ref[rs, :] = (x_ref[rs, :] * (1.0 + sc_ref[...]) + sh_ref[...]).astype(BF16)
        for c in range(ROW_DMA_CHUNKS):
            x_copy(c).wait()
            _for_chunk_rows(c, rows, body)
        og_ref[...] = _dot_nt(h_ref[...], wgate_ref[...].astype(BF16))

    r = _dot_nt(h_ref[...], w_ref[...].astype(BF16))
    pos = lax.broadcasted_iota(jnp.int32, (tm, 1), 0) % period
    prev = jnp.where(pos == 0, 0.0, pltpu.roll(r, 1, 0))
    nxt = jnp.where(pos == period - 1, 0.0, pltpu.roll(r, tm - 1, 0))
    cw = cw_ref[...]
    o_ref[...] = prev * cw[0:1] + r * cw[1:2] + nxt * cw[2:3] + cb_ref[...]


def _proj(x, shift, scale, w_t, w_block, gate_row0, n_gates, conv_w, conv_b, *, rows_per_mod,
          period, tm, tn):
    n, d = x.shape
    p = conv_w.shape[1]
    per = rows_per_mod // tm
    mod_spec = pl.BlockSpec((None, 1, d), lambda i, j: (i // per, 0, 0))
    return pl.pallas_call(
        functools.partial(_proj_kernel, period=period),
        grid=(n // tm, p // tn),
        in_specs=[pl.BlockSpec(memory_space=pl.ANY),
                  mod_spec, mod_spec,
                  pl.BlockSpec((None, tn, d), lambda i, j: (0, w_block(j), 0)),
                  pl.BlockSpec((None, n_gates, d), lambda i, j: (0, gate_row0 // n_gates, 0)),
                  pl.BlockSpec((V7X_SUBLANES, tn), lambda i, j: (0, j)),
                  pl.BlockSpec((1, tn), lambda i, j: (0, j))],
        out_specs=[pl.BlockSpec((tm, tn), lambda i, j: (i, j)),
                   pl.BlockSpec((tm, n_gates), lambda i, j: (i, 0))],
        out_shape=[jax.ShapeDtypeStruct((n, p), F32),
                   jax.ShapeDtypeStruct((n, n_gates), F32)],
        scratch_shapes=[pltpu.VMEM((tm, d), F32),
                        pltpu.VMEM((tm, d), BF16),
                        pltpu.SemaphoreType.DMA((ROW_DMA_CHUNKS,))],
        compiler_params=_params("arbitrary", "arbitrary"),
        name="proj",
    )(x, shift, scale, w_t, w_t, conv_w, conv_b)


def _filt_kernel(z_ref, w1_ref, b1_ref, f1_ref, w2_ref, b2_ref, f2_ref, w3_ref, dl_ref,
                 h_ref, asum_ref):
    hi = lax.Precision.HIGHEST
    z = z_ref[...]
    a = jnp.sin(f1_ref[...] * (jnp.dot(z, w1_ref[...], precision=hi, preferred_element_type=F32)
                               + b1_ref[...]))
    a = jnp.sin(f2_ref[...] * (jnp.dot(a, w2_ref[...], precision=hi, preferred_element_type=F32)
                               + b2_ref[...]))
    h = jnp.dot(a, w3_ref[...], precision=hi, preferred_element_type=F32)
    h = h * jnp.exp(-z[:, 0:1] * dl_ref[...])

    @pl.when(pl.program_id(0) == 0)
    def _():
        asum_ref[...] = jnp.zeros_like(asum_ref)

    h_ref[...] = h
    asum_ref[...] += jnp.sum(jnp.abs(h), axis=0, keepdims=True)


def _filters(z, w1, b1, f1, w2, b2, f2, w3, deltas, tl):
    l, ze = z.shape
    fh = w2.shape[0]
    n = w3.shape[1]
    full = lambda shape: pl.BlockSpec(shape, lambda i: (0, 0))
    return pl.pallas_call(
        _filt_kernel,
        grid=(l // tl,),
        in_specs=[pl.BlockSpec((tl, ze), lambda i: (i, 0)),
                  full((ze, fh)), full((1, fh)), full((1, fh)),
                  full((fh, fh)), full((1, fh)), full((1, fh)),
                  full((fh, n)), full((1, n))],
        out_specs=[pl.BlockSpec((tl, n), lambda i: (i, 0)), full((1, n))],
        out_shape=[jax.ShapeDtypeStruct((l, n), F32), jax.ShapeDtypeStruct((1, n), F32)],
        compiler_params=_params("arbitrary"),
        name="filt",
    )(z, w1, b1, f1, w2, b2, f2, w3, deltas)


@functools.lru_cache(maxsize=None)
def _dft_constants(seq_len):
    n = 2 * seq_len
    nh = n // DFT_N2 // 2
    t1 = np.arange(nh)[None, :]
    k1 = np.arange(nh)[:, None]
    th = 2.0 * np.pi * (t1 * k1) / (2 * nh)
    fwd = np.stack([np.cos(th), -np.sin(th)], axis=1).reshape(2 * nh, nh)
    pair = np.where(k1 == 0, 1.0, 2.0) / n
    inv = np.stack([pair * np.cos(th), -pair * np.sin(th)], axis=1).reshape(2 * nh, nh).T
    eye = np.eye(V7X_SUBLANES)
    fa = np.kron(fwd, eye)
    fa_inv = np.kron(inv, eye)
    t2 = np.arange(DFT_N2)[None, None, :]
    k2 = np.arange(DFT_N2)[None, :, None]
    ph = 2.0 * np.pi * (t2 * k2 / DFT_N2 + t2 * np.arange(nh + 1)[:, None, None] / n)
    mr, mi = np.cos(ph), -np.sin(ph)
    mb = np.concatenate([np.concatenate([mr, -mi], axis=2),
                         np.concatenate([mi, mr], axis=2)], axis=1)
    as_bf16 = lambda a: jnp.asarray(a, dtype=F32).astype(BF16)
    return as_bf16(fa), as_bf16(fa_inv), as_bf16(mb)


def _units_per_step(units):
    return max(k for k in range(1, 12) if units % k == 0)


def _nyquist_sign(nh):
    t1 = lax.broadcasted_iota(jnp.int32, (nh, 1, 1), 0)
    return jnp.where(t1 % 2 == 0, 1.0, -1.0).astype(F32)


def _stage_a(fa_ref, u, s_ref):
    nh = fa_ref.shape[1] // V7X_SUBLANES
    lanes = s_ref.shape[-1]
    sign = _nyquist_sign(nh)

    def body(g, carry):
        rhs = u(g)
        out = jnp.dot(fa_ref[...], rhs.reshape(nh * V7X_SUBLANES, lanes).astype(BF16),
                      preferred_element_type=F32)
        s_ref[0:2 * nh, g, :, :] = out.reshape(2 * nh, V7X_SUBLANES, lanes)
        s_ref[2 * nh, g, :, :] = jnp.sum(rhs * sign, axis=0)
        s_ref[2 * nh + 1, g, :, :] = jnp.zeros((V7X_SUBLANES, lanes), F32)
        return carry
    lax.fori_loop(0, DFT_GROUPS, body, 0, unroll=True)


def _spec_kernel(hf_ref, hb_ref, nf_ref, nb_ref, fa_ref, mb_ref, g_ref, s_ref, *, kb):
    s = pl.program_id(2)
    cb = hf_ref.shape[-1]

    @pl.when(s == 0)
    def _():
        def both(g):
            hf, hb = hf_ref[:, g, :, :], hb_ref[:, g, :, :]
            return jnp.concatenate([hf + hb, hf - hb], axis=-1)
        _stage_a(fa_ref, both, s_ref)

    @pl.when(s > 0)
    def _():
        inv_norm = 1.0 / (nf_ref[...] + nb_ref[...])
        half = DFT_N2
        for i in range(kb):
            k1 = (s - 1) * kb + i
            a = s_ref[pl.ds(2 * k1, 2), :, :, :].reshape(2 * half, 2 * cb).astype(BF16)
            z = jnp.dot(mb_ref[i], a, preferred_element_type=F32)
            g_ref[i, 0:half, :] = (z[0:half, 0:cb] * inv_norm).astype(g_ref.dtype)
            g_ref[i, half:, :] = (z[half:, cb:] * inv_norm).astype(g_ref.dtype)


def _spectra(hraw, asum, consts, hy_w, cb):
    fa, _, mb = consts
    units = mb.shape[0]
    nh = units - 1
    kb = _units_per_step(units)
    ncb = hy_w // cb
    h5 = hraw.reshape(nh, DFT_GROUPS, V7X_SUBLANES, hraw.shape[1])
    blk = (nh, DFT_GROUPS, V7X_SUBLANES, cb)
    kmap = lambda s: jnp.maximum(s - 1, 0)
    return pl.pallas_call(
        functools.partial(_spec_kernel, kb=kb),
        grid=(HY_ORDER, ncb, 1 + units // kb),
        in_specs=[pl.BlockSpec(blk, lambda o, c, s: (0, 0, 0, (2 * o) * ncb + c)),
                  pl.BlockSpec(blk, lambda o, c, s: (0, 0, 0, (2 * o + 1) * ncb + c)),
                  pl.BlockSpec((1, cb), lambda o, c, s: (0, (2 * o) * ncb + c)),
                  pl.BlockSpec((1, cb), lambda o, c, s: (0, (2 * o + 1) * ncb + c)),
                  pl.BlockSpec(fa.shape, lambda o, c, s: (0, 0)),
                  pl.BlockSpec((kb, 2 * DFT_N2, 2 * DFT_N2), lambda o, c, s: (kmap(s), 0, 0))],
        out_specs=pl.BlockSpec((None, kb, 2 * DFT_N2, cb), lambda o, c, s: (o, kmap(s), 0, c)),
        out_shape=jax.ShapeDtypeStruct((HY_ORDER, units, 2 * DFT_N2, hy_w), BF16),
        scratch_shapes=[pltpu.VMEM((2 * units, DFT_GROUPS, V7X_SUBLANES, 2 * cb), F32)],
        compiler_params=_params("arbitrary", "arbitrary", "arbitrary"),
        name="spec",
    )(h5, h5, asum, asum, fa, mb)


def _conv_kernel(u_ref, x_ref, bias_ref, fa_ref, fai_ref, mb_ref, g_ref, o_ref, s_ref, *, kb):
    s = pl.program_id(2)
    last = pl.num_programs(2) - 1
    lanes = s_ref.shape[-1]
    half = DFT_N2
    nh = fa_ref.shape[1] // V7X_SUBLANES

    @pl.when(s == 0)
    def _():
        _stage_a(fa_ref, lambda g: u_ref[:, g, :, :], s_ref)

    @pl.when(jnp.logical_and(s > 0, s < last))
    def _():
        for i in range(kb):
            k1 = (s - 1) * kb + i
            a = s_ref[pl.ds(2 * k1, 2), :, :, :].reshape(2 * half, lanes).astype(BF16)
            z = jnp.dot(mb_ref[i], a, preferred_element_type=F32)
            zr, zi = z[0:half], z[half:]
            gr = g_ref[i, 0:half, :].astype(F32)
            gi = g_ref[i, half:, :].astype(F32)
            p = jnp.concatenate([zr * gr - zi * gi, zr * gi + zi * gr], axis=0).astype(BF16)
            y = lax.dot_general(mb_ref[i], p, (((0,), (0,)), ((), ())), preferred_element_type=F32)
            s_ref[pl.ds(2 * k1, 2), :, :, :] = y.reshape(2, DFT_GROUPS, V7X_SUBLANES, lanes)

    @pl.when(s == last)
    def _():
        n = 2 * nh * DFT_N2
        sign = _nyquist_sign(nh) * (1.0 / n)

        def body(g, carry):
            rhs = s_ref[0:2 * nh, g, :, :].reshape(2 * nh * V7X_SUBLANES, lanes)
            conv = jnp.dot(fai_ref[...], rhs.astype(BF16), preferred_element_type=F32)
            conv = conv.reshape(nh, V7X_SUBLANES, lanes) + sign * s_ref[2 * nh, g, :, :]
            u = u_ref[:, g, :, :]
            o_ref[:, g, :, :] = x_ref[:, g, :, :] * (conv + bias_ref[...] * u)
            return carry
        lax.fori_loop(0, DFT_GROUPS, body, 0, unroll=True)


def _longconv_gate(u_src, u_col, x_src, x_col, bias, spectra, order, consts, hy_w, cb):
    fa, fa_inv, mb = consts
    b, l = u_src.shape[:2]
    units = mb.shape[0]
    nh = units - 1
    kb = _units_per_step(units)
    ncb = hy_w // cb
    view = lambda a: a.reshape(b, nh, DFT_GROUPS, V7X_SUBLANES, a.shape[2])
    blk = (None, nh, DFT_GROUPS, V7X_SUBLANES, cb)
    kmap = lambda s: jnp.clip(s - 1, 0, units // kb - 1)
    out = pl.pallas_call(
        functools.partial(_conv_kernel, kb=kb),
        grid=(ncb, b, 2 + units // kb),
        in_specs=[pl.BlockSpec(blk, lambda c, i, s: (i, 0, 0, 0, u_col // cb + c)),
                  pl.BlockSpec(blk, lambda c, i, s: (i, 0, 0, 0, x_col // cb + c)),
                  pl.BlockSpec((1, cb), lambda c, i, s: (0, c)),
                  pl.BlockSpec(fa.shape, lambda c, i, s: (0, 0)),
                  pl.BlockSpec(fa_inv.shape, lambda c, i, s: (0, 0)),
                  pl.BlockSpec((kb, 2 * DFT_N2, 2 * DFT_N2), lambda c, i, s: (kmap(s), 0, 0)),
                  pl.BlockSpec((None, kb, 2 * DFT_N2, cb), lambda c, i, s: (order, kmap(s), 0, c))],
        out_specs=pl.BlockSpec(blk, lambda c, i, s: (i, 0, 0, 0, c)),
        out_shape=jax.ShapeDtypeStruct((b, nh, DFT_GROUPS, V7X_SUBLANES, hy_w), F32),
        scratch_shapes=[pltpu.VMEM((2 * units, DFT_GROUPS, V7X_SUBLANES, cb), F32)],
        compiler_params=_params("arbitrary", "arbitrary", "arbitrary"),
        name="conv",
    )(view(u_src), view(x_src), bias, fa, fa_inv, mb, spectra)
    return out.reshape(b, l, hy_w)


def _mlstm_dir(q, k, v_aug, ig_col, b_col, ig_row, b_row, b_end, keep, ct_ref, m_ref, idx):
    m = m_ref[idx]
    ct = ct_ref[idx]
    end_row = b_end - b_row + ig_row
    end_col = b_end - b_col + ig_col
    m_new = jnp.maximum(b_end + m, jnp.max(end_row, axis=1, keepdims=True))
    h = None
    if q is not None:
        dlog = jnp.where(keep, b_col - b_row + ig_row, -jnp.inf)
        inter = b_col + m
        m_j = jnp.maximum(inter, jnp.max(dlog, axis=1, keepdims=True))
        s = lax.dot_general(q, k, (((1,), (1,)), ((), ())), preferred_element_type=F32)
        p = (s * jnp.exp(dlog - m_j)).astype(BF16)
        nd = (jnp.dot(p, v_aug, preferred_element_type=F32)
              + jnp.exp(inter - m_j) * jnp.dot(q, ct.astype(BF16), preferred_element_type=F32))
        dv = nd.shape[1] - V7X_LANES
        den = nd[:, dv:dv + 1]
        h = nd[:, :dv] / jnp.maximum(jnp.abs(den), jnp.exp(-m_j))
    w_state = jnp.exp(end_col - m_new)
    decay = jnp.exp(b_end + m - m_new)
    vw = (v_aug.astype(F32) * w_state).astype(BF16)
    ct_ref[idx] = decay * ct + lax.dot_general(k, vw, (((0,), (0,)), ((), ())),
                                               preferred_element_type=F32)
    m_ref[idx] = m_new
    return h


def _mlstm_kernel(qf_ref, kf_ref, vf_ref, gf_ref, gtf_ref,
                  qb_ref, kb_ref, vb_ref, gb_ref, gtb_ref,
                  vc_ref, kc_ref, gc_ref, gtc_ref, bias_col_ref, bias_row_ref,
                  hf_ref, hb_ref, ct_ref, m_ref, *, heads):
    i = pl.program_id(1)
    hi = lax.Precision.HIGHEST
    lc = kc_ref.shape[0]
    dqk = kc_ref.shape[1] // heads
    dv = vc_ref.shape[1] // heads
    ng = 4 * heads
    row = lax.broadcasted_iota(jnp.int32, (lc, lc), 0)
    col = lax.broadcasted_iota(jnp.int32, (lc, lc), 1)
    lower = row >= col
    upper = row <= col
    tri_lo = lower.astype(F32)
    tri_up = upper.astype(F32)
    ones = jnp.ones((lc, V7X_LANES), BF16)

    def run(direction, q_ref, k_ref, v_ref, g_ref, gt_ref, out_ref):
        gates = g_ref[...] + bias_col_ref[...]
        gates_t = gt_ref[...] + bias_row_ref[...]
        tri_col, tri_row, keep = (tri_lo, tri_up, lower) if direction == 0 else (tri_up, tri_lo, upper)
        lf_t = jax.nn.log_sigmoid(gates_t)
        b_cols = jnp.dot(tri_col, jax.nn.log_sigmoid(gates), precision=hi, preferred_element_type=F32)
        b_rows = jnp.dot(lf_t, tri_row, precision=hi, preferred_element_type=F32)
        b_ends = jnp.sum(lf_t, axis=1, keepdims=True)
        for h in range(heads):
            ci = direction * 2 * heads + h
            cf = ci + heads
            k = k_ref[:, h * dqk:(h + 1) * dqk].astype(BF16)
            v_aug = jnp.concatenate([v_ref[:, h * dv:(h + 1) * dv].astype(BF16), ones], axis=1)
            q = None if q_ref is None else q_ref[:, h * dqk:(h + 1) * dqk].astype(BF16)
            out = _mlstm_dir(q, k, v_aug, gates[:, ci:ci + 1], b_cols[:, cf:cf + 1],
                             gates_t[ci:ci + 1, :], b_rows[cf:cf + 1, :], b_ends[cf:cf + 1, :],
                             keep, ct_ref, m_ref, direction * heads + h)
            if out_ref is not None:
                out_ref[:, h * dv:(h + 1) * dv] = out

    @pl.when(i == 0)
    def _():
        ct_ref[...] = jnp.zeros_like(ct_ref)
        m_ref[...] = jnp.zeros_like(m_ref)
        run(0, None, kc_ref, vc_ref, gc_ref, gtc_ref, None)
        run(1, None, kc_ref, vc_ref, gc_ref, gtc_ref, None)

    @pl.when(i > 0)
    def _():
        run(0, qf_ref, kf_ref, vf_ref, gf_ref, gtf_ref, hf_ref)
        run(1, qb_ref, kb_ref, vb_ref, gb_ref, gtb_ref, hb_ref)


def _mlstm(p_lat, g_lat, gt_lat, p_ctx, g_ctx, gt_ctx, gate_b, *, q_col, k_col, v_col,
           kc_col, vc_col, heads, dqk, dv):
    b, l = p_lat.shape[:2]
    lc = p_ctx.shape[1]
    nc = l // lc
    qk_w, v_w = heads * dqk, heads * dv
    ng = 4 * heads
    fwd = lambda i: jnp.maximum(i - 1, 0)
    bwd = lambda i: nc - jnp.maximum(i, 1)

    def lat_specs(chunk):
        return [pl.BlockSpec((None, lc, qk_w), lambda bi, i: (bi, chunk(i), q_col // qk_w)),
                pl.BlockSpec((None, lc, qk_w), lambda bi, i: (bi, chunk(i), k_col // qk_w)),
                pl.BlockSpec((None, lc, v_w), lambda bi, i: (bi, chunk(i), v_col // v_w)),
                pl.BlockSpec((None, lc, ng), lambda bi, i: (bi, chunk(i), 0)),
                pl.BlockSpec((None, ng, lc), lambda bi, i: (bi, 0, chunk(i)))]

    ctx_specs = [pl.BlockSpec((None, lc, v_w), lambda bi, i: (bi, 0, vc_col // v_w)),
                 pl.BlockSpec((None, lc, qk_w), lambda bi, i: (bi, 0, kc_col // qk_w)),
                 pl.BlockSpec((None, lc, ng), lambda bi, i: (bi, 0, 0)),
                 pl.BlockSpec((None, ng, lc), lambda bi, i: (bi, 0, 0))]
    bias_specs = [pl.BlockSpec((1, ng), lambda bi, i: (0, 0)),
                  pl.BlockSpec((ng, 1), lambda bi, i: (0, 0))]
    out_sds = jax.ShapeDtypeStruct((b, l, v_w), F32)
    return pl.pallas_call(
        functools.partial(_mlstm_kernel, heads=heads),
        grid=(b, nc + 1),
        in_specs=lat_specs(fwd) + lat_specs(bwd) + ctx_specs + bias_specs,
        out_specs=[pl.BlockSpec((None, lc, v_w), lambda bi, i: (bi, fwd(i), 0)),
                   pl.BlockSpec((None, lc, v_w), lambda bi, i: (bi, bwd(i), 0))],
        out_shape=[out_sds, out_sds],
        scratch_shapes=[pltpu.VMEM((2 * heads, dqk, dv + V7X_LANES), F32),
                        pltpu.VMEM((2 * heads, 1, 1), F32)],
        compiler_params=_params("arbitrary", "arbitrary"),
        name="mlstm",
    )(p_lat, p_lat, p_lat, g_lat, gt_lat, p_lat, p_lat, p_lat, g_lat, gt_lat,
      p_ctx, p_ctx, g_ctx, gt_ctx, gate_b.reshape(1, ng), gate_b.reshape(ng, 1))


def _merge_kernel(hf_ref, hb_ref, o_ref, w_ref, out_ref):
    h = _ln(hf_ref[...] + hb_ref[...])
    out_ref[...] = (h * w_ref[...] * jax.nn.sigmoid(o_ref[...])).astype(out_ref.dtype)


def _merge(hf, hb, p_lat, o_col, norm_w, heads, tm):
    n, w = hf.shape
    dv = w // heads
    return pl.pallas_call(
        _merge_kernel,
        grid=(n // tm, heads),
        in_specs=[pl.BlockSpec((tm, dv), lambda i, h: (i, h)),
                  pl.BlockSpec((tm, dv), lambda i, h: (i, h)),
                  pl.BlockSpec((tm, dv), lambda i, h: (i, o_col // dv + h)),
                  pl.BlockSpec((1, dv), lambda i, h: (0, h))],
        out_specs=pl.BlockSpec((tm, dv), lambda i, h: (i, h)),
        out_shape=jax.ShapeDtypeStruct((n, w), BF16),
        compiler_params=_params("arbitrary", "arbitrary"),
        name="merge",
    )(hf, hb, p_lat, norm_w)


def _outp_kernel(hy_ref, ml_ref, w_ref, x_ref, gt_ref, o_ref, a_ref, *, alpha):
    @pl.when(pl.program_id(1) == 0)
    def _():
        ka = hy_ref.shape[1]
        a_ref[:, :ka] = hy_ref[...].astype(BF16)
        a_ref[:, ka:] = ml_ref[...]

    y = jnp.dot(a_ref[...], w_ref[...], preferred_element_type=F32)
    o_ref[...] = alpha * x_ref[...] + gt_ref[...] * y


def _outp(hy, ml, w, x, gate, *, rows_per_mod, alpha, tm, tn):
    n, d = x.shape
    ka, kb = hy.shape[1], ml.shape[1]
    per = rows_per_mod // tm
    return pl.pallas_call(
        functools.partial(_outp_kernel, alpha=alpha),
        grid=(n // tm, d // tn),
        in_specs=[pl.BlockSpec((tm, ka), lambda i, j: (i, 0)),
                  pl.BlockSpec((tm, kb), lambda i, j: (i, 0)),
                  pl.BlockSpec((ka + kb, tn), lambda i, j: (0, j)),
                  pl.BlockSpec((tm, tn), lambda i, j: (i, j)),
                  pl.BlockSpec((None, 1, tn), lambda i, j: (i // per, 0, j))],
        out_specs=pl.BlockSpec((tm, tn), lambda i, j: (i, j)),
        out_shape=jax.ShapeDtypeStruct((n, d), F32),
        scratch_shapes=[pltpu.VMEM((tm, ka + kb), BF16)],
        compiler_params=_params("arbitrary", "arbitrary"),
        name="outp",
    )(hy, ml, w, x, gate)


def kernel(x, c, ctx, c_ctx, ada_w, ada_b, ln_g, ln_b, ffn1_wi, ffn1_wo, ffn2_wi, ffn2_wo,
           w_in, hy_conv_w, hy_conv_b, hy_filt_w1, hy_filt_b1, hy_filt_f1, hy_filt_w2,
           hy_filt_b2, hy_filt_f2, hy_filt_w3, hy_bias, ml_conv_w, ml_conv_b, ml_gate_b,
           ml_norm_w, w_out):
    depth = ada_w.shape[0]
    assert depth == 1, "only the depth-1 block is implemented"
    b, l, d = x.shape
    lc = ctx.shape[1]
    heads = ML_HEADS
    hy_w = hy_bias.shape[2]
    ml_w = ml_norm_w.shape[1]
    dv = ml_w // heads
    ml_qk = ml_conv_w.shape[2] // 2
    dqk = ml_qk // heads
    p_hy = 3 * hy_w
    p_state0 = p_hy + ml_qk + ml_w
    p_main = p_state0 + ml_qk + ml_w
    n_gates = 4 * heads
    alpha = (2.0 * depth) ** 0.25
    assert l % GRID_W == 0 and l % lc == 0 and lc % GRID_W == 0

    tm = min(512, l)
    tm_ctx = min(512, b * lc)
    tm_ffn = min(1024, l)
    tm_ffn_ctx = min(1024, b * lc)

    cvec = jnp.concatenate([c, c_ctx[None], jnp.zeros((V7X_SUBLANES - (b + 1) % V7X_SUBLANES, d), F32)])
    mods = _ada(cvec, ada_w.reshape(d, 9 * d), ada_b.reshape(1, 9 * d), tn=min(512, d))
    mods = mods.reshape(cvec.shape[0], 9, 1, d)
    m_lat = [mods[:b, k] for k in range(9)]
    m_ctx = [mods[b:b + 1, k] for k in range(9)]

    lg = [ln_g[0, k][None] for k in range(3)]
    lb = [ln_b[0, k][None] for k in range(3)]
    bf = lambda a: a.astype(BF16)
    x2 = x.reshape(b * l, d)
    ctx2 = ctx.reshape(b * lc, d)

    wi1, wo1 = ffn1_wi[0], ffn1_wo[0]
    tf = 256
    x1 = _ffn(x2, lg[0], lb[0], m_lat[0], m_lat[1], m_lat[2], wi1, wo1, lg[0], lb[0],
              rows_per_mod=l, pre="ln", alpha=alpha, tm=tm_ffn, tf=tf)
    c1 = _ffn(ctx2, lg[0], lb[0], m_ctx[0], m_ctx[1], m_ctx[2], wi1, wo1, lg[0], lb[0],
              rows_per_mod=b * lc, pre="ln", alpha=alpha, tm=tm_ffn_ctx, tf=tf)

    w_in_t = jnp.swapaxes(w_in, 1, 2)
    q_scale = dqk ** -0.5
    ident = jnp.array([0.0, 1.0, 0.0], F32)[:, None]
    taps = jnp.concatenate([
        hy_conv_w[0], ml_conv_w[0][:, :ml_qk] * q_scale, jnp.tile(ident, (1, ml_w)),
        ml_conv_w[0][:, ml_qk:], jnp.tile(ident, (1, ml_w))], axis=1)
    taps = jnp.pad(taps, ((0, V7X_SUBLANES - 3), (0, 0)))
    tap_b = jnp.concatenate([
        hy_conv_b[0], ml_conv_b[0][:ml_qk] * q_scale, jnp.zeros((ml_w,), F32),
        ml_conv_b[0][ml_qk:], jnp.zeros((ml_w,), F32)])[None]
    tn = min(512, ml_qk)
    p_lat, g_lat = _proj(x1, m_lat[3], m_lat[4], w_in_t, lambda j: j, p_main, n_gates, taps, tap_b,
                         rows_per_mod=l, period=GRID_W, tm=tm_ffn, tn=tn)
    v0 = p_state0 + ml_qk
    ctx_cols = lambda a: jnp.concatenate([a[:, v0:p_main], a[:, p_state0:v0]], axis=1)
    nv = ml_w // tn
    ctx_block = lambda j: jnp.where(j < nv, j + v0 // tn, j - nv + p_state0 // tn)
    p_ctx, g_ctx = _proj(c1, m_ctx[3], m_ctx[4], w_in_t, ctx_block, p_main, n_gates,
                         ctx_cols(taps), ctx_cols(tap_b),
                         rows_per_mod=b * lc, period=lc, tm=tm_ctx, tn=tn)
    p_lat3 = p_lat.reshape(b, l, p_main)
    p_ctx3 = p_ctx.reshape(b, lc, p_main - p_state0)

    t = jnp.linspace(0.0, 1.0, l, dtype=F32)[:, None]
    w = (2.0 * math.pi / l) * jnp.arange(l, dtype=F32)[:, None]
    bands = jnp.linspace(1e-4, HY_BANDS - 1, HY_BANDS, dtype=F32)[None, :]
    z = jnp.concatenate([t, jnp.cos(bands * w), -jnp.sin(bands * w)], axis=-1)
    ze = V7X_LANES
    z = jnp.pad(z, ((0, 0), (0, ze - z.shape[1])))
    w1 = jnp.pad(hy_filt_w1[0], ((0, ze - hy_filt_w1.shape[1]), (0, 0)))
    max_decay = math.log(HY_DECAY_TARGET) / HY_FAST_PCT
    min_decay = math.log(HY_DECAY_TARGET) / HY_SLOW_PCT
    deltas = jnp.abs(jnp.linspace(min_decay, max_decay, hy_w, dtype=F32))
    deltas = jnp.tile(deltas, 2 * HY_ORDER)[None]
    hraw, asum = _filters(z, w1, hy_filt_b1[0][None], hy_filt_f1[0][None], hy_filt_w2[0],
                          hy_filt_b2[0][None], hy_filt_f2[0][None], hy_filt_w3[0], deltas,
                          tl=min(256, l))
    consts = _dft_constants(l)
    spectra = _spectra(hraw, asum, consts, hy_w, cb=V7X_LANES)
    cb = min(V7X_MXU_DIM, hy_w)
    z1 = _longconv_gate(p_lat3, 0, p_lat3, hy_w, hy_bias[0, 0][None], spectra, 0, consts, hy_w, cb)
    hy = _longconv_gate(z1, 0, p_lat3, 2 * hy_w, hy_bias[0, 1][None], spectra, 1, consts, hy_w, cb)

    g_lat, g_ctx = g_lat.reshape(b, l, n_gates), g_ctx.reshape(b, lc, n_gates)
    gt_lat, gt_ctx = jnp.swapaxes(g_lat, 1, 2), jnp.swapaxes(g_ctx, 1, 2)
    hf, hb = _mlstm(p_lat3, g_lat, gt_lat, p_ctx3, g_ctx, gt_ctx, ml_gate_b[0],
                    q_col=p_hy, k_col=p_state0, v_col=p_state0 + ml_qk,
                    kc_col=ml_w, vc_col=0, heads=heads, dqk=dqk, dv=dv)
    ml = _merge(hf.reshape(b * l, ml_w), hb.reshape(b * l, ml_w), p_lat, p_hy + ml_qk,
                ml_norm_w[0][None], heads, tm)

    y2 = _outp(hy.reshape(b * l, hy_w), ml, bf(w_out[0]), x1, m_lat[5],
               rows_per_mod=l, alpha=alpha, tm=tm, tn=min(1024, d))

    out = _ffn(y2, lg[1], lb[1], m_lat[6], m_lat[7], m_lat[8], ffn2_wi[0], ffn2_wo[0],
               lg[2], lb[2], rows_per_mod=l, pre="ln_affine", alpha=alpha, tm=tm_ffn, tf=tf)
    return out.reshape(b, l, d)
```

```python
import functools
import math

import numpy as np
import jax
import jax.numpy as jnp
from jax import lax
from jax.experimental import pallas as pl
from jax.experimental.pallas import tpu as pltpu

GRID_W = 64
ML_HEADS = 4
HY_ORDER = 2
HY_BANDS = 16
HY_DECAY_TARGET = 1e-2
HY_FAST_PCT = 0.3
HY_SLOW_PCT = 1.5
LN_EPS = 1e-5

V7X_VMEM_BYTES = 64 * 1024 * 1024
V7X_LANES = 128
V7X_SUBLANES = 8
V7X_MXU_DIM = 256
VMEM_LIMIT_BYTES = V7X_VMEM_BYTES - 4 * 1024 * 1024

DFT_N2 = 128
DFT_GROUPS = DFT_N2 // V7X_SUBLANES
ROW_CHUNK = 128
ROW_DMA_CHUNKS = 4

F32 = jnp.float32
BF16 = jnp.bfloat16


def _params(*semantics):
    return pltpu.CompilerParams(dimension_semantics=semantics, vmem_limit_bytes=VMEM_LIMIT_BYTES)


def _ln(x):
    mu = jnp.mean(x, axis=-1, keepdims=True)
    xc = x - mu
    var = jnp.mean(xc * xc, axis=-1, keepdims=True)
    return xc * lax.rsqrt(var + LN_EPS)


def _silu(x):
    return x * jax.nn.sigmoid(x)


def _ada_kernel(c_ref, w_ref, b_ref, o_ref):
    s = _silu(c_ref[...]).astype(BF16)
    o_ref[...] = jnp.dot(s, w_ref[...].astype(BF16), preferred_element_type=F32) + b_ref[...]


def _ada(cvec, w, b, tn):
    rows, d = cvec.shape
    n = w.shape[1]
    return pl.pallas_call(
        _ada_kernel,
        grid=(n // tn,),
        in_specs=[pl.BlockSpec((rows, d), lambda j: (0, 0)),
                  pl.BlockSpec((d, tn), lambda j: (0, j)),
                  pl.BlockSpec((1, tn), lambda j: (0, j))],
        out_specs=pl.BlockSpec((rows, tn), lambda j: (0, j)),
        out_shape=jax.ShapeDtypeStruct((rows, n), F32),
        compiler_params=_params("arbitrary"),
        name="ada",
    )(cvec, w, b)


def _for_chunk_rows(c, rows, body):
    size = min(ROW_CHUNK, rows)
    assert rows % size == 0

    def step(r, carry):
        body(pl.ds(pl.multiple_of(c * rows + r * size, size), size))
        return carry
    lax.fori_loop(0, rows // size, step, 0)


def _ffn_kernel(x_hbm, pg_ref, pb_ref, sh_ref, sc_ref, gt_ref, wg_ref, wu_ref, wo_ref,
                lg_ref, lb_ref, out_hbm, acc_ref, h_ref, in_sem, out_sem, *, pre, alpha):
    i = pl.program_id(0)
    f = pl.program_id(1)
    chunks = ROW_DMA_CHUNKS
    tm = acc_ref.shape[0]
    rows = tm // chunks

    def x_copy(c):
        return pltpu.make_async_copy(x_hbm.at[pl.ds(i * tm + c * rows, rows), :],
                                     acc_ref.at[pl.ds(c * rows, rows), :], in_sem.at[c])

    def out_copy(tile, c):
        return pltpu.make_async_copy(acc_ref.at[pl.ds(c * rows, rows), :],
                                     out_hbm.at[pl.ds(tile * tm + c * rows, rows), :], out_sem.at[c])

    def request(c):
        @pl.when(i > 0)
        def _():
            out_copy(i - 1, c).wait()
        x_copy(c).start()

    def prologue(c):
        x_copy(c).wait()

        def body(rs):
            x = acc_ref[rs, :]
            if pre != "none":
                x = _ln(x)
            if pre == "ln_affine":
                x = x * pg_ref[...] + pb_ref[...]
            h_ref[rs, :] = (x * (1.0 + sc_ref[...]) + sh_ref[...]).astype(BF16)
            acc_ref[rs, :] = alpha * x
        _for_chunk_rows(c, rows, body)

    @pl.when(f == 0)
    def _():
        for c in range(chunks - 1):
            request(c)
        prologue(0)
        request(chunks - 1)
        for c in range(1, chunks):
            prologue(c)

    h = h_ref[...]
    g = jnp.dot(h, wg_ref[...].astype(BF16), preferred_element_type=F32)
    u = jnp.dot(h, wu_ref[...].astype(BF16), preferred_element_type=F32)
    a = (_silu(g) * u).astype(BF16)
    acc_ref[...] += (0.5 * gt_ref[...]) * jnp.dot(a, wo_ref[...].astype(BF16),
                                                  preferred_element_type=F32)

    @pl.when(f == pl.num_programs(1) - 1)
    def _():
        def body(rs):
            acc_ref[rs, :] = _ln(acc_ref[rs, :]) * lg_ref[...] + lb_ref[...]
        for c in range(chunks):
            _for_chunk_rows(c, rows, body)
            out_copy(i, c).start()

        @pl.when(i == pl.num_programs(0) - 1)
        def _():
            for c in range(chunks):
                out_copy(i, c).wait()


def _ffn(x, pre_g, pre_b, shift, scale, gate, wi, wo, ln_g, ln_b, *, rows_per_mod, pre, alpha,
         tm, tf):
    n, d = x.shape
    ff = wo.shape[0]
    nf = ff // tf
    per = rows_per_mod // tm
    mod_spec = pl.BlockSpec((None, 1, d), lambda i, f: (i // per, 0, 0))
    vec_spec = pl.BlockSpec((1, d), lambda i, f: (0, 0))
    return pl.pallas_call(
        functools.partial(_ffn_kernel, pre=pre, alpha=alpha),
        grid=(n // tm, nf),
        in_specs=[pl.BlockSpec(memory_space=pl.ANY),
                  vec_spec, vec_spec,
                  mod_spec, mod_spec, mod_spec,
                  pl.BlockSpec((d, tf), lambda i, f: (0, f)),
                  pl.BlockSpec((d, tf), lambda i, f: (0, nf + f)),
                  pl.BlockSpec((tf, d), lambda i, f: (f, 0)),
                  vec_spec, vec_spec],
        out_specs=pl.BlockSpec(memory_space=pl.ANY),
        out_shape=jax.ShapeDtypeStruct((n, d), F32),
        scratch_shapes=[pltpu.VMEM((tm, d), F32),
                        pltpu.VMEM((tm, d), BF16),
                        pltpu.SemaphoreType.DMA((ROW_DMA_CHUNKS,)),
                        pltpu.SemaphoreType.DMA((ROW_DMA_CHUNKS,))],
        compiler_params=_params("arbitrary", "arbitrary"),
        name="ffn",
    )(x, pre_g, pre_b, shift, scale, gate, wi, wi, wo, ln_g, ln_b)


def _dot_nt(a, b_t):
    return lax.dot_general(a, b_t, (((1,), (1,)), ((), ())), preferred_element_type=F32)


def _proj_kernel(x_hbm, sh_ref, sc_ref, w_ref, wgate_ref, cw_ref, cb_ref, o_ref, og_ref,
                 x_ref, h_ref, sem, *, period):
    i = pl.program_id(0)
    tm = x_ref.shape[0]
    rows = tm // ROW_DMA_CHUNKS

    def x_copy(c):
        return pltpu.make_async_copy(x_hbm.at[pl.ds(i * tm + c * rows, rows), :],
                                     x_ref.at[pl.ds(c * rows, rows), :], sem.at[c])

    @pl.when(pl.program_id(1) == 0)
    def _():
        for c in range(ROW_DMA_CHUNKS):
            x_copy(c).start()

        def body(rs):
            h_ref[rs, :] = (x_ref[rs, :] * (1.0 + sc_ref[...]) + sh_ref[...]).astype(BF16)
        for c in range(ROW_DMA_CHUNKS):
            x_copy(c).wait()
            _for_chunk_rows(c, rows, body)
        og_ref[...] = _dot_nt(h_ref[...], wgate_ref[...].astype(BF16))

    r = _dot_nt(h_ref[...], w_ref[...].astype(BF16))
    pos = lax.broadcasted_iota(jnp.int32, (tm, 1), 0) % period
    prev = jnp.where(pos == 0, 0.0, pltpu.roll(r, 1, 0))
    nxt = jnp.where(pos == period - 1, 0.0, pltpu.roll(r, tm - 1, 0))
    cw = cw_ref[...]
    o_ref[...] = prev * cw[0:1] + r * cw[1:2] + nxt * cw[2:3] + cb_ref[...]


def _proj(x, shift, scale, w_t, w_block, gate_row0, n_gates, conv_w, conv_b, *, rows_per_mod,
          period, tm, tn):
    n, d = x.shape
    p = conv_w.shape[1]
    per = rows_per_mod // tm
    mod_spec = pl.BlockSpec((None, 1, d), lambda i, j: (i // per, 0, 0))
    return pl.pallas_call(
        functools.partial(_proj_kernel, period=period),
        grid=(n // tm, p // tn),
        in_specs=[pl.BlockSpec(memory_space=pl.ANY),
                  mod_spec, mod_spec,
                  pl.BlockSpec((None, tn, d), lambda i, j: (0, w_block(j), 0)),
                  pl.BlockSpec((None, n_gates, d), lambda i, j: (0, gate_row0 // n_gates, 0)),
                  pl.BlockSpec((V7X_SUBLANES, tn), lambda i, j: (0, j)),
                  pl.BlockSpec((1, tn), lambda i, j: (0, j))],
        out_specs=[pl.BlockSpec((tm, tn), lambda i, j: (i, j)),
                   pl.BlockSpec((tm, n_gates), lambda i, j: (i, 0))],
        out_shape=[jax.ShapeDtypeStruct((n, p), F32),
                   jax.ShapeDtypeStruct((n, n_gates), F32)],
        scratch_shapes=[pltpu.VMEM((tm, d), F32),
                        pltpu.VMEM((tm, d), BF16),
                        pltpu.SemaphoreType.DMA((ROW_DMA_CHUNKS,))],
        compiler_params=_params("arbitrary", "arbitrary"),
        name="proj",
    )(x, shift, scale, w_t, w_t, conv_w, conv_b)


def _filt_kernel(z_ref, w1_ref, b1_ref, f1_ref, w2_ref, b2_ref, f2_ref, w3_ref, dl_ref,
                 h_ref, asum_ref):
    hi = lax.Precision.HIGHEST
    z = z_ref[...]
    a = jnp.sin(f1_ref[...] * (jnp.dot(z, w1_ref[...], precision=hi, preferred_element_type=F32)
                               + b1_ref[...]))
    a = jnp.sin(f2_ref[...] * (jnp.dot(a, w2_ref[...], precision=hi, preferred_element_type=F32)
                               + b2_ref[...]))
    h = jnp.dot(a, w3_ref[...], precision=hi, preferred_element_type=F32)
    h = h * jnp.exp(-z[:, 0:1] * dl_ref[...])

    @pl.when(pl.program_id(0) == 0)
    def _():
        asum_ref[...] = jnp.zeros_like(asum_ref)

    h_ref[...] = h
    asum_ref[...] += jnp.sum(jnp.abs(h), axis=0, keepdims=True)


def _filters(z, w1, b1, f1, w2, b2, f2, w3, deltas, tl):
    l, ze = z.shape
    fh = w2.shape[0]
    n = w3.shape[1]
    full = lambda shape: pl.BlockSpec(shape, lambda i: (0, 0))
    return pl.pallas_call(
        _filt_kernel,
        grid=(l // tl,),
        in_specs=[pl.BlockSpec((tl, ze), lambda i: (i, 0)),
                  full((ze, fh)), full((1, fh)), full((1, fh)),
                  full((fh, fh)), full((1, fh)), full((1, fh)),
                  full((fh, n)), full((1, n))],
        out_specs=[pl.BlockSpec((tl, n), lambda i: (i, 0)), full((1, n))],
        out_shape=[jax.ShapeDtypeStruct((l, n), F32), jax.ShapeDtypeStruct((1, n), F32)],
        compiler_params=_params("arbitrary"),
        name="filt",
    )(z, w1, b1, f1, w2, b2, f2, w3, deltas)


@functools.lru_cache(maxsize=None)
def _dft_constants(seq_len):
    n = 2 * seq_len
    nh = n // DFT_N2 // 2
    t1 = np.arange(nh)[None, :]
    k1 = np.arange(nh)[:, None]
    th = 2.0 * np.pi * (t1 * k1) / (2 * nh)
    fwd = np.stack([np.cos(th), -np.sin(th)], axis=1).reshape(2 * nh, nh)
    pair = np.where(k1 == 0, 1.0, 2.0) / n
    inv = np.stack([pair * np.cos(th), -pair * np.sin(th)], axis=1).reshape(2 * nh, nh).T
    eye = np.eye(V7X_SUBLANES)
    fa = np.kron(fwd, eye)
    fa_inv = np.kron(inv, eye)
    t2 = np.arange(DFT_N2)[None, None, :]
    k2 = np.arange(DFT_N2)[None, :, None]
    ph = 2.0 * np.pi * (t2 * k2 / DFT_N2 + t2 * np.arange(nh + 1)[:, None, None] / n)
    mr, mi = np.cos(ph), -np.sin(ph)
    mb = np.concatenate([np.concatenate([mr, -mi], axis=2),
                         np.concatenate([mi, mr], axis=2)], axis=1)
    as_bf16 = lambda a: jnp.asarray(a, dtype=F32).astype(BF16)
    return as_bf16(fa), as_bf16(fa_inv), as_bf16(mb)


def _units_per_step(units):
    return max(k for k in range(1, 12) if units % k == 0)


def _nyquist_sign(nh):
    t1 = lax.broadcasted_iota(jnp.int32, (nh, 1, 1), 0)
    return jnp.where(t1 % 2 == 0, 1.0, -1.0).astype(F32)


def _stage_a(fa_ref, u, s_ref):
    nh = fa_ref.shape[1] // V7X_SUBLANES
    lanes = s_ref.shape[-1]
    sign = _nyquist_sign(nh)

    def body(g, carry):
        rhs = u(g)
        out = jnp.dot(fa_ref[...], rhs.reshape(nh * V7X_SUBLANES, lanes).astype(BF16),
                      preferred_element_type=F32)
        s_ref[0:2 * nh, g, :, :] = out.reshape(2 * nh, V7X_SUBLANES, lanes)
        s_ref[2 * nh, g, :, :] = jnp.sum(rhs * sign, axis=0)
        s_ref[2 * nh + 1, g, :, :] = jnp.zeros((V7X_SUBLANES, lanes), F32)
        return carry
    lax.fori_loop(0, DFT_GROUPS, body, 0, unroll=True)


def _spec_kernel(hf_ref, hb_ref, nf_ref, nb_ref, fa_ref, mb_ref, g_ref, s_ref, *, kb):
    s = pl.program_id(2)
    cb = hf_ref.shape[-1]

    @pl.when(s == 0)
    def _():
        def both(g):
            hf, hb = hf_ref[:, g, :, :], hb_ref[:, g, :, :]
            return jnp.concatenate([hf + hb, hf - hb], axis=-1)
        _stage_a(fa_ref, both, s_ref)

    @pl.when(s > 0)
    def _():
        inv_norm = 1.0 / (nf_ref[...] + nb_ref[...])
        half = DFT_N2
        for i in range(kb):
            k1 = (s - 1) * kb + i
            a = s_ref[pl.ds(2 * k1, 2), :, :, :].reshape(2 * half, 2 * cb).astype(BF16)
            z = jnp.dot(mb_ref[i], a, preferred_element_type=F32)
            g_ref[i, 0:half, :] = (z[0:half, 0:cb] * inv_norm).astype(g_ref.dtype)
            g_ref[i, half:, :] = (z[half:, cb:] * inv_norm).astype(g_ref.dtype)


def _spectra(hraw, asum, consts, hy_w, cb):
    fa, _, mb = consts
    units = mb.shape[0]
    nh = units - 1
    kb = _units_per_step(units)
    ncb = hy_w // cb
    h5 = hraw.reshape(nh, DFT_GROUPS, V7X_SUBLANES, hraw.shape[1])
    blk = (nh, DFT_GROUPS, V7X_SUBLANES, cb)
    kmap = lambda s: jnp.maximum(s - 1, 0)
    return pl.pallas_call(
        functools.partial(_spec_kernel, kb=kb),
        grid=(HY_ORDER, ncb, 1 + units // kb),
        in_specs=[pl.BlockSpec(blk, lambda o, c, s: (0, 0, 0, (2 * o) * ncb + c)),
                  pl.BlockSpec(blk, lambda o, c, s: (0, 0, 0, (2 * o + 1) * ncb + c)),
                  pl.BlockSpec((1, cb), lambda o, c, s: (0, (2 * o) * ncb + c)),
                  pl.BlockSpec((1, cb), lambda o, c, s: (0, (2 * o + 1) * ncb + c)),
                  pl.BlockSpec(fa.shape, lambda o, c, s: (0, 0)),
                  pl.BlockSpec((kb, 2 * DFT_N2, 2 * DFT_N2), lambda o, c, s: (kmap(s), 0, 0))],
        out_specs=pl.BlockSpec((None, kb, 2 * DFT_N2, cb), lambda o, c, s: (o, kmap(s), 0, c)),
        out_shape=jax.ShapeDtypeStruct((HY_ORDER, units, 2 * DFT_N2, hy_w), BF16),
        scratch_shapes=[pltpu.VMEM((2 * units, DFT_GROUPS, V7X_SUBLANES, 2 * cb), F32)],
        compiler_params=_params("arbitrary", "arbitrary", "arbitrary"),
        name="spec",
    )(h5, h5, asum, asum, fa, mb)


def _conv_kernel(u_ref, x_ref, bias_ref, fa_ref, fai_ref, mb_ref, g_ref, o_ref, s_ref, *, kb):
    s = pl.program_id(2)
    last = pl.num_programs(2) - 1
    lanes = s_ref.shape[-1]
    half = DFT_N2
    nh = fa_ref.shape[1] // V7X_SUBLANES

    @pl.when(s == 0)
    def _():
        _stage_a(fa_ref, lambda g: u_ref[:, g, :, :], s_ref)

    @pl.when(jnp.logical_and(s > 0, s < last))
    def _():
        for i in range(kb):
            k1 = (s - 1) * kb + i
            a = s_ref[pl.ds(2 * k1, 2), :, :, :].reshape(2 * half, lanes).astype(BF16)
            z = jnp.dot(mb_ref[i], a, preferred_element_type=F32)
            zr, zi = z[0:half], z[half:]
            gr = g_ref[i, 0:half, :].astype(F32)
            gi = g_ref[i, half:, :].astype(F32)
            p = jnp.concatenate([zr * gr - zi * gi, zr * gi + zi * gr], axis=0).astype(BF16)
            y = lax.dot_general(mb_ref[i], p, (((0,), (0,)), ((), ())), preferred_element_type=F32)
            s_ref[pl.ds(2 * k1, 2), :, :, :] = y.reshape(2, DFT_GROUPS, V7X_SUBLANES, lanes)

    @pl.when(s == last)
    def _():
        n = 2 * nh * DFT_N2
        sign = _nyquist_sign(nh) * (1.0 / n)

        def body(g, carry):
            rhs = s_ref[0:2 * nh, g, :, :].reshape(2 * nh * V7X_SUBLANES, lanes)
            conv = jnp.dot(fai_ref[...], rhs.astype(BF16), preferred_element_type=F32)
            conv = conv.reshape(nh, V7X_SUBLANES, lanes) + sign * s_ref[2 * nh, g, :, :]
            u = u_ref[:, g, :, :]
            o_ref[:, g, :, :] = x_ref[:, g, :, :] * (conv + bias_ref[...] * u)
            return carry
        lax.fori_loop(0, DFT_GROUPS, body, 0, unroll=True)


def _longconv_gate(u_src, u_col, x_src, x_col, bias, spectra, order, consts, hy_w, cb):
    fa, fa_inv, mb = consts
    b, l = u_src.shape[:2]
    units = mb.shape[0]
    nh = units - 1
    kb = _units_per_step(units)
    ncb = hy_w // cb
    view = lambda a: a.reshape(b, nh, DFT_GROUPS, V7X_SUBLANES, a.shape[2])
    blk = (None, nh, DFT_GROUPS, V7X_SUBLANES, cb)
    kmap = lambda s: jnp.clip(s - 1, 0, units // kb - 1)
    out = pl.pallas_call(
        functools.partial(_conv_kernel, kb=kb),
        grid=(ncb, b, 2 + units // kb),
        in_specs=[pl.BlockSpec(blk, lambda c, i, s: (i, 0, 0, 0, u_col // cb + c)),
                  pl.BlockSpec(blk, lambda c, i, s: (i, 0, 0, 0, x_col // cb + c)),
                  pl.BlockSpec((1, cb), lambda c, i, s: (0, c)),
                  pl.BlockSpec(fa.shape, lambda c, i, s: (0, 0)),
                  pl.BlockSpec(fa_inv.shape, lambda c, i, s: (0, 0)),
                  pl.BlockSpec((kb, 2 * DFT_N2, 2 * DFT_N2), lambda c, i, s: (kmap(s), 0, 0)),
                  pl.BlockSpec((None, kb, 2 * DFT_N2, cb), lambda c, i, s: (order, kmap(s), 0, c))],
        out_specs=pl.BlockSpec(blk, lambda c, i, s: (i, 0, 0, 0, c)),
        out_shape=jax.ShapeDtypeStruct((b, nh, DFT_GROUPS, V7X_SUBLANES, hy_w), F32),
        scratch_shapes=[pltpu.VMEM((2 * units, DFT_GROUPS, V7X_SUBLANES, cb), F32)],
        compiler_params=_params("arbitrary", "arbitrary", "arbitrary"),
        name="conv",
    )(view(u_src), view(x_src), bias, fa, fa_inv, mb, spectra)
    return out.reshape(b, l, hy_w)


def _mlstm_dir(q, k, v_aug, ig_col, b_col, ig_row, b_row, b_end, keep, ct_ref, m_ref, idx):
    m = m_ref[idx]
    ct = ct_ref[idx]
    end_row = b_end - b_row + ig_row
    end_col = b_end - b_col + ig_col
    m_new = jnp.maximum(b_end + m, jnp.max(end_row, axis=1, keepdims=True))
    h = None
    if q is not None:
        dlog = jnp.where(keep, b_col - b_row + ig_row, -jnp.inf)
        inter = b_col + m
        m_j = jnp.maximum(inter, jnp.max(dlog, axis=1, keepdims=True))
        s = lax.dot_general(q, k, (((1,), (1,)), ((), ())), preferred_element_type=F32)
        p = (s * jnp.exp(dlog - m_j)).astype(BF16)
        nd = (jnp.dot(p, v_aug, preferred_element_type=F32)
              + jnp.exp(inter - m_j) * jnp.dot(q, ct.astype(BF16), preferred_element_type=F32))
        dv = nd.shape[1] - V7X_LANES
        den = nd[:, dv:dv + 1]
        h = nd[:, :dv] / jnp.maximum(jnp.abs(den), jnp.exp(-m_j))
    w_state = jnp.exp(end_col - m_new)
    decay = jnp.exp(b_end + m - m_new)
    vw = (v_aug.astype(F32) * w_state).astype(BF16)
    ct_ref[idx] = decay * ct + lax.dot_general(k, vw, (((0,), (0,)), ((), ())),
                                               preferred_element_type=F32)
    m_ref[idx] = m_new
    return h


def _mlstm_kernel(qf_ref, kf_ref, vf_ref, gf_ref, gtf_ref,
                  qb_ref, kb_ref, vb_ref, gb_ref, gtb_ref,
                  vc_ref, kc_ref, gc_ref, gtc_ref, bias_col_ref, bias_row_ref,
                  hf_ref, hb_ref, ct_ref, m_ref, *, heads):
    i = pl.program_id(1)
    hi = lax.Precision.HIGHEST
    lc = kc_ref.shape[0]
    dqk = kc_ref.shape[1] // heads
    dv = vc_ref.shape[1] // heads
    ng = 4 * heads
    row = lax.broadcasted_iota(jnp.int32, (lc, lc), 0)
    col = lax.broadcasted_iota(jnp.int32, (lc, lc), 1)
    lower = row >= col
    upper = row <= col
    tri_lo = lower.astype(F32)
    tri_up = upper.astype(F32)
    ones = jnp.ones((lc, V7X_LANES), BF16)

    def run(direction, q_ref, k_ref, v_ref, g_ref, gt_ref, out_ref):
        gates = g_ref[...] + bias_col_ref[...]
        gates_t = gt_ref[...] + bias_row_ref[...]
        tri_col, tri_row, keep = (tri_lo, tri_up, lower) if direction == 0 else (tri_up, tri_lo, upper)
        lf_t = jax.nn.log_sigmoid(gates_t)
        b_cols = jnp.dot(tri_col, jax.nn.log_sigmoid(gates), precision=hi, preferred_element_type=F32)
        b_rows = jnp.dot(lf_t, tri_row, precision=hi, preferred_element_type=F32)
        b_ends = jnp.sum(lf_t, axis=1, keepdims=True)
        for h in range(heads):
            ci = direction * 2 * heads + h
            cf = ci + heads
            k = k_ref[:, h * dqk:(h + 1) * dqk].astype(BF16)
            v_aug = jnp.concatenate([v_ref[:, h * dv:(h + 1) * dv].astype(BF16), ones], axis=1)
            q = None if q_ref is None else q_ref[:, h * dqk:(h + 1) * dqk].astype(BF16)
            out = _mlstm_dir(q, k, v_aug, gates[:, ci:ci + 1], b_cols[:, cf:cf + 1],
                             gates_t[ci:ci + 1, :], b_rows[cf:cf + 1, :], b_ends[cf:cf + 1, :],
                             keep, ct_ref, m_ref, direction * heads + h)
            if out_ref is not None:
                out_ref[:, h * dv:(h + 1) * dv] = out

    @pl.when(i == 0)
    def _():
        ct_ref[...] = jnp.zeros_like(ct_ref)
        m_ref[...] = jnp.zeros_like(m_ref)
        run(0, None, kc_ref, vc_ref, gc_ref, gtc_ref, None)
        run(1, None, kc_ref, vc_ref, gc_ref, gtc_ref, None)

    @pl.when(i > 0)
    def _():
        run(0, qf_ref, kf_ref, vf_ref, gf_ref, gtf_ref, hf_ref)
        run(1, qb_ref, kb_ref, vb_ref, gb_ref, gtb_ref, hb_ref)


def _mlstm(p_lat, g_lat, gt_lat, p_ctx, g_ctx, gt_ctx, gate_b, *, q_col, k_col, v_col,
           kc_col, vc_col, heads, dqk, dv):
    b, l = p_lat.shape[:2]
    lc = p_ctx.shape[1]
    nc = l // lc
    qk_w, v_w = heads * dqk, heads * dv
    ng = 4 * heads
    fwd = lambda i: jnp.maximum(i - 1, 0)
    bwd = lambda i: nc - jnp.maximum(i, 1)

    def lat_specs(chunk):
        return [pl.BlockSpec((None, lc, qk_w), lambda bi, i: (bi, chunk(i), q_col // qk_w)),
                pl.BlockSpec((None, lc, qk_w), lambda bi, i: (bi, chunk(i), k_col // qk_w)),
                pl.BlockSpec((None, lc, v_w), lambda bi, i: (bi, chunk(i), v_col // v_w)),
                pl.BlockSpec((None, lc, ng), lambda bi, i: (bi, chunk(i), 0)),
                pl.BlockSpec((None, ng, lc), lambda bi, i: (bi, 0, chunk(i)))]

    ctx_specs = [pl.BlockSpec((None, lc, v_w), lambda bi, i: (bi, 0, vc_col // v_w)),
                 pl.BlockSpec((None, lc, qk_w), lambda bi, i: (bi, 0, kc_col // qk_w)),
                 pl.BlockSpec((None, lc, ng), lambda bi, i: (bi, 0, 0)),
                 pl.BlockSpec((None, ng, lc), lambda bi, i: (bi, 0, 0))]
    bias_specs = [pl.BlockSpec((1, ng), lambda bi, i: (0, 0)),
                  pl.BlockSpec((ng, 1), lambda bi, i: (0, 0))]
    out_sds = jax.ShapeDtypeStruct((b, l, v_w), F32)
    return pl.pallas_call(
        functools.partial(_mlstm_kernel, heads=heads),
        grid=(b, nc + 1),
        in_specs=lat_specs(fwd) + lat_specs(bwd) + ctx_specs + bias_specs,
        out_specs=[pl.BlockSpec((None, lc, v_w), lambda bi, i: (bi, fwd(i), 0)),
                   pl.BlockSpec((None, lc, v_w), lambda bi, i: (bi, bwd(i), 0))],
        out_shape=[out_sds, out_sds],
        scratch_shapes=[pltpu.VMEM((2 * heads, dqk, dv + V7X_LANES), F32),
                        pltpu.VMEM((2 * heads, 1, 1), F32)],
        compiler_params=_params("arbitrary", "arbitrary"),
        name="mlstm",
    )(p_lat, p_lat, p_lat, g_lat, gt_lat, p_lat, p_lat, p_lat, g_lat, gt_lat,
      p_ctx, p_ctx, g_ctx, gt_ctx, gate_b.reshape(1, ng), gate_b.reshape(ng, 1))


def _merge_kernel(hy_ref, hf_ref, hb_ref, o_ref, w_ref, out_ref, *, hy_blocks):
    j = pl.program_id(1)

    @pl.when(j < hy_blocks)
    def _():
        out_ref[...] = hy_ref[...].astype(out_ref.dtype)

    @pl.when(j >= hy_blocks)
    def _():
        h = _ln(hf_ref[...] + hb_ref[...])
        out_ref[...] = (h * w_ref[...] * jax.nn.sigmoid(o_ref[...])).astype(out_ref.dtype)


def _merge(hy, hf, hb, p_lat, o_col, norm_w, heads, tm):
    n, w = hf.shape
    dv = w // heads
    hy_blocks = hy.shape[1] // dv
    head = lambda j: jnp.maximum(j - hy_blocks, 0)
    return pl.pallas_call(
        functools.partial(_merge_kernel, hy_blocks=hy_blocks),
        grid=(n // tm, hy_blocks + heads),
        in_specs=[pl.BlockSpec((tm, dv), lambda i, j: (i, jnp.minimum(j, hy_blocks - 1))),
                  pl.BlockSpec((tm, dv), lambda i, j: (i, head(j))),
                  pl.BlockSpec((tm, dv), lambda i, j: (i, head(j))),
                  pl.BlockSpec((tm, dv), lambda i, j: (i, o_col // dv + head(j))),
                  pl.BlockSpec((1, dv), lambda i, j: (0, head(j)))],
        out_specs=pl.BlockSpec((tm, dv), lambda i, j: (i, j)),
        out_shape=jax.ShapeDtypeStruct((n, hy.shape[1] + w), BF16),
        compiler_params=_params("arbitrary", "arbitrary"),
        name="merge",
    )(hy, hf, hb, p_lat, norm_w)


def _outp_kernel(a_ref, w_ref, x_ref, gt_ref, o_ref, *, alpha):
    y = jnp.dot(a_ref[...], w_ref[...], preferred_element_type=F32)
    o_ref[...] = alpha * x_ref[...] + gt_ref[...] * y


def _outp(a, w, x, gate, *, rows_per_mod, alpha, tm, tn):
    n, d = x.shape
    k = a.shape[1]
    per = rows_per_mod // tm
    return pl.pallas_call(
        functools.partial(_outp_kernel, alpha=alpha),
        grid=(d // tn, n // tm),
        in_specs=[pl.BlockSpec((tm, k), lambda j, i: (i, 0)),
                  pl.BlockSpec((k, tn), lambda j, i: (0, j)),
                  pl.BlockSpec((tm, tn), lambda j, i: (i, j)),
                  pl.BlockSpec((None, 1, tn), lambda j, i: (i // per, 0, j))],
        out_specs=pl.BlockSpec((tm, tn), lambda j, i: (i, j)),
        out_shape=jax.ShapeDtypeStruct((n, d), F32),
        compiler_params=_params("arbitrary", "arbitrary"),
        name="outp",
    )(a, w, x, gate)


def kernel(x, c, ctx, c_ctx, ada_w, ada_b, ln_g, ln_b, ffn1_wi, ffn1_wo, ffn2_wi, ffn2_wo,
           w_in, hy_conv_w, hy_conv_b, hy_filt_w1, hy_filt_b1, hy_filt_f1, hy_filt_w2,
           hy_filt_b2, hy_filt_f2, hy_filt_w3, hy_bias, ml_conv_w, ml_conv_b, ml_gate_b,
           ml_norm_w, w_out):
    depth = ada_w.shape[0]
    assert depth == 1, "only the depth-1 block is implemented"
    b, l, d = x.shape
    lc = ctx.shape[1]
    heads = ML_HEADS
    hy_w = hy_bias.shape[2]
    ml_w = ml_norm_w.shape[1]
    dv = ml_w // heads
    ml_qk = ml_conv_w.shape[2] // 2
    dqk = ml_qk // heads
    p_hy = 3 * hy_w
    p_state0 = p_hy + ml_qk + ml_w
    p_main = p_state0 + ml_qk + ml_w
    n_gates = 4 * heads
    alpha = (2.0 * depth) ** 0.25
    assert l % GRID_W == 0 and l % lc == 0 and lc % GRID_W == 0

    tm = min(512, l)
    tm_ctx = min(512, b * lc)
    tm_ffn = min(1024, l)
    tm_ffn_ctx = min(1024, b * lc)

    cvec = jnp.concatenate([c, c_ctx[None], jnp.zeros((V7X_SUBLANES - (b + 1) % V7X_SUBLANES, d), F32)])
    mods = _ada(cvec, ada_w.reshape(d, 9 * d), ada_b.reshape(1, 9 * d), tn=min(512, d))
    mods = mods.reshape(cvec.shape[0], 9, 1, d)
    m_lat = [mods[:b, k] for k in range(9)]
    m_ctx = [mods[b:b + 1, k] for k in range(9)]

    lg = [ln_g[0, k][None] for k in range(3)]
    lb = [ln_b[0, k][None] for k in range(3)]
    bf = lambda a: a.astype(BF16)
    x2 = x.reshape(b * l, d)
    ctx2 = ctx.reshape(b * lc, d)

    wi1, wo1 = ffn1_wi[0], ffn1_wo[0]
    tf = 256
    x1 = _ffn(x2, lg[0], lb[0], m_lat[0], m_lat[1], m_lat[2], wi1, wo1, lg[0], lb[0],
              rows_per_mod=l, pre="ln", alpha=alpha, tm=tm_ffn, tf=tf)
    c1 = _ffn(ctx2, lg[0], lb[0], m_ctx[0], m_ctx[1], m_ctx[2], wi1, wo1, lg[0], lb[0],
              rows_per_mod=b * lc, pre="ln", alpha=alpha, tm=tm_ffn_ctx, tf=tf)

    w_in_t = jnp.swapaxes(w_in, 1, 2)
    q_scale = dqk ** -0.5
    ident = jnp.array([0.0, 1.0, 0.0], F32)[:, None]
    taps = jnp.concatenate([
        hy_conv_w[0], ml_conv_w[0][:, :ml_qk] * q_scale, jnp.tile(ident, (1, ml_w)),
        ml_conv_w[0][:, ml_qk:], jnp.tile(ident, (1, ml_w))], axis=1)
    taps = jnp.pad(taps, ((0, V7X_SUBLANES - 3), (0, 0)))
    tap_b = jnp.concatenate([
        hy_conv_b[0], ml_conv_b[0][:ml_qk] * q_scale, jnp.zeros((ml_w,), F32),
        ml_conv_b[0][ml_qk:], jnp.zeros((ml_w,), F32)])[None]
    tn = min(512, ml_qk)
    p_lat, g_lat = _proj(x1, m_lat[3], m_lat[4], w_in_t, lambda j: j, p_main, n_gates, taps, tap_b,
                         rows_per_mod=l, period=GRID_W, tm=tm_ffn, tn=tn)
    v0 = p_state0 + ml_qk
    ctx_cols = lambda a: jnp.concatenate([a[:, v0:p_main], a[:, p_state0:v0]], axis=1)
    nv = ml_w // tn
    ctx_block = lambda j: jnp.where(j < nv, j + v0 // tn, j - nv + p_state0 // tn)
    p_ctx, g_ctx = _proj(c1, m_ctx[3], m_ctx[4], w_in_t, ctx_block, p_main, n_gates,
                         ctx_cols(taps), ctx_cols(tap_b),
                         rows_per_mod=b * lc, period=lc, tm=tm_ctx, tn=tn)
    p_lat3 = p_lat.reshape(b, l, p_main)
    p_ctx3 = p_ctx.reshape(b, lc, p_main - p_state0)

    t = jnp.linspace(0.0, 1.0, l, dtype=F32)[:, None]
    w = (2.0 * math.pi / l) * jnp.arange(l, dtype=F32)[:, None]
    bands = jnp.linspace(1e-4, HY_BANDS - 1, HY_BANDS, dtype=F32)[None, :]
    z = jnp.concatenate([t, jnp.cos(bands * w), -jnp.sin(bands * w)], axis=-1)
    ze = V7X_LANES
    z = jnp.pad(z, ((0, 0), (0, ze - z.shape[1])))
    w1 = jnp.pad(hy_filt_w1[0], ((0, ze - hy_filt_w1.shape[1]), (0, 0)))
    max_decay = math.log(HY_DECAY_TARGET) / HY_FAST_PCT
    min_decay = math.log(HY_DECAY_TARGET) / HY_SLOW_PCT
    deltas = jnp.abs(jnp.linspace(min_decay, max_decay, hy_w, dtype=F32))
    deltas = jnp.tile(deltas, 2 * HY_ORDER)[None]
    hraw, asum = _filters(z, w1, hy_filt_b1[0][None], hy_filt_f1[0][None], hy_filt_w2[0],
                          hy_filt_b2[0][None], hy_filt_f2[0][None], hy_filt_w3[0], deltas,
                          tl=min(256, l))
    consts = _dft_constants(l)
    spectra = _spectra(hraw, asum, consts, hy_w, cb=V7X_LANES)
    cb = min(V7X_MXU_DIM, hy_w)
    z1 = _longconv_gate(p_lat3, 0, p_lat3, hy_w, hy_bias[0, 0][None], spectra, 0, consts, hy_w, cb)
    hy = _longconv_gate(z1, 0, p_lat3, 2 * hy_w, hy_bias[0, 1][None], spectra, 1, consts, hy_w, cb)

    g_lat, g_ctx = g_lat.reshape(b, l, n_gates), g_ctx.reshape(b, lc, n_gates)
    gt_lat, gt_ctx = jnp.swapaxes(g_lat, 1, 2), jnp.swapaxes(g_ctx, 1, 2)
    hf, hb = _mlstm(p_lat3, g_lat, gt_lat, p_ctx3, g_ctx, gt_ctx, ml_gate_b[0],
                    q_col=p_hy, k_col=p_state0, v_col=p_state0 + ml_qk,
                    kc_col=ml_w, vc_col=0, heads=heads, dqk=dqk, dv=dv)
    mixed = _merge(hy.reshape(b * l, hy_w), hf.reshape(b * l, ml_w), hb.reshape(b * l, ml_w),
                   p_lat, p_hy + ml_qk, ml_norm_w[0][None], heads, tm)

    y2 = _outp(mixed, bf(w_out[0]), x1, m_lat[5],
               rows_per_mod=l, alpha=alpha, tm=tm_ffn, tn=min(1024, d))

    out = _ffn(y2, lg[1], lb[1], m_lat[6], m_lat[7], m_lat[8], ffn2_wi[0], ffn2_wo[0],
               lg[2], lb[2], rows_per_mod=l, pre="ln_affine", alpha=alpha, tm=tm_ffn, tf=tf)
    return out.reshape(b, l, d)
```

```python
import functools
import math

import numpy as np
import jax
import jax.numpy as jnp
from jax import lax
from jax.experimental import pallas as pl
from jax.experimental.pallas import tpu as pltpu

GRID_W = 64
ML_HEADS = 4
HY_ORDER = 2
HY_BANDS = 16
HY_DECAY_TARGET = 1e-2
HY_FAST_PCT = 0.3
HY_SLOW_PCT = 1.5
LN_EPS = 1e-5

V7X_VMEM_BYTES = 64 * 1024 * 1024
V7X_LANES = 128
V7X_SUBLANES = 8
V7X_MXU_DIM = 256
VMEM_LIMIT_BYTES = V7X_VMEM_BYTES - 4 * 1024 * 1024

DFT_N2 = 128
DFT_GROUPS = DFT_N2 // V7X_SUBLANES
ROW_CHUNK = 128
PROJ_SUB_ROWS = 256
ROW_DMA_CHUNKS = 4

F32 = jnp.float32
BF16 = jnp.bfloat16


def _params(*semantics):
    return pltpu.CompilerParams(dimension_semantics=semantics, vmem_limit_bytes=VMEM_LIMIT_BYTES)


def _ln(x):
    mu = jnp.mean(x, axis=-1, keepdims=True)
    xc = x - mu
    var = jnp.mean(xc * xc, axis=-1, keepdims=True)
    return xc * lax.rsqrt(var + LN_EPS)


def _silu(x):
    return x * jax.nn.sigmoid(x)


def _ada_kernel(c_ref, w_ref, b_ref, o_ref):
    s = _silu(c_ref[...]).astype(BF16)
    o_ref[...] = jnp.dot(s, w_ref[...].astype(BF16), preferred_element_type=F32) + b_ref[...]


def _ada(cvec, w, b, tn):
    rows, d = cvec.shape
    n = w.shape[1]
    return pl.pallas_call(
        _ada_kernel,
        grid=(n // tn,),
        in_specs=[pl.BlockSpec((rows, d), lambda j: (0, 0)),
                  pl.BlockSpec((d, tn), lambda j: (0, j)),
                  pl.BlockSpec((1, tn), lambda j: (0, j))],
        out_specs=pl.BlockSpec((rows, tn), lambda j: (0, j)),
        out_shape=jax.ShapeDtypeStruct((rows, n), F32),
        compiler_params=_params("arbitrary"),
        name="ada",
    )(cvec, w, b)


def _for_chunk_rows(c, rows, body):
    size = min(ROW_CHUNK, rows)
    assert rows % size == 0

    def step(r, carry):
        body(pl.ds(pl.multiple_of(c * rows + r * size, size), size))
        return carry
    lax.fori_loop(0, rows // size, step, 0)


def _ffn_kernel(x_hbm, pg_ref, pb_ref, sh_ref, sc_ref, gt_ref, wg_ref, wu_ref, wo_ref,
                lg_ref, lb_ref, out_hbm, acc_ref, h_ref, in_sem, out_sem, *, pre, alpha):
    i = pl.program_id(0)
    f = pl.program_id(1)
    chunks = ROW_DMA_CHUNKS
    tm = acc_ref.shape[0]
    rows = tm // chunks

    def x_copy(c):
        return pltpu.make_async_copy(x_hbm.at[pl.ds(i * tm + c * rows, rows), :],
                                     acc_ref.at[pl.ds(c * rows, rows), :], in_sem.at[c])

    def out_copy(tile, c):
        return pltpu.make_async_copy(acc_ref.at[pl.ds(c * rows, rows), :],
                                     out_hbm.at[pl.ds(tile * tm + c * rows, rows), :], out_sem.at[c])

    def request(c):
        @pl.when(i > 0)
        def _():
            out_copy(i - 1, c).wait()
        x_copy(c).start()

    def prologue(c):
        x_copy(c).wait()

        def body(rs):
            x = acc_ref[rs, :]
            if pre != "none":
                x = _ln(x)
            if pre == "ln_affine":
                x = x * pg_ref[...] + pb_ref[...]
            h_ref[rs, :] = (x * (1.0 + sc_ref[...]) + sh_ref[...]).astype(BF16)
            acc_ref[rs, :] = alpha * x
        _for_chunk_rows(c, rows, body)

    @pl.when(f == 0)
    def _():
        for c in range(chunks - 1):
            request(c)
        prologue(0)
        request(chunks - 1)
        for c in range(1, chunks):
            prologue(c)

    h = h_ref[...]
    g = jnp.dot(h, wg_ref[...].astype(BF16), preferred_element_type=F32)
    u = jnp.dot(h, wu_ref[...].astype(BF16), preferred_element_type=F32)
    a = (_silu(g) * u).astype(BF16)
    acc_ref[...] += (0.5 * gt_ref[...]) * jnp.dot(a, wo_ref[...].astype(BF16),
                                                  preferred_element_type=F32)

    @pl.when(f == pl.num_programs(1) - 1)
    def _():
        def body(rs):
            acc_ref[rs, :] = _ln(acc_ref[rs, :]) * lg_ref[...] + lb_ref[...]
        for c in range(chunks):
            _for_chunk_rows(c, rows, body)
            out_copy(i, c).start()

        @pl.when(i == pl.num_programs(0) - 1)
        def _():
            for c in range(chunks):
                out_copy(i, c).wait()


def _ffn(x, pre_g, pre_b, shift, scale, gate, wi, wo, ln_g, ln_b, *, rows_per_mod, pre, alpha,
         tm, tf):
    n, d = x.shape
    ff = wo.shape[0]
    nf = ff // tf
    per = rows_per_mod // tm
    mod_spec = pl.BlockSpec((None, 1, d), lambda i, f: (i // per, 0, 0))
    vec_spec = pl.BlockSpec((1, d), lambda i, f: (0, 0))
    return pl.pallas_call(
        functools.partial(_ffn_kernel, pre=pre, alpha=alpha),
        grid=(n // tm, nf),
        in_specs=[pl.BlockSpec(memory_space=pl.ANY),
                  vec_spec, vec_spec,
                  mod_spec, mod_spec, mod_spec,
                  pl.BlockSpec((d, tf), lambda i, f: (0, f)),
                  pl.BlockSpec((d, tf), lambda i, f: (0, nf + f)),
                  pl.BlockSpec((tf, d), lambda i, f: (f, 0)),
                  vec_spec, vec_spec],
        out_specs=pl.BlockSpec(memory_space=pl.ANY),
        out_shape=jax.ShapeDtypeStruct((n, d), F32),
        scratch_shapes=[pltpu.VMEM((tm, d), F32),
                        pltpu.VMEM((tm, d), BF16),
                        pltpu.SemaphoreType.DMA((ROW_DMA_CHUNKS,)),
                        pltpu.SemaphoreType.DMA((ROW_DMA_CHUNKS,))],
        compiler_params=_params("arbitrary", "arbitrary"),
        name="ffn",
    )(x, pre_g, pre_b, shift, scale, gate, wi, wi, wo, ln_g, ln_b)


def _dot_nt(a, b_t):
    return lax.dot_general(a, b_t, (((1,), (1,)), ((), ())), preferred_element_type=F32)


def _proj_kernel(x_hbm, sh_ref, sc_ref, w_ref, wgate_ref, cw_ref, cb_ref, o_ref, og_ref,
                 x_ref, h_ref, sem, *, period):
    i = pl.program_id(0)
    tm = x_ref.shape[0]
    rows = tm // ROW_DMA_CHUNKS

    def x_copy(c):
        return pltpu.make_async_copy(x_hbm.at[pl.ds(i * tm + c * rows, rows), :],
                                     x_ref.at[pl.ds(c * rows, rows), :], sem.at[c])

    @pl.when(pl.program_id(1) == 0)
    def _():
        for c in range(ROW_DMA_CHUNKS):
            x_copy(c).start()

        def body(rs):
            h_ref[rs, :] = (x_ref[rs, :] * (1.0 + sc_ref[...]) + sh_ref[...]).astype(BF16)
        for c in range(ROW_DMA_CHUNKS):
            x_copy(c).wait()
            _for_chunk_rows(c, rows, body)
        og_ref[...] = _dot_nt(h_ref[...], wgate_ref[...].astype(BF16))

    w = w_ref[...].astype(BF16)
    cw = cw_ref[...]
    sub = max(period, PROJ_SUB_ROWS)
    assert tm % sub == 0 and sub % period == 0
    pos =lax.broadcasted_iota(jnp.int32, (sub, 1), 0) % period
    for m in range(tm // sub):
        r = _dot_nt(h_ref[m * sub:(m + 1) * sub, :], w)
        prev = jnp.where(pos == 0, 0.0, pltpu.roll(r, 1, 0))
        nxt = jnp.where(pos == period - 1, 0.0, pltpu.roll(r, sub - 1, 0))
        o_ref[m * sub:(m + 1) * sub, :] = (prev * cw[0:1] + r * cw[1:2] + nxt * cw[2:3]
                                           + cb_ref[...])


def _proj(x, shift, scale, w_t, w_block, gate_row0, n_gates, conv_w, conv_b, *, rows_per_mod,
          period, tm, tn):
    n, d = x.shape
    p = conv_w.shape[1]
    per = rows_per_mod // tm
    mod_spec = pl.BlockSpec((None, 1, d), lambda i, j: (i // per, 0, 0))
    return pl.pallas_call(
        functools.partial(_proj_kernel, period=period),
        grid=(n // tm, p // tn),
        in_specs=[pl.BlockSpec(memory_space=pl.ANY),
                  mod_spec, mod_spec,
                  pl.BlockSpec((None, tn, d), lambda i, j: (0, w_block(j), 0)),
                  pl.BlockSpec((None, n_gates, d), lambda i, j: (0, gate_row0 // n_gates, 0)),
                  pl.BlockSpec((V7X_SUBLANES, tn), lambda i, j: (0, j)),
                  pl.BlockSpec((1, tn), lambda i, j: (0, j))],
        out_specs=[pl.BlockSpec((tm, tn), lambda i, j: (i, j)),
                   pl.BlockSpec((tm, n_gates), lambda i, j: (i, 0))],
        out_shape=[jax.ShapeDtypeStruct((n, p), F32),
                   jax.ShapeDtypeStruct((n, n_gates), F32)],
        scratch_shapes=[pltpu.VMEM((tm, d), F32),
                        pltpu.VMEM((tm, d), BF16),
                        pltpu.SemaphoreType.DMA((ROW_DMA_CHUNKS,))],
        compiler_params=_params("arbitrary", "arbitrary"),
        name="proj",
    )(x, shift, scale, w_t, w_t, conv_w, conv_b)


def _filt_kernel(z_ref, w1_ref, b1_ref, f1_ref, w2_ref, b2_ref, f2_ref, w3_ref, dl_ref,
                 h_ref, asum_ref):
    hi = lax.Precision.HIGHEST
    z = z_ref[...]
    a = jnp.sin(f1_ref[...] * (jnp.dot(z, w1_ref[...], precision=hi, preferred_element_type=F32)
                               + b1_ref[...]))
    a = jnp.sin(f2_ref[...] * (jnp.dot(a, w2_ref[...], precision=hi, preferred_element_type=F32)
                               + b2_ref[...]))
    h = jnp.dot(a, w3_ref[...], precision=hi, preferred_element_type=F32)
    h = h * jnp.exp(-z[:, 0:1] * dl_ref[...])

    @pl.when(pl.program_id(0) == 0)
    def _():
        asum_ref[...] = jnp.zeros_like(asum_ref)

    h_ref[...] = h
    asum_ref[...] += jnp.sum(jnp.abs(h), axis=0, keepdims=True)


def _filters(z, w1, b1, f1, w2, b2, f2, w3, deltas, tl):
    l, ze = z.shape
    fh = w2.shape[0]
    n = w3.shape[1]
    full = lambda shape: pl.BlockSpec(shape, lambda i: (0, 0))
    return pl.pallas_call(
        _filt_kernel,
        grid=(l // tl,),
        in_specs=[pl.BlockSpec((tl, ze), lambda i: (i, 0)),
                  full((ze, fh)), full((1, fh)), full((1, fh)),
                  full((fh, fh)), full((1, fh)), full((1, fh)),
                  full((fh, n)), full((1, n))],
        out_specs=[pl.BlockSpec((tl, n), lambda i: (i, 0)), full((1, n))],
        out_shape=[jax.ShapeDtypeStruct((l, n), F32), jax.ShapeDtypeStruct((1, n), F32)],
        compiler_params=_params("arbitrary"),
        name="filt",
    )(z, w1, b1, f1, w2, b2, f2, w3, deltas)


@functools.lru_cache(maxsize=None)
def _dft_constants(seq_len):
    n = 2 * seq_len
    nh = n // DFT_N2 // 2
    t1 = np.arange(nh)[None, :]
    k1 = np.arange(nh)[:, None]
    th = 2.0 * np.pi * (t1 * k1) / (2 * nh)
    fwd = np.stack([np.cos(th), -np.sin(th)], axis=1).reshape(2 * nh, nh)
    pair = np.where(k1 == 0, 1.0, 2.0) / n
    inv = np.stack([pair * np.cos(th), -pair * np.sin(th)], axis=1).reshape(2 * nh, nh).T
    eye = np.eye(V7X_SUBLANES)
    fa = np.kron(fwd, eye)
    fa_inv = np.kron(inv, eye)
    t2 = np.arange(DFT_N2)[None, None, :]
    k2 = np.arange(DFT_N2)[None, :, None]
    ph = 2.0 * np.pi * (t2 * k2 / DFT_N2 + t2 * np.arange(nh + 1)[:, None, None] / n)
    mr, mi = np.cos(ph), -np.sin(ph)
    mb = np.concatenate([np.concatenate([mr, -mi], axis=2),
                         np.concatenate([mi, mr], axis=2)], axis=1)
    as_bf16 = lambda a: jnp.asarray(a, dtype=F32).astype(BF16)
    return as_bf16(fa), as_bf16(fa_inv), as_bf16(mb)


def _units_per_step(units):
    return max(k for k in range(1, 12) if units % k == 0)


def _nyquist_sign(nh):
    t1 = lax.broadcasted_iota(jnp.int32, (nh, 1, 1), 0)
    return jnp.where(t1 % 2 == 0, 1.0, -1.0).astype(F32)


def _stage_a(fa_ref, u, s_ref):
    nh = fa_ref.shape[1] // V7X_SUBLANES
    lanes = s_ref.shape[-1]
    sign = _nyquist_sign(nh)

    def body(g, carry):
        rhs = u(g)
        out = jnp.dot(fa_ref[...], rhs.reshape(nh * V7X_SUBLANES, lanes).astype(BF16),
                      preferred_element_type=F32)
        s_ref[0:2 * nh, g, :, :] = out.reshape(2 * nh, V7X_SUBLANES, lanes)
        s_ref[2 * nh, g, :, :] = jnp.sum(rhs * sign, axis=0)
        s_ref[2 * nh + 1, g, :, :] = jnp.zeros((V7X_SUBLANES, lanes), F32)
        return carry
    lax.fori_loop(0, DFT_GROUPS, body, 0, unroll=True)


def _spec_kernel(hf_ref, hb_ref, nf_ref, nb_ref, fa_ref, mb_ref, g_ref, s_ref, *, kb):
    s = pl.program_id(2)
    cb = hf_ref.shape[-1]

    @pl.when(s == 0)
    def _():
        def both(g):
            hf, hb = hf_ref[:, g, :, :], hb_ref[:, g, :, :]
            return jnp.concatenate([hf + hb, hf - hb], axis=-1)
        _stage_a(fa_ref, both, s_ref)

    @pl.when(s > 0)
    def _():
        inv_norm = 1.0 / (nf_ref[...] + nb_ref[...])
        half = DFT_N2
        for i in range(kb):
            k1 = (s - 1) * kb + i
            a = s_ref[pl.ds(2 * k1, 2), :, :, :].reshape(2 * half, 2 * cb).astype(BF16)
            z = jnp.dot(mb_ref[k1], a, preferred_element_type=F32)
            g_ref[i, 0:half, :] = (z[0:half, 0:cb] * inv_norm).astype(g_ref.dtype)
            g_ref[i, half:, :] = (z[half:, cb:] * inv_norm).astype(g_ref.dtype)


def _spectra(hraw, asum, consts, hy_w, cb):
    fa, _, mb = consts
    units = mb.shape[0]
    nh = units - 1
    kb = _units_per_step(units)
    ncb = hy_w // cb
    h5 = hraw.reshape(nh, DFT_GROUPS, V7X_SUBLANES, hraw.shape[1])
    blk = (nh, DFT_GROUPS, V7X_SUBLANES, cb)
    kmap = lambda s: jnp.maximum(s - 1, 0)
    return pl.pallas_call(
        functools.partial(_spec_kernel, kb=kb),
        grid=(HY_ORDER, ncb, 1 + units // kb),
        in_specs=[pl.BlockSpec(blk, lambda o, c, s: (0, 0, 0, (2 * o) * ncb + c)),
                  pl.BlockSpec(blk, lambda o, c, s: (0, 0, 0, (2 * o + 1) * ncb + c)),
                  pl.BlockSpec((1, cb), lambda o, c, s: (0, (2 * o) * ncb + c)),
                  pl.BlockSpec((1, cb), lambda o, c, s: (0, (2 * o + 1) * ncb + c)),
                  pl.BlockSpec(fa.shape, lambda o, c, s: (0, 0)),
                  pl.BlockSpec(mb.shape, lambda o, c, s: (0, 0, 0), pipeline_mode=pl.Buffered(1))],
        out_specs=pl.BlockSpec((None, kb, 2 * DFT_N2, cb), lambda o, c, s: (o, kmap(s), 0, c)),
        out_shape=jax.ShapeDtypeStruct((HY_ORDER, units, 2 * DFT_N2, hy_w), BF16),
        scratch_shapes=[pltpu.VMEM((2 * units, DFT_GROUPS, V7X_SUBLANES, 2 * cb), F32)],
        compiler_params=_params("arbitrary", "arbitrary", "arbitrary"),
        name="spec",
    )(h5, h5, asum, asum, fa, mb)


def _conv_kernel(u_ref, x_ref, bias_ref, fa_ref, fai_ref, mb_ref, g_ref, o_ref, s_ref, *, kb):
    lanes = s_ref.shape[-1]
    half = DFT_N2
    nh = fa_ref.shape[1] // V7X_SUBLANES

    _stage_a(fa_ref, lambda g: u_ref[:, g, :, :], s_ref)

    def unit(k1, carry):
        a = s_ref[pl.ds(2 * k1, 2), :, :, :].reshape(2 * half, lanes).astype(BF16)
        z = jnp.dot(mb_ref[k1], a, preferred_element_type=F32)
        zr, zi = z[0:half], z[half:]
        gr = g_ref[k1, 0:half, :].astype(F32)
        gi = g_ref[k1, half:, :].astype(F32)
        p = jnp.concatenate([zr * gr - zi * gi, zr * gi + zi * gr], axis=0).astype(BF16)
        y = lax.dot_general(mb_ref[k1], p, (((0,), (0,)), ((), ())), preferred_element_type=F32)
        s_ref[pl.ds(2 * k1, 2), :, :, :] = y.reshape(2, DFT_GROUPS, V7X_SUBLANES, lanes)
        return carry
    lax.fori_loop(0, nh + 1, unit, 0, unroll=kb)

    n = 2 * nh * DFT_N2
    sign = _nyquist_sign(nh) * (1.0 / n)

    def body(g, carry):
        rhs = s_ref[0:2 * nh, g, :, :].reshape(2 * nh * V7X_SUBLANES, lanes)
        conv = jnp.dot(fai_ref[...], rhs.astype(BF16), preferred_element_type=F32)
        conv = conv.reshape(nh, V7X_SUBLANES, lanes) + sign * s_ref[2 * nh, g, :, :]
        u = u_ref[:, g, :, :]
        o_ref[:, g, :, :] = x_ref[:, g, :, :] * (conv + bias_ref[...] * u)
        return carry
    lax.fori_loop(0, DFT_GROUPS, body, 0, unroll=True)


def _longconv_gate(u_src, u_col, x_src, x_col, bias, spectra, order, consts, hy_w, cb):
    fa, fa_inv, mb = consts
    b, l = u_src.shape[:2]
    units = mb.shape[0]
    nh = units - 1
    kb = _units_per_step(units)
    ncb = hy_w // cb
    view = lambda a: a.reshape(b, nh, DFT_GROUPS, V7X_SUBLANES, a.shape[2])
    blk = (None, nh, DFT_GROUPS, V7X_SUBLANES, cb)
    once = pl.Buffered(1)
    out = pl.pallas_call(
        functools.partial(_conv_kernel, kb=kb),
        grid=(ncb, b),
        in_specs=[pl.BlockSpec(blk, lambda c, i: (i, 0, 0, 0, u_col // cb + c)),
                  pl.BlockSpec(blk, lambda c, i: (i, 0, 0, 0, x_col // cb + c)),
                  pl.BlockSpec((1, cb), lambda c, i: (0, c)),
                  pl.BlockSpec(fa.shape, lambda c, i: (0, 0), pipeline_mode=once),
                  pl.BlockSpec(fa_inv.shape, lambda c, i: (0, 0), pipeline_mode=once),
                  pl.BlockSpec(mb.shape, lambda c, i: (0, 0, 0), pipeline_mode=once),
                  pl.BlockSpec((None, units, 2 * DFT_N2, cb), lambda c, i: (order, 0, 0, c),
                               pipeline_mode=once)],
        out_specs=pl.BlockSpec(blk, lambda c, i: (i, 0, 0, 0, c)),
        out_shape=jax.ShapeDtypeStruct((b, nh, DFT_GROUPS, V7X_SUBLANES, hy_w), F32),
        scratch_shapes=[pltpu.VMEM((2 * units, DFT_GROUPS, V7X_SUBLANES, cb), F32)],
        compiler_params=_params("arbitrary", "arbitrary"),
        name="conv",
    )(view(u_src), view(x_src), bias, fa, fa_inv, mb, spectra)
    return out.reshape(b, l, hy_w)


def _mlstm_dir(q, k, v_aug, ig_col, b_col, ig_row, b_row, b_end, keep, ct_ref, m_ref, idx):
    m = m_ref[idx]
    ct = ct_ref[idx]
    end_row = b_end - b_row + ig_row
    end_col = b_end - b_col + ig_col
    m_new = jnp.maximum(b_end + m, jnp.max(end_row, axis=1, keepdims=True))
    h = None
    if q is not None:
        dlog = jnp.where(keep, b_col - b_row + ig_row, -jnp.inf)
        inter = b_col + m
        m_j = jnp.maximum(inter, jnp.max(dlog, axis=1, keepdims=True))
        s = lax.dot_general(q, k, (((1,), (1,)), ((), ())), preferred_element_type=F32)
        p = (s * jnp.exp(dlog - m_j)).astype(BF16)
        nd = (jnp.dot(p, v_aug, preferred_element_type=F32)
              + jnp.exp(inter - m_j) * jnp.dot(q, ct.astype(BF16), preferred_element_type=F32))
        dv = nd.shape[1] - V7X_LANES
        den = nd[:, dv:dv + 1]
        h = nd[:, :dv] / jnp.maximum(jnp.abs(den), jnp.exp(-m_j))
    w_state = jnp.exp(end_col - m_new)
    decay = jnp.exp(b_end + m - m_new)
    vw = (v_aug.astype(F32) * w_state).astype(BF16)
    ct_ref[idx] = decay * ct + lax.dot_general(k, vw, (((0,), (0,)), ((), ())),
                                               preferred_element_type=F32)
    m_ref[idx] = m_new
    return h


def _mlstm_kernel(qf_ref, kf_ref, vf_ref, gf_ref, gtf_ref,
                  qb_ref, kb_ref, vb_ref, gb_ref, gtb_ref,
                  vc_ref, kc_ref, gc_ref, gtc_ref, bias_col_ref, bias_row_ref,
                  hf_ref, hb_ref, ct_ref, m_ref, *, heads):
    i = pl.program_id(1)
    hi = lax.Precision.HIGHEST
    lc = kc_ref.shape[0]
    dqk = kc_ref.shape[1] // heads
    dv = vc_ref.shape[1] // heads
    ng = 4 * heads
    row = lax.broadcasted_iota(jnp.int32, (lc, lc), 0)
    col = lax.broadcasted_iota(jnp.int32, (lc, lc), 1)
    lower = row >= col
    upper = row <= col
    tri_lo = lower.astype(F32)
    tri_up = upper.astype(F32)
    ones = jnp.ones((lc, V7X_LANES), BF16)

    def run(direction, q_ref, k_ref, v_ref, g_ref, gt_ref, out_ref):
        gates = g_ref[...] + bias_col_ref[...]
        gates_t = gt_ref[...] + bias_row_ref[...]
        tri_col, tri_row, keep = (tri_lo, tri_up, lower) if direction == 0 else (tri_up, tri_lo, upper)
        lf_t = jax.nn.log_sigmoid(gates_t)
        b_cols = jnp.dot(tri_col, jax.nn.log_sigmoid(gates), precision=hi, preferred_element_type=F32)
        b_rows = jnp.dot(lf_t, tri_row, precision=hi, preferred_element_type=F32)
        b_ends = jnp.sum(lf_t, axis=1, keepdims=True)
        for h in range(heads):
            ci = direction * 2 * heads + h
            cf = ci + heads
            k = k_ref[:, h * dqk:(h + 1) * dqk].astype(BF16)
            v_aug = jnp.concatenate([v_ref[:, h * dv:(h + 1) * dv].astype(BF16), ones], axis=1)
            q = None if q_ref is None else q_ref[:, h * dqk:(h + 1) * dqk].astype(BF16)
            out = _mlstm_dir(q, k, v_aug, gates[:, ci:ci + 1], b_cols[:, cf:cf + 1],
                             gates_t[ci:ci + 1, :], b_rows[cf:cf + 1, :], b_ends[cf:cf + 1, :],
                             keep, ct_ref, m_ref, direction * heads + h)
            if out_ref is not None:
                out_ref[:, h * dv:(h + 1) * dv] = out

    @pl.when(i == 0)
    def _():
        ct_ref[...] = jnp.zeros_like(ct_ref)
        m_ref[...] = jnp.zeros_like(m_ref)
        run(0, None, kc_ref, vc_ref, gc_ref, gtc_ref, None)
        run(1, None, kc_ref, vc_ref, gc_ref, gtc_ref, None)

    @pl.when(i > 0)
    def _():
        run(0, qf_ref, kf_ref, vf_ref, gf_ref, gtf_ref, hf_ref)
        run(1, qb_ref, kb_ref, vb_ref, gb_ref, gtb_ref, hb_ref)


def _mlstm(p_lat, g_lat, gt_lat, p_ctx, g_ctx, gt_ctx, gate_b, *, q_col, k_col, v_col,
           kc_col, vc_col, heads, dqk, dv):
    b, l = p_lat.shape[:2]
    lc = p_ctx.shape[1]
    nc = l // lc
    qk_w, v_w = heads * dqk, heads * dv
    ng = 4 * heads
    fwd = lambda i: jnp.maximum(i - 1, 0)
    bwd = lambda i: nc - jnp.maximum(i, 1)

    def lat_specs(chunk):
        return [pl.BlockSpec((None, lc, qk_w), lambda bi, i: (bi, chunk(i), q_col // qk_w)),
                pl.BlockSpec((None, lc, qk_w), lambda bi, i: (bi, chunk(i), k_col // qk_w)),
                pl.BlockSpec((None, lc, v_w), lambda bi, i: (bi, chunk(i), v_col // v_w)),
                pl.BlockSpec((None, lc, ng), lambda bi, i: (bi, chunk(i), 0)),
                pl.BlockSpec((None, ng, lc), lambda bi, i: (bi, 0, chunk(i)))]

    ctx_specs = [pl.BlockSpec((None, lc, v_w), lambda bi, i: (bi, 0, vc_col // v_w)),
                 pl.BlockSpec((None, lc, qk_w), lambda bi, i: (bi, 0, kc_col // qk_w)),
                 pl.BlockSpec((None, lc, ng), lambda bi, i: (bi, 0, 0)),
                 pl.BlockSpec((None, ng, lc), lambda bi, i: (bi, 0, 0))]
    bias_specs = [pl.BlockSpec((1, ng), lambda bi, i: (0, 0)),
                  pl.BlockSpec((ng, 1), lambda bi, i: (0, 0))]
    out_sds = jax.ShapeDtypeStruct((b, l, v_w), F32)
    return pl.pallas_call(
        functools.partial(_mlstm_kernel, heads=heads),
        grid=(b, nc + 1),
        in_specs=lat_specs(fwd) + lat_specs(bwd) + ctx_specs + bias_specs,
        out_specs=[pl.BlockSpec((None, lc, v_w), lambda bi, i: (bi, fwd(i), 0)),
                   pl.BlockSpec((None, lc, v_w), lambda bi, i: (bi, bwd(i), 0))],
        out_shape=[out_sds, out_sds],
        scratch_shapes=[pltpu.VMEM((2 * heads, dqk, dv + V7X_LANES), F32),
                        pltpu.VMEM((2 * heads, 1, 1), F32)],
        compiler_params=_params("arbitrary", "arbitrary"),
        name="mlstm",
    )(p_lat, p_lat, p_lat, g_lat, gt_lat, p_lat, p_lat, p_lat, g_lat, gt_lat,
      p_ctx, p_ctx, g_ctx, gt_ctx, gate_b.reshape(1, ng), gate_b.reshape(ng, 1))


def _merge_kernel(hy_ref, hf_ref, hb_ref, o_ref, w_ref, out_ref, *, hy_blocks):
    j = pl.program_id(1)

    @pl.when(j < hy_blocks)
    def _():
        out_ref[...] = hy_ref[...].astype(out_ref.dtype)

    @pl.when(j >= hy_blocks)
    def _():
        h = _ln(hf_ref[...] + hb_ref[...])
        out_ref[...] = (h * w_ref[...] * jax.nn.sigmoid(o_ref[...])).astype(out_ref.dtype)


def _merge(hy, hf, hb, p_lat, o_col, norm_w, heads, tm):
    n, w = hf.shape
    dv = w // heads
    hy_blocks = hy.shape[1] // dv
    head = lambda j: jnp.maximum(j - hy_blocks, 0)
    return pl.pallas_call(
        functools.partial(_merge_kernel, hy_blocks=hy_blocks),
        grid=(n // tm, hy_blocks + heads),
        in_specs=[pl.BlockSpec((tm, dv), lambda i, j: (i, jnp.minimum(j, hy_blocks - 1))),
                  pl.BlockSpec((tm, dv), lambda i, j: (i, head(j))),
                  pl.BlockSpec((tm, dv), lambda i, j: (i, head(j))),
                  pl.BlockSpec((tm, dv), lambda i, j: (i, o_col // dv + head(j))),
                  pl.BlockSpec((1, dv), lambda i, j: (0, head(j)))],
        out_specs=pl.BlockSpec((tm, dv), lambda i, j: (i, j)),
        out_shape=jax.ShapeDtypeStruct((n, hy.shape[1] + w), BF16),
        compiler_params=_params("arbitrary", "arbitrary"),
        name="merge",
    )(hy, hf, hb, p_lat, norm_w)


def _outp_kernel(a_ref, w_ref, x_ref, gt_ref, o_ref, *, alpha):
    y = jnp.dot(a_ref[...], w_ref[...], preferred_element_type=F32)
    o_ref[...] = alpha * x_ref[...] + gt_ref[...] * y


def _outp(a, w, x, gate, *, rows_per_mod, alpha, tm, tn):
    n, d = x.shape
    k = a.shape[1]
    per = rows_per_mod // tm
    return pl.pallas_call(
        functools.partial(_outp_kernel, alpha=alpha),
        grid=(d // tn, n // tm),
        in_specs=[pl.BlockSpec((tm, k), lambda j, i: (i, 0)),
                  pl.BlockSpec((k, tn), lambda j, i: (0, j)),
                  pl.BlockSpec((tm, tn), lambda j, i: (i, j)),
                  pl.BlockSpec((None, 1, tn), lambda j, i: (i // per, 0, j))],
        out_specs=pl.BlockSpec((tm, tn), lambda j, i: (i, j)),
        out_shape=jax.ShapeDtypeStruct((n, d), F32),
        compiler_params=_params("arbitrary", "arbitrary"),
        name="outp",
    )(a, w, x, gate)


def kernel(x, c, ctx, c_ctx, ada_w, ada_b, ln_g, ln_b, ffn1_wi, ffn1_wo, ffn2_wi, ffn2_wo,
           w_in, hy_conv_w, hy_conv_b, hy_filt_w1, hy_filt_b1, hy_filt_f1, hy_filt_w2,
           hy_filt_b2, hy_filt_f2, hy_filt_w3, hy_bias, ml_conv_w, ml_conv_b, ml_gate_b,
           ml_norm_w, w_out):
    depth = ada_w.shape[0]
    assert depth == 1, "only the depth-1 block is implemented"
    b, l, d = x.shape
    lc = ctx.shape[1]
    heads = ML_HEADS
    hy_w = hy_bias.shape[2]
    ml_w = ml_norm_w.shape[1]
    dv = ml_w // heads
    ml_qk = ml_conv_w.shape[2] // 2
    dqk = ml_qk // heads
    p_hy = 3 * hy_w
    p_state0 = p_hy + ml_qk + ml_w
    p_main = p_state0 + ml_qk + ml_w
    n_gates = 4 * heads
    alpha = (2.0 * depth) ** 0.25
    assert l % GRID_W == 0 and l % lc == 0 and lc % GRID_W == 0

    tm = min(512, l)
    tm_ctx = min(512, b * lc)
    tm_ffn = min(1024, l)
    tm_ffn_ctx = min(1024, b * lc)

    cvec = jnp.concatenate([c, c_ctx[None], jnp.zeros((V7X_SUBLANES - (b + 1) % V7X_SUBLANES, d), F32)])
    mods = _ada(cvec, ada_w.reshape(d, 9 * d), ada_b.reshape(1, 9 * d), tn=min(512, d))
    mods = mods.reshape(cvec.shape[0], 9, 1, d)
    m_lat = [mods[:b, k] for k in range(9)]
    m_ctx = [mods[b:b + 1, k] for k in range(9)]

    lg = [ln_g[0, k][None] for k in range(3)]
    lb = [ln_b[0, k][None] for k in range(3)]
    bf = lambda a: a.astype(BF16)
    x2 = x.reshape(b * l, d)
    ctx2 = ctx.reshape(b * lc, d)

    wi1, wo1 = ffn1_wi[0], ffn1_wo[0]
    tf = 256
    x1 = _ffn(x2, lg[0], lb[0], m_lat[0], m_lat[1], m_lat[2], wi1, wo1, lg[0], lb[0],
              rows_per_mod=l, pre="ln", alpha=alpha, tm=tm_ffn, tf=tf)
    c1 = _ffn(ctx2, lg[0], lb[0], m_ctx[0], m_ctx[1], m_ctx[2], wi1, wo1, lg[0], lb[0],
              rows_per_mod=b * lc, pre="ln", alpha=alpha, tm=tm_ffn_ctx, tf=tf)

    w_in_t = jnp.swapaxes(w_in, 1, 2)
    q_scale = dqk ** -0.5
    ident = jnp.array([0.0, 1.0, 0.0], F32)[:, None]
    taps = jnp.concatenate([
        hy_conv_w[0], ml_conv_w[0][:, :ml_qk] * q_scale, jnp.tile(ident, (1, ml_w)),
        ml_conv_w[0][:, ml_qk:], jnp.tile(ident, (1, ml_w))], axis=1)
    taps = jnp.pad(taps, ((0, V7X_SUBLANES - 3), (0, 0)))
    tap_b = jnp.concatenate([
        hy_conv_b[0], ml_conv_b[0][:ml_qk] * q_scale, jnp.zeros((ml_w,), F32),
        ml_conv_b[0][ml_qk:], jnp.zeros((ml_w,), F32)])[None]
    tn = min(512, ml_qk)
    p_lat, g_lat = _proj(x1, m_lat[3], m_lat[4], w_in_t, lambda j: j, p_main, n_gates, taps, tap_b,
                         rows_per_mod=l, period=GRID_W, tm=tm_ffn, tn=tn)
    v0 = p_state0 + ml_qk
    ctx_cols = lambda a: jnp.concatenate([a[:, v0:p_main], a[:, p_state0:v0]], axis=1)
    nv = ml_w // tn
    ctx_block = lambda j: jnp.where(j < nv, j + v0 // tn, j - nv + p_state0 // tn)
    p_ctx, g_ctx = _proj(c1, m_ctx[3], m_ctx[4], w_in_t, ctx_block, p_main, n_gates,
                         ctx_cols(taps), ctx_cols(tap_b),
                         rows_per_mod=b * lc, period=lc, tm=tm_ctx, tn=tn)
    p_lat3 = p_lat.reshape(b, l, p_main)
    p_ctx3 = p_ctx.reshape(b, lc, p_main - p_state0)

    t = jnp.linspace(0.0, 1.0, l, dtype=F32)[:, None]
    w = (2.0 * math.pi / l) * jnp.arange(l, dtype=F32)[:, None]
    bands = jnp.linspace(1e-4, HY_BANDS - 1, HY_BANDS, dtype=F32)[None, :]
    z = jnp.concatenate([t, jnp.cos(bands * w), -jnp.sin(bands * w)], axis=-1)
    ze = V7X_LANES
    z = jnp.pad(z, ((0, 0), (0, ze - z.shape[1])))
    w1 = jnp.pad(hy_filt_w1[0], ((0, ze - hy_filt_w1.shape[1]), (0, 0)))
    max_decay = math.log(HY_DECAY_TARGET) / HY_FAST_PCT
    min_decay = math.log(HY_DECAY_TARGET) / HY_SLOW_PCT
    deltas = jnp.abs(jnp.linspace(min_decay, max_decay, hy_w, dtype=F32))
    deltas = jnp.tile(deltas, 2 * HY_ORDER)[None]
    hraw, asum = _filters(z, w1, hy_filt_b1[0][None], hy_filt_f1[0][None], hy_filt_w2[0],
                          hy_filt_b2[0][None], hy_filt_f2[0][None], hy_filt_w3[0], deltas,
                          tl=min(256, l))
    consts = _dft_constants(l)
    spectra = _spectra(hraw, asum, consts, hy_w, cb=V7X_LANES)
    cb = min(V7X_MXU_DIM, hy_w)
    z1 = _longconv_gate(p_lat3, 0, p_lat3, hy_w, hy_bias[0, 0][None], spectra, 0, consts, hy_w, cb)
    hy = _longconv_gate(z1, 0, p_lat3, 2 * hy_w, hy_bias[0, 1][None], spectra, 1, consts, hy_w, cb)

    g_lat, g_ctx = g_lat.reshape(b, l, n_gates), g_ctx.reshape(b, lc, n_gates)
    gt_lat, gt_ctx = jnp.swapaxes(g_lat, 1, 2), jnp.swapaxes(g_ctx, 1, 2)
    hf, hb = _mlstm(p_lat3, g_lat, gt_lat, p_ctx3, g_ctx, gt_ctx, ml_gate_b[0],
                    q_col=p_hy, k_col=p_state0, v_col=p_state0 + ml_qk,
                    kc_col=ml_w, vc_col=0, heads=heads, dqk=dqk, dv=dv)
    mixed = _merge(hy.reshape(b * l, hy_w), hf.reshape(b * l, ml_w), hb.reshape(b * l, ml_w),
                   p_lat, p_hy + ml_qk, ml_norm_w[0][None], heads, min(2048, l))

    y2 = _outp(mixed, bf(w_out[0]), x1, m_lat[5],
               rows_per_mod=l, alpha=alpha, tm=tm_ffn, tn=min(1024, d))

    out = _ffn(y2, lg[1], lb[1], m_lat[6], m_lat[7], m_lat[8], ffn2_wi[0], ffn2_wo[0],
               lg[2], lb[2], rows_per_mod=l, pre="ln_affine", alpha=alpha, tm=tm_ffn, tf=tf)
    return out.reshape(b, l, d)
```

```python
import functools
import math

import numpy as np
import jax
import jax.numpy as jnp
from jax import lax
from jax.experimental import pallas as pl
from jax.experimental.pallas import tpu as pltpu

GRID_W = 64
ML_HEADS = 4
HY_ORDER = 2
HY_BANDS = 16
HY_DECAY_TARGET = 1e-2
HY_FAST_PCT = 0.3
HY_SLOW_PCT = 1.5
LN_EPS = 1e-5

V7X_VMEM_BYTES = 64 * 1024 * 1024
V7X_LANES = 128
V7X_SUBLANES = 8
V7X_MXU_DIM = 256
VMEM_LIMIT_BYTES = V7X_VMEM_BYTES - 4 * 1024 * 1024

DFT_N2 = 128
DFT_GROUPS = DFT_N2 // V7X_SUBLANES
ROW_CHUNK = 128
ROW_DMA_CHUNKS = 4

F32 = jnp.float32
BF16 = jnp.bfloat16


def _params(*semantics):
    return pltpu.CompilerParams(dimension_semantics=semantics, vmem_limit_bytes=VMEM_LIMIT_BYTES)


def _ln(x):
    mu = jnp.mean(x, axis=-1, keepdims=True)
    xc = x - mu
    var = jnp.mean(xc * xc, axis=-1, keepdims=True)
    return xc * lax.rsqrt(var + LN_EPS)


def _silu(x):
    return x * jax.nn.sigmoid(x)


def _ada_kernel(c_ref, w_ref, b_ref, o_ref):
    s = _silu(c_ref[...]).astype(BF16)
    o_ref[...] = jnp.dot(s, w_ref[...].astype(BF16), preferred_element_type=F32) + b_ref[...]


def _ada(cvec, w, b, tn):
    rows, d = cvec.shape
    n = w.shape[1]
    return pl.pallas_call(
        _ada_kernel,
        grid=(n // tn,),
        in_specs=[pl.BlockSpec((rows, d), lambda j: (0, 0)),
                  pl.BlockSpec((d, tn), lambda j: (0, j)),
                  pl.BlockSpec((1, tn), lambda j: (0, j))],
        out_specs=pl.BlockSpec((rows, tn), lambda j: (0, j)),
        out_shape=jax.ShapeDtypeStruct((rows, n), F32),
        compiler_params=_params("arbitrary"),
        name="ada",
    )(cvec, w, b)


def _for_chunk_rows(c, rows, body):
    size = min(ROW_CHUNK, rows)
    assert rows % size == 0

    def step(r, carry):
        body(pl.ds(pl.multiple_of(c * rows + r * size, size), size))
        return carry
    lax.fori_loop(0, rows // size, step, 0)


def _ffn_kernel(x_hbm, pg_ref, pb_ref, sh_ref, sc_ref, gt_ref, wg_ref, wu_ref, wo_ref,
                lg_ref, lb_ref, out_hbm, acc_ref, h_ref, in_sem, out_sem, *, pre, alpha):
    i = pl.program_id(0)
    f = pl.program_id(1)
    chunks = ROW_DMA_CHUNKS
    tm = acc_ref.shape[0]
    rows = tm // chunks

    def x_copy(c):
        return pltpu.make_async_copy(x_hbm.at[pl.ds(i * tm + c * rows, rows), :],
                                     acc_ref.at[pl.ds(c * rows, rows), :], in_sem.at[c])

    def out_copy(tile, c):
        return pltpu.make_async_copy(acc_ref.at[pl.ds(c * rows, rows), :],
                                     out_hbm.at[pl.ds(tile * tm + c * rows, rows), :], out_sem.at[c])

    def request(c):
        @pl.when(i > 0)
        def _():
            out_copy(i - 1, c).wait()
        x_copy(c).start()

    def prologue(c):
        x_copy(c).wait()

        def body(rs):
            x = acc_ref[rs, :]
            if pre != "none":
                x = _ln(x)
            if pre == "ln_affine":
                x = x * pg_ref[...] + pb_ref[...]
            h_ref[rs, :] = (x * (1.0 + sc_ref[...]) + sh_ref[...]).astype(BF16)
            acc_ref[rs, :] = alpha * x
        _for_chunk_rows(c, rows, body)

    @pl.when(f == 0)
    def _():
        for c in range(chunks - 1):
            request(c)
        prologue(0)
        request(chunks - 1)
        for c in range(1, chunks):
            prologue(c)

    h = h_ref[...]
    g = jnp.dot(h, wg_ref[...].astype(BF16), preferred_element_type=F32)
    u = jnp.dot(h, wu_ref[...].astype(BF16), preferred_element_type=F32)
    a = (_silu(g) * u).astype(BF16)
    acc_ref[...] += (0.5 * gt_ref[...]) * jnp.dot(a, wo_ref[...].astype(BF16),
                                                  preferred_element_type=F32)

    @pl.when(f == pl.num_programs(1) - 1)
    def _():
        def body(rs):
            acc_ref[rs, :] = _ln(acc_ref[rs, :]) * lg_ref[...] + lb_ref[...]
        for c in range(chunks):
            _for_chunk_rows(c, rows, body)
            out_copy(i, c).start()

        @pl.when(i == pl.num_programs(0) - 1)
        def _():
            for c in range(chunks):
                out_copy(i, c).wait()


def _ffn(x, pre_g, pre_b, shift, scale, gate, wi, wo, ln_g, ln_b, *, rows_per_mod, pre, alpha,
         tm, tf):
    n, d = x.shape
    ff = wo.shape[0]
    nf = ff // tf
    per = rows_per_mod // tm
    mod_spec = pl.BlockSpec((None, 1, d), lambda i, f: (i // per, 0, 0))
    vec_spec = pl.BlockSpec((1, d), lambda i, f: (0, 0))
    return pl.pallas_call(
        functools.partial(_ffn_kernel, pre=pre, alpha=alpha),
        grid=(n // tm, nf),
        in_specs=[pl.BlockSpec(memory_space=pl.ANY),
                  vec_spec, vec_spec,
                  mod_spec, mod_spec, mod_spec,
                  pl.BlockSpec((d, tf), lambda i, f: (0, f)),
                  pl.BlockSpec((d, tf), lambda i, f: (0, nf + f)),
                  pl.BlockSpec((tf, d), lambda i, f: (f, 0)),
                  vec_spec, vec_spec],
        out_specs=pl.BlockSpec(memory_space=pl.ANY),
        out_shape=jax.ShapeDtypeStruct((n, d), F32),
        scratch_shapes=[pltpu.VMEM((tm, d), F32),
                        pltpu.VMEM((tm, d), BF16),
                        pltpu.SemaphoreType.DMA((ROW_DMA_CHUNKS,)),
                        pltpu.SemaphoreType.DMA((ROW_DMA_CHUNKS,))],
        compiler_params=_params("arbitrary", "arbitrary"),
        name="ffn",
    )(x, pre_g, pre_b, shift, scale, gate, wi, wi, wo, ln_g, ln_b)


def _dot_nt(a, b_t):
    return lax.dot_general(a, b_t, (((1,), (1,)), ((), ())), preferred_element_type=F32)


def _proj_kernel(x_hbm, sh_ref, sc_ref, w_ref, wgate_ref, cw_ref, cb_ref, o_ref, og_ref,
                 x_ref, h_ref, sem, *, period):
    i = pl.program_id(0)
    tm = x_ref.shape[0]
    rows = tm // ROW_DMA_CHUNKS

    def x_copy(c):
        return pltpu.make_async_copy(x_hbm.at[pl.ds(i * tm + c * rows, rows), :],
                                     x_ref.at[pl.ds(c * rows, rows), :], sem.at[c])

    @pl.when(pl.program_id(1) == 0)
    def _():
        for c in range(ROW_DMA_CHUNKS):
            x_copy(c).start()

        def body(rs):
            h_ref[rs, :] = (x_ref[rs, :] * (1.0 + sc_ref[...]) + sh_ref[...]).astype(BF16)
        for c in range(ROW_DMA_CHUNKS):
            x_copy(c).wait()
            _for_chunk_rows(c, rows, body)
        og_ref[...] = _dot_nt(h_ref[...], wgate_ref[...].astype(BF16))

    r = _dot_nt(h_ref[...], w_ref[...].astype(BF16))
    pos = lax.broadcasted_iota(jnp.int32, (tm, 1), 0) % period
    prev = jnp.where(pos == 0, 0.0, pltpu.roll(r, 1, 0))
    nxt = jnp.where(pos == period - 1, 0.0, pltpu.roll(r, tm - 1, 0))
    cw = cw_ref[...]
    o_ref[...] = prev * cw[0:1] + r * cw[1:2] + nxt * cw[2:3] + cb_ref[...]


def _proj(x, shift, scale, w_t, w_block, gate_row0, n_gates, conv_w, conv_b, *, rows_per_mod,
          period, tm, tn):
    n, d = x.shape
    p = conv_w.shape[1]
    per = rows_per_mod // tm
    mod_spec = pl.BlockSpec((None, 1, d), lambda i, j: (i // per, 0, 0))
    return pl.pallas_call(
        functools.partial(_proj_kernel, period=period),
        grid=(n // tm, p // tn),
        in_specs=[pl.BlockSpec(memory_space=pl.ANY),
                  mod_spec, mod_spec,
                  pl.BlockSpec((None, tn, d), lambda i, j: (0, w_block(j), 0)),
                  pl.BlockSpec((None, n_gates, d), lambda i, j: (0, gate_row0 // n_gates, 0)),
                  pl.BlockSpec((V7X_SUBLANES, tn), lambda i, j: (0, j)),
                  pl.BlockSpec((1, tn), lambda i, j: (0, j))],
        out_specs=[pl.BlockSpec((tm, tn), lambda i, j: (i, j)),
                   pl.BlockSpec((tm, n_gates), lambda i, j: (i, 0))],
        out_shape=[jax.ShapeDtypeStruct((n, p), F32),
                   jax.ShapeDtypeStruct((n, n_gates), F32)],
        scratch_shapes=[pltpu.VMEM((tm, d), F32),
                        pltpu.VMEM((tm, d), BF16),
                        pltpu.SemaphoreType.DMA((ROW_DMA_CHUNKS,))],
        compiler_params=_params("arbitrary", "arbitrary"),
        name="proj",
    )(x, shift, scale, w_t, w_t, conv_w, conv_b)


def _filt_kernel(z_ref, w1_ref, b1_ref, f1_ref, w2_ref, b2_ref, f2_ref, w3_ref, dl_ref,
                 h_ref, asum_ref):
    hi = lax.Precision.HIGHEST
    z = z_ref[...]
    a = jnp.sin(f1_ref[...] * (jnp.dot(z, w1_ref[...], precision=hi, preferred_element_type=F32)
                               + b1_ref[...]))
    a = jnp.sin(f2_ref[...] * (jnp.dot(a, w2_ref[...], precision=hi, preferred_element_type=F32)
                               + b2_ref[...]))
    split = lambda v: (v.astype(BF16), (v - v.astype(BF16).astype(F32)).astype(BF16))
    (a_hi, a_lo), (w_hi, w_lo) = split(a), split(w3_ref[...])
    h = (jnp.dot(a_hi, w_hi, preferred_element_type=F32)
         + jnp.dot(a_hi, w_lo, preferred_element_type=F32)
         + jnp.dot(a_lo, w_hi, preferred_element_type=F32))
    h = h * jnp.exp(-z[:, 0:1] * dl_ref[...])

    @pl.when(pl.program_id(0) == 0)
    def _():
        asum_ref[...] = jnp.zeros_like(asum_ref)

    h_ref[...] = h
    asum_ref[...] += jnp.sum(jnp.abs(h), axis=0, keepdims=True)


def _filters(z, w1, b1, f1, w2, b2, f2, w3, deltas, tl):
    l, ze = z.shape
    fh = w2.shape[0]
    n = w3.shape[1]
    full = lambda shape: pl.BlockSpec(shape, lambda i: (0, 0))
    return pl.pallas_call(
        _filt_kernel,
        grid=(l // tl,),
        in_specs=[pl.BlockSpec((tl, ze), lambda i: (i, 0)),
                  full((ze, fh)), full((1, fh)), full((1, fh)),
                  full((fh, fh)), full((1, fh)), full((1, fh)),
                  full((fh, n)), full((1, n))],
        out_specs=[pl.BlockSpec((tl, n), lambda i: (i, 0)), full((1, n))],
        out_shape=[jax.ShapeDtypeStruct((l, n), F32), jax.ShapeDtypeStruct((1, n), F32)],
        compiler_params=_params("arbitrary"),
        name="filt",
    )(z, w1, b1, f1, w2, b2, f2, w3, deltas)


@functools.lru_cache(maxsize=None)
def _dft_constants(seq_len):
    n = 2 * seq_len
    nh = n // DFT_N2 // 2
    t1 = np.arange(nh)[None, :]
    k1 = np.arange(nh)[:, None]
    th = 2.0 * np.pi * (t1 * k1) / (2 * nh)
    fwd = np.stack([np.cos(th), -np.sin(th)], axis=1).reshape(2 * nh, nh)
    pair = np.where(k1 == 0, 1.0, 2.0) / n
    inv = np.stack([pair * np.cos(th), -pair * np.sin(th)], axis=1).reshape(2 * nh, nh).T
    eye = np.eye(V7X_SUBLANES)
    fa = np.kron(fwd, eye)
    fa_inv = np.kron(inv, eye)
    t2 = np.arange(DFT_N2)[None, None, :]
    k2 = np.arange(DFT_N2)[None, :, None]
    ph = 2.0 * np.pi * (t2 * k2 / DFT_N2 + t2 * np.arange(nh + 1)[:, None, None] / n)
    mr, mi = np.cos(ph), -np.sin(ph)
    mb = np.concatenate([np.concatenate([mr, -mi], axis=2),
                         np.concatenate([mi, mr], axis=2)], axis=1)
    as_bf16 = lambda a: jnp.asarray(a, dtype=F32).astype(BF16)
    return as_bf16(fa), as_bf16(fa_inv), as_bf16(mb)


def _units_per_step(units):
    return max(k for k in range(1, 12) if units % k == 0)


def _nyquist_sign(nh):
    t1 = lax.broadcasted_iota(jnp.int32, (nh, 1, 1), 0)
    return jnp.where(t1 % 2 == 0, 1.0, -1.0).astype(F32)


def _stage_a(fa_ref, u, s_ref):
    nh = fa_ref.shape[1] // V7X_SUBLANES
    lanes = s_ref.shape[-1]
    sign = _nyquist_sign(nh)

    def body(g, carry):
        rhs = u(g)
        out = jnp.dot(fa_ref[...], rhs.reshape(nh * V7X_SUBLANES, lanes).astype(BF16),
                      preferred_element_type=F32)
        s_ref[0:2 * nh, g, :, :] = out.reshape(2 * nh, V7X_SUBLANES, lanes)
        s_ref[2 * nh, g, :, :] = jnp.sum(rhs * sign, axis=0)
        s_ref[2 * nh + 1, g, :, :] = jnp.zeros((V7X_SUBLANES, lanes), F32)
        return carry
    lax.fori_loop(0, DFT_GROUPS, body, 0, unroll=True)


def _spec_kernel(hf_ref, hb_ref, nf_ref, nb_ref, fa_ref, mb_ref, g_ref, s_ref, *, kb):
    s = pl.program_id(2)
    cb = hf_ref.shape[-1]

    @pl.when(s == 0)
    def _():
        def both(g):
            hf, hb = hf_ref[:, g, :, :], hb_ref[:, g, :, :]
            return jnp.concatenate([hf + hb, hf - hb], axis=-1)
        _stage_a(fa_ref, both, s_ref)

    @pl.when(s > 0)
    def _():
        inv_norm = 1.0 / (nf_ref[...] + nb_ref[...])
        half = DFT_N2
        for i in range(kb):
            k1 = (s - 1) * kb + i
            a = s_ref[pl.ds(2 * k1, 2), :, :, :].reshape(2 * half, 2 * cb).astype(BF16)
            z = jnp.dot(mb_ref[k1], a, preferred_element_type=F32)
            g_ref[i, 0:half, :] = (z[0:half, 0:cb] * inv_norm).astype(g_ref.dtype)
            g_ref[i, half:, :] = (z[half:, cb:] * inv_norm).astype(g_ref.dtype)


def _spectra(hraw, asum, consts, hy_w, cb):
    fa, _, mb = consts
    units = mb.shape[0]
    nh = units - 1
    kb = _units_per_step(units)
    ncb = hy_w // cb
    h5 = hraw.reshape(nh, DFT_GROUPS, V7X_SUBLANES, hraw.shape[1])
    blk = (nh, DFT_GROUPS, V7X_SUBLANES, cb)
    kmap = lambda s: jnp.maximum(s - 1, 0)
    return pl.pallas_call(
        functools.partial(_spec_kernel, kb=kb),
        grid=(HY_ORDER, ncb, 1 + units // kb),
        in_specs=[pl.BlockSpec(blk, lambda o, c, s: (0, 0, 0, (2 * o) * ncb + c)),
                  pl.BlockSpec(blk, lambda o, c, s: (0, 0, 0, (2 * o + 1) * ncb + c)),
                  pl.BlockSpec((1, cb), lambda o, c, s: (0, (2 * o) * ncb + c)),
                  pl.BlockSpec((1, cb), lambda o, c, s: (0, (2 * o + 1) * ncb + c)),
                  pl.BlockSpec(fa.shape, lambda o, c, s: (0, 0)),
                  pl.BlockSpec(mb.shape, lambda o, c, s: (0, 0, 0), pipeline_mode=pl.Buffered(1))],
        out_specs=pl.BlockSpec((None, kb, 2 * DFT_N2, cb), lambda o, c, s: (o, kmap(s), 0, c)),
        out_shape=jax.ShapeDtypeStruct((HY_ORDER, units, 2 * DFT_N2, hy_w), BF16),
        scratch_shapes=[pltpu.VMEM((2 * units, DFT_GROUPS, V7X_SUBLANES, 2 * cb), F32)],
        compiler_params=_params("arbitrary", "arbitrary", "arbitrary"),
        name="spec",
    )(h5, h5, asum, asum, fa, mb)


def _conv_kernel(u_ref, x_ref, bias_ref, fa_ref, fai_ref, mb_ref, g_ref, o_ref, s_ref, *, kb):
    lanes = s_ref.shape[-1]
    half = DFT_N2
    nh = fa_ref.shape[1] // V7X_SUBLANES

    _stage_a(fa_ref, lambda g: u_ref[:, g, :, :], s_ref)

    def unit(k1, carry):
        a = s_ref[pl.ds(2 * k1, 2), :, :, :].reshape(2 * half, lanes).astype(BF16)
        z = jnp.dot(mb_ref[k1], a, preferred_element_type=F32)
        zr, zi = z[0:half], z[half:]
        gr = g_ref[k1, 0:half, :].astype(F32)
        gi = g_ref[k1, half:, :].astype(F32)
        p = jnp.concatenate([zr * gr - zi * gi, zr * gi + zi * gr], axis=0).astype(BF16)
        y = lax.dot_general(mb_ref[k1], p, (((0,), (0,)), ((), ())), preferred_element_type=F32)
        s_ref[pl.ds(2 * k1, 2), :, :, :] = y.reshape(2, DFT_GROUPS, V7X_SUBLANES, lanes)
        return carry
    lax.fori_loop(0, nh + 1, unit, 0, unroll=kb)

    n = 2 * nh * DFT_N2
    sign = _nyquist_sign(nh) * (1.0 / n)

    def body(g, carry):
        rhs = s_ref[0:2 * nh, g, :, :].reshape(2 * nh * V7X_SUBLANES, lanes)
        conv = jnp.dot(fai_ref[...], rhs.astype(BF16), preferred_element_type=F32)
        conv = conv.reshape(nh, V7X_SUBLANES, lanes) + sign * s_ref[2 * nh, g, :, :]
        u = u_ref[:, g, :, :]
        o_ref[:, g, :, :] = x_ref[:, g, :, :] * (conv + bias_ref[...] * u)
        return carry
    lax.fori_loop(0, DFT_GROUPS, body, 0, unroll=True)


def _longconv_gate(u_src, u_col, x_src, x_col, bias, spectra, order, consts, hy_w, cb):
    fa, fa_inv, mb = consts
    b, l = u_src.shape[:2]
    units = mb.shape[0]
    nh = units - 1
    kb = _units_per_step(units)
    ncb = hy_w // cb
    view = lambda a: a.reshape(b, nh, DFT_GROUPS, V7X_SUBLANES, a.shape[2])
    blk = (None, nh, DFT_GROUPS, V7X_SUBLANES, cb)
    once = pl.Buffered(1)
    out = pl.pallas_call(
        functools.partial(_conv_kernel, kb=kb),
        grid=(ncb, b),
        in_specs=[pl.BlockSpec(blk, lambda c, i: (i, 0, 0, 0, u_col // cb + c)),
                  pl.BlockSpec(blk, lambda c, i: (i, 0, 0, 0, x_col // cb + c)),
                  pl.BlockSpec((1, cb), lambda c, i: (0, c)),
                  pl.BlockSpec(fa.shape, lambda c, i: (0, 0), pipeline_mode=once),
                  pl.BlockSpec(fa_inv.shape, lambda c, i: (0, 0), pipeline_mode=once),
                  pl.BlockSpec(mb.shape, lambda c, i: (0, 0, 0), pipeline_mode=once),
                  pl.BlockSpec((None, units, 2 * DFT_N2, cb), lambda c, i: (order, 0, 0, c),
                               pipeline_mode=once)],
        out_specs=pl.BlockSpec(blk, lambda c, i: (i, 0, 0, 0, c)),
        out_shape=jax.ShapeDtypeStruct((b, nh, DFT_GROUPS, V7X_SUBLANES, hy_w), F32),
        scratch_shapes=[pltpu.VMEM((2 * units, DFT_GROUPS, V7X_SUBLANES, cb), F32)],
        compiler_params=_params("arbitrary", "arbitrary"),
        name="conv",
    )(view(u_src), view(x_src), bias, fa, fa_inv, mb, spectra)
    return out.reshape(b, l, hy_w)


def _mlstm_dir(q, k, v_aug, ig_col, b_col, ig_row, b_row, b_end, keep, ct_ref, m_ref, idx):
    m = m_ref[idx]
    ct = ct_ref[idx]
    end_row = b_end - b_row + ig_row
    end_col = b_end - b_col + ig_col
    m_new = jnp.maximum(b_end + m, jnp.max(end_row, axis=1, keepdims=True))
    h = None
    if q is not None:
        dlog = jnp.where(keep, b_col - b_row + ig_row, -jnp.inf)
        inter = b_col + m
        m_j = jnp.maximum(inter, jnp.max(dlog, axis=1, keepdims=True))
        s = lax.dot_general(q, k, (((1,), (1,)), ((), ())), preferred_element_type=F32)
        p = (s * jnp.exp(dlog - m_j)).astype(BF16)
        nd = (jnp.dot(p, v_aug, preferred_element_type=F32)
              + jnp.exp(inter - m_j) * jnp.dot(q, ct.astype(BF16), preferred_element_type=F32))
        dv = nd.shape[1] - V7X_LANES
        den = nd[:, dv:dv + 1]
        h = nd[:, :dv] / jnp.maximum(jnp.abs(den), jnp.exp(-m_j))
    w_state = jnp.exp(end_col - m_new)
    decay = jnp.exp(b_end + m - m_new)
    vw = (v_aug.astype(F32) * w_state).astype(BF16)
    ct_ref[idx] = decay * ct + lax.dot_general(k, vw, (((0,), (0,)), ((), ())),
                                               preferred_element_type=F32)
    m_ref[idx] = m_new
    return h


def _mlstm_kernel(qf_ref, kf_ref, vf_ref, gf_ref, gtf_ref,
                  qb_ref, kb_ref, vb_ref, gb_ref, gtb_ref,
                  vc_ref, kc_ref, gc_ref, gtc_ref, bias_col_ref, bias_row_ref,
                  hf_ref, hb_ref, ct_ref, m_ref, *, heads):
    i = pl.program_id(1)
    hi = lax.Precision.HIGHEST
    lc = kc_ref.shape[0]
    dqk = kc_ref.shape[1] // heads
    dv = vc_ref.shape[1] // heads
    ng = 4 * heads
    row = lax.broadcasted_iota(jnp.int32, (lc, lc), 0)
    col = lax.broadcasted_iota(jnp.int32, (lc, lc), 1)
    lower = row >= col
    upper = row <= col
    tri_lo = lower.astype(F32)
    tri_up = upper.astype(F32)
    ones = jnp.ones((lc, V7X_LANES), BF16)

    def run(direction, q_ref, k_ref, v_ref, g_ref, gt_ref, out_ref):
        gates = g_ref[...] + bias_col_ref[...]
        gates_t = gt_ref[...] + bias_row_ref[...]
        tri_col, tri_row, keep = (tri_lo, tri_up, lower) if direction == 0 else (tri_up, tri_lo, upper)
        lf_t = jax.nn.log_sigmoid(gates_t)
        b_cols = jnp.dot(tri_col, jax.nn.log_sigmoid(gates), precision=hi, preferred_element_type=F32)
        b_rows = jnp.dot(lf_t, tri_row, precision=hi, preferred_element_type=F32)
        b_ends = jnp.sum(lf_t, axis=1, keepdims=True)
        for h in range(heads):
            ci = direction * 2 * heads + h
            cf = ci + heads
            k = k_ref[:, h * dqk:(h + 1) * dqk].astype(BF16)
            v_aug = jnp.concatenate([v_ref[:, h * dv:(h + 1) * dv].astype(BF16), ones], axis=1)
            q = None if q_ref is None else q_ref[:, h * dqk:(h + 1) * dqk].astype(BF16)
            out = _mlstm_dir(q, k, v_aug, gates[:, ci:ci + 1], b_cols[:, cf:cf + 1],
                             gates_t[ci:ci + 1, :], b_rows[cf:cf + 1, :], b_ends[cf:cf + 1, :],
                             keep, ct_ref, m_ref, direction * heads + h)
            if out_ref is not None:
                out_ref[:, h * dv:(h + 1) * dv] = out

    @pl.when(i == 0)
    def _():
        ct_ref[...] = jnp.zeros_like(ct_ref)
        m_ref[...] = jnp.zeros_like(m_ref)
        run(0, None, kc_ref, vc_ref, gc_ref, gtc_ref, None)
        run(1, None, kc_ref, vc_ref, gc_ref, gtc_ref, None)

    @pl.when(i > 0)
    def _():
        run(0, qf_ref, kf_ref, vf_ref, gf_ref, gtf_ref, hf_ref)
        run(1, qb_ref, kb_ref, vb_ref, gb_ref, gtb_ref, hb_ref)


def _mlstm(p_lat, g_lat, gt_lat, p_ctx, g_ctx, gt_ctx, gate_b, *, q_col, k_col, v_col,
           kc_col, vc_col, heads, dqk, dv):
    b, l = p_lat.shape[:2]
    lc = p_ctx.shape[1]
    nc = l // lc
    qk_w, v_w = heads * dqk, heads * dv
    ng = 4 * heads
    fwd = lambda i: jnp.maximum(i - 1, 0)
    bwd = lambda i: nc - jnp.maximum(i, 1)

    def lat_specs(chunk):
        return [pl.BlockSpec((None, lc, qk_w), lambda bi, i: (bi, chunk(i), q_col // qk_w)),
                pl.BlockSpec((None, lc, qk_w), lambda bi, i: (bi, chunk(i), k_col // qk_w)),
                pl.BlockSpec((None, lc, v_w), lambda bi, i: (bi, chunk(i), v_col // v_w)),
                pl.BlockSpec((None, lc, ng), lambda bi, i: (bi, chunk(i), 0)),
                pl.BlockSpec((None, ng, lc), lambda bi, i: (bi, 0, chunk(i)))]

    ctx_specs = [pl.BlockSpec((None, lc, v_w), lambda bi, i: (bi, 0, vc_col // v_w)),
                 pl.BlockSpec((None, lc, qk_w), lambda bi, i: (bi, 0, kc_col // qk_w)),
                 pl.BlockSpec((None, lc, ng), lambda bi, i: (bi, 0, 0)),
                 pl.BlockSpec((None, ng, lc), lambda bi, i: (bi, 0, 0))]
    bias_specs = [pl.BlockSpec((1, ng), lambda bi, i: (0, 0)),
                  pl.BlockSpec((ng, 1), lambda bi, i: (0, 0))]
    out_sds = jax.ShapeDtypeStruct((b, l, v_w), F32)
    return pl.pallas_call(
        functools.partial(_mlstm_kernel, heads=heads),
        grid=(b, nc + 1),
        in_specs=lat_specs(fwd) + lat_specs(bwd) + ctx_specs + bias_specs,
        out_specs=[pl.BlockSpec((None, lc, v_w), lambda bi, i: (bi, fwd(i), 0)),
                   pl.BlockSpec((None, lc, v_w), lambda bi, i: (bi, bwd(i), 0))],
        out_shape=[out_sds, out_sds],
        scratch_shapes=[pltpu.VMEM((2 * heads, dqk, dv + V7X_LANES), F32),
                        pltpu.VMEM((2 * heads, 1, 1), F32)],
        compiler_params=_params("arbitrary", "arbitrary"),
        name="mlstm",
    )(p_lat, p_lat, p_lat, g_lat, gt_lat, p_lat, p_lat, p_lat, g_lat, gt_lat,
      p_ctx, p_ctx, g_ctx, gt_ctx, gate_b.reshape(1, ng), gate_b.reshape(ng, 1))


def _merge_kernel(hy_ref, hf_ref, hb_ref, o_ref, w_ref, out_ref, *, hy_blocks):
    j = pl.program_id(1)

    @pl.when(j < hy_blocks)
    def _():
        out_ref[...] = hy_ref[...].astype(out_ref.dtype)

    @pl.when(j >= hy_blocks)
    def _():
        h = _ln(hf_ref[...] + hb_ref[...])
        out_ref[...] = (h * w_ref[...] * jax.nn.sigmoid(o_ref[...])).astype(out_ref.dtype)


def _merge(hy, hf, hb, p_lat, o_col, norm_w, heads, tm):
    n, w = hf.shape
    dv = w // heads
    hy_blocks = hy.shape[1] // dv
    head = lambda j: jnp.maximum(j - hy_blocks, 0)
    return pl.pallas_call(
        functools.partial(_merge_kernel, hy_blocks=hy_blocks),
        grid=(n // tm, hy_blocks + heads),
        in_specs=[pl.BlockSpec((tm, dv), lambda i, j: (i, jnp.minimum(j, hy_blocks - 1))),
                  pl.BlockSpec((tm, dv), lambda i, j: (i, head(j))),
                  pl.BlockSpec((tm, dv), lambda i, j: (i, head(j))),
                  pl.BlockSpec((tm, dv), lambda i, j: (i, o_col // dv + head(j))),
                  pl.BlockSpec((1, dv), lambda i, j: (0, head(j)))],
        out_specs=pl.BlockSpec((tm, dv), lambda i, j: (i, j)),
        out_shape=jax.ShapeDtypeStruct((n, hy.shape[1] + w), BF16),
        compiler_params=_params("arbitrary", "arbitrary"),
        name="merge",
    )(hy, hf, hb, p_lat, norm_w)


def _outp_kernel(a_ref, w_ref, x_ref, gt_ref, o_ref, *, alpha):
    y = jnp.dot(a_ref[...], w_ref[...], preferred_element_type=F32)
    o_ref[...] = alpha * x_ref[...] + gt_ref[...] * y


def _outp(a, w, x, gate, *, rows_per_mod, alpha, tm, tn):
    n, d = x.shape
    k = a.shape[1]
    per = rows_per_mod // tm
    return pl.pallas_call(
        functools.partial(_outp_kernel, alpha=alpha),
        grid=(d // tn, n // tm),
        in_specs=[pl.BlockSpec((tm, k), lambda j, i: (i, 0)),
                  pl.BlockSpec((k, tn), lambda j, i: (0, j)),
                  pl.BlockSpec((tm, tn), lambda j, i: (i, j)),
                  pl.BlockSpec((None, 1, tn), lambda j, i: (i // per, 0, j))],
        out_specs=pl.BlockSpec((tm, tn), lambda j, i: (i, j)),
        out_shape=jax.ShapeDtypeStruct((n, d), F32),
        compiler_params=_params("arbitrary", "arbitrary"),
        name="outp",
    )(a, w, x, gate)


def kernel(x, c, ctx, c_ctx, ada_w, ada_b, ln_g, ln_b, ffn1_wi, ffn1_wo, ffn2_wi, ffn2_wo,
           w_in, hy_conv_w, hy_conv_b, hy_filt_w1, hy_filt_b1, hy_filt_f1, hy_filt_w2,
           hy_filt_b2, hy_filt_f2, hy_filt_w3, hy_bias, ml_conv_w, ml_conv_b, ml_gate_b,
           ml_norm_w, w_out):
    depth = ada_w.shape[0]
    assert depth == 1, "only the depth-1 block is implemented"
    b, l, d = x.shape
    lc = ctx.shape[1]
    heads = ML_HEADS
    hy_w = hy_bias.shape[2]
    ml_w = ml_norm_w.shape[1]
    dv = ml_w // heads
    ml_qk = ml_conv_w.shape[2] // 2
    dqk = ml_qk // heads
    p_hy = 3 * hy_w
    p_state0 = p_hy + ml_qk + ml_w
    p_main = p_state0 + ml_qk + ml_w
    n_gates = 4 * heads
    alpha = (2.0 * depth) ** 0.25
    assert l % GRID_W == 0 and l % lc == 0 and lc % GRID_W == 0

    tm = min(512, l)
    tm_ctx = min(512, b * lc)
    tm_ffn = min(1024, l)
    tm_ffn_ctx = min(1024, b * lc)

    cvec = jnp.concatenate([c, c_ctx[None], jnp.zeros((V7X_SUBLANES - (b + 1) % V7X_SUBLANES, d), F32)])
    mods = _ada(cvec, ada_w.reshape(d, 9 * d), ada_b.reshape(1, 9 * d), tn=min(512, d))
    mods = mods.reshape(cvec.shape[0], 9, 1, d)
    m_lat = [mods[:b, k] for k in range(9)]
    m_ctx = [mods[b:b + 1, k] for k in range(9)]

    lg = [ln_g[0, k][None] for k in range(3)]
    lb = [ln_b[0, k][None] for k in range(3)]
    bf = lambda a: a.astype(BF16)
    x2 = x.reshape(b * l, d)
    ctx2 = ctx.reshape(b * lc, d)

    wi1, wo1 = ffn1_wi[0], ffn1_wo[0]
    tf = 256
    x1 = _ffn(x2, lg[0], lb[0], m_lat[0], m_lat[1], m_lat[2], wi1, wo1, lg[0], lb[0],
              rows_per_mod=l, pre="ln", alpha=alpha, tm=tm_ffn, tf=tf)
    c1 = _ffn(ctx2, lg[0], lb[0], m_ctx[0], m_ctx[1], m_ctx[2], wi1, wo1, lg[0], lb[0],
              rows_per_mod=b * lc, pre="ln", alpha=alpha, tm=tm_ffn_ctx, tf=tf)

    w_in_t = jnp.swapaxes(w_in, 1, 2)
    q_scale = dqk ** -0.5
    ident = jnp.array([0.0, 1.0, 0.0], F32)[:, None]
    taps = jnp.concatenate([
        hy_conv_w[0], ml_conv_w[0][:, :ml_qk] * q_scale, jnp.tile(ident, (1, ml_w)),
        ml_conv_w[0][:, ml_qk:], jnp.tile(ident, (1, ml_w))], axis=1)
    taps = jnp.pad(taps, ((0, V7X_SUBLANES - 3), (0, 0)))
    tap_b = jnp.concatenate([
        hy_conv_b[0], ml_conv_b[0][:ml_qk] * q_scale, jnp.zeros((ml_w,), F32),
        ml_conv_b[0][ml_qk:], jnp.zeros((ml_w,), F32)])[None]
    tn = min(512, ml_qk)
    p_lat, g_lat = _proj(x1, m_lat[3], m_lat[4], w_in_t, lambda j: j, p_main, n_gates, taps, tap_b,
                         rows_per_mod=l, period=GRID_W, tm=tm_ffn, tn=tn)
    v0 = p_state0 + ml_qk
    ctx_cols = lambda a: jnp.concatenate([a[:, v0:p_main], a[:, p_state0:v0]], axis=1)
    nv = ml_w // tn
    ctx_block = lambda j: jnp.where(j < nv, j + v0 // tn, j - nv + p_state0 // tn)
    p_ctx, g_ctx = _proj(c1, m_ctx[3], m_ctx[4], w_in_t, ctx_block, p_main, n_gates,
                         ctx_cols(taps), ctx_cols(tap_b),
                         rows_per_mod=b * lc, period=lc, tm=tm_ctx, tn=tn)
    p_lat3 = p_lat.reshape(b, l, p_main)
    p_ctx3 = p_ctx.reshape(b, lc, p_main - p_state0)

    t = jnp.linspace(0.0, 1.0, l, dtype=F32)[:, None]
    w = (2.0 * math.pi / l) * jnp.arange(l, dtype=F32)[:, None]
    bands = jnp.linspace(1e-4, HY_BANDS - 1, HY_BANDS, dtype=F32)[None, :]
    z = jnp.concatenate([t, jnp.cos(bands * w), -jnp.sin(bands * w)], axis=-1)
    ze = V7X_LANES
    z = jnp.pad(z, ((0, 0), (0, ze - z.shape[1])))
    w1 = jnp.pad(hy_filt_w1[0], ((0, ze - hy_filt_w1.shape[1]), (0, 0)))
    max_decay = math.log(HY_DECAY_TARGET) / HY_FAST_PCT
    min_decay = math.log(HY_DECAY_TARGET) / HY_SLOW_PCT
    deltas = jnp.abs(jnp.linspace(min_decay, max_decay, hy_w, dtype=F32))
    deltas = jnp.tile(deltas, 2 * HY_ORDER)[None]
    hraw, asum = _filters(z, w1, hy_filt_b1[0][None], hy_filt_f1[0][None], hy_filt_w2[0],
                          hy_filt_b2[0][None], hy_filt_f2[0][None], hy_filt_w3[0], deltas,
                          tl=min(256, l))
    consts = _dft_constants(l)
    spectra = _spectra(hraw, asum, consts, hy_w, cb=V7X_LANES)
    cb = min(V7X_MXU_DIM, hy_w)
    z1 = _longconv_gate(p_lat3, 0, p_lat3, hy_w, hy_bias[0, 0][None], spectra, 0, consts, hy_w, cb)
    hy = _longconv_gate(z1, 0, p_lat3, 2 * hy_w, hy_bias[0, 1][None], spectra, 1, consts, hy_w, cb)

    g_lat, g_ctx = g_lat.reshape(b, l, n_gates), g_ctx.reshape(b, lc, n_gates)
    gt_lat, gt_ctx = jnp.swapaxes(g_lat, 1, 2), jnp.swapaxes(g_ctx, 1, 2)
    hf, hb = _mlstm(p_lat3, g_lat, gt_lat, p_ctx3, g_ctx, gt_ctx, ml_gate_b[0],
                    q_col=p_hy, k_col=p_state0, v_col=p_state0 + ml_qk,
                    kc_col=ml_w, vc_col=0, heads=heads, dqk=dqk, dv=dv)
    mixed = _merge(hy.reshape(b * l, hy_w), hf.reshape(b * l, ml_w), hb.reshape(b * l, ml_w),
                   p_lat, p_hy + ml_qk, ml_norm_w[0][None], heads, min(2048, l))

    y2 = _outp(mixed, bf(w_out[0]), x1, m_lat[5],
               rows_per_mod=l, alpha=alpha, tm=tm_ffn, tn=min(1024, d))

    out = _ffn(y2, lg[1], lb[1], m_lat[6], m_lat[7], m_lat[8], ffn2_wi[0], ffn2_wo[0],
               lg[2], lb[2], rows_per_mod=l, pre="ln_affine", alpha=alpha, tm=tm_ffn, tf=tf)
    return out.reshape(b, l, d)
```

```python
import functools
import math

import numpy as np
import jax
import jax.numpy as jnp
from jax import lax
from jax.experimental import pallas as pl
from jax.experimental.pallas import tpu as pltpu

GRID_W = 64
ML_HEADS = 4
HY_ORDER = 2
HY_BANDS = 16
HY_DECAY_TARGET = 1e-2
HY_FAST_PCT = 0.3
HY_SLOW_PCT = 1.5
LN_EPS = 1e-5

V7X_VMEM_BYTES = 64 * 1024 * 1024
V7X_LANES = 128
V7X_SUBLANES = 8
V7X_MXU_DIM = 256
VMEM_LIMIT_BYTES = V7X_VMEM_BYTES - 4 * 1024 * 1024

DFT_N2 = 128
DFT_GROUPS = DFT_N2 // V7X_SUBLANES
ROW_CHUNK = 128
ROW_DMA_CHUNKS = 4

F32 = jnp.float32
BF16 = jnp.bfloat16


def _params(*semantics):
    return pltpu.CompilerParams(dimension_semantics=semantics, vmem_limit_bytes=VMEM_LIMIT_BYTES)


def _ln(x):
    mu = jnp.mean(x, axis=-1, keepdims=True)
    xc = x - mu
    var = jnp.mean(xc * xc, axis=-1, keepdims=True)
    return xc * lax.rsqrt(var + LN_EPS)


def _silu(x):
    return x * jax.nn.sigmoid(x)


def _ada_kernel(c_ref, w_ref, b_ref, o_ref):
    s = _silu(c_ref[...]).astype(BF16)
    o_ref[...] = jnp.dot(s, w_ref[...].astype(BF16), preferred_element_type=F32) + b_ref[...]


def _ada(cvec, w, b, tn):
    rows, d = cvec.shape
    n = w.shape[1]
    return pl.pallas_call(
        _ada_kernel,
        grid=(n // tn,),
        in_specs=[pl.BlockSpec((rows, d), lambda j: (0, 0)),
                  pl.BlockSpec((d, tn), lambda j: (0, j)),
                  pl.BlockSpec((1, tn), lambda j: (0, j))],
        out_specs=pl.BlockSpec((rows, tn), lambda j: (0, j)),
        out_shape=jax.ShapeDtypeStruct((rows, n), F32),
        compiler_params=_params("arbitrary"),
        name="ada",
    )(cvec, w, b)


def _for_chunk_rows(c, rows, body):
    size = min(ROW_CHUNK, rows)
    assert rows % size == 0

    def step(r, carry):
        body(pl.ds(pl.multiple_of(c * rows + r * size, size), size))
        return carry
    lax.fori_loop(0, rows // size, step, 0)


def _ffn_kernel(x_hbm, pg_ref, pb_ref, sh_ref, sc_ref, gt_ref, wg_ref, wu_ref, wo_ref,
                lg_ref, lb_ref, out_hbm, acc_ref, h_ref, in_sem, out_sem, *, pre, alpha):
    i = pl.program_id(0)
    f = pl.program_id(1)
    chunks = ROW_DMA_CHUNKS
    tm = acc_ref.shape[0]
    rows = tm // chunks

    def x_copy(c):
        return pltpu.make_async_copy(x_hbm.at[pl.ds(i * tm + c * rows, rows), :],
                                     acc_ref.at[pl.ds(c * rows, rows), :], in_sem.at[c])

    def out_copy(tile, c):
        return pltpu.make_async_copy(acc_ref.at[pl.ds(c * rows, rows), :],
                                     out_hbm.at[pl.ds(tile * tm + c * rows, rows), :], out_sem.at[c])

    def request(c):
        @pl.when(i > 0)
        def _():
            out_copy(i - 1, c).wait()
        x_copy(c).start()

    def prologue(c):
        x_copy(c).wait()

        def body(rs):
            x = acc_ref[rs, :]
            if pre != "none":
                x = _ln(x)
            if pre == "ln_affine":
                x = x * pg_ref[...] + pb_ref[...]
            h_ref[rs, :] = (x * (1.0 + sc_ref[...]) + sh_ref[...]).astype(BF16)
            acc_ref[rs, :] = alpha * x
        _for_chunk_rows(c, rows, body)

    @pl.when(f == 0)
    def _():
        for c in range(chunks - 1):
            request(c)
        prologue(0)
        request(chunks - 1)
        for c in range(1, chunks):
            prologue(c)

    h = h_ref[...]
    g = jnp.dot(h, wg_ref[...].astype(BF16), preferred_element_type=F32)
    u = jnp.dot(h, wu_ref[...].astype(BF16), preferred_element_type=F32)
    a = (_silu(g) * u).astype(BF16)
    acc_ref[...] += (0.5 * gt_ref[...]) * jnp.dot(a, wo_ref[...].astype(BF16),
                                                  preferred_element_type=F32)

    @pl.when(f == pl.num_programs(1) - 1)
    def _():
        def body(rs):
            acc_ref[rs, :] = _ln(acc_ref[rs, :]) * lg_ref[...] + lb_ref[...]
        for c in range(chunks):
            _for_chunk_rows(c, rows, body)
            out_copy(i, c).start()

        @pl.when(i == pl.num_programs(0) - 1)
        def _():
            for c in range(chunks):
                out_copy(i, c).wait()


def _ffn(x, pre_g, pre_b, shift, scale, gate, wi, wo, ln_g, ln_b, *, rows_per_mod, pre, alpha,
         tm, tf):
    n, d = x.shape
    ff = wo.shape[0]
    nf = ff // tf
    per = rows_per_mod // tm
    mod_spec = pl.BlockSpec((None, 1, d), lambda i, f: (i // per, 0, 0))
    vec_spec = pl.BlockSpec((1, d), lambda i, f: (0, 0))
    return pl.pallas_call(
        functools.partial(_ffn_kernel, pre=pre, alpha=alpha),
        grid=(n // tm, nf),
        in_specs=[pl.BlockSpec(memory_space=pl.ANY),
                  vec_spec, vec_spec,
                  mod_spec, mod_spec, mod_spec,
                  pl.BlockSpec((d, tf), lambda i, f: (0, f)),
                  pl.BlockSpec((d, tf), lambda i, f: (0, nf + f)),
                  pl.BlockSpec((tf, d), lambda i, f: (f, 0)),
                  vec_spec, vec_spec],
        out_specs=pl.BlockSpec(memory_space=pl.ANY),
        out_shape=jax.ShapeDtypeStruct((n, d), F32),
        scratch_shapes=[pltpu.VMEM((tm, d), F32),
                        pltpu.VMEM((tm, d), BF16),
                        pltpu.SemaphoreType.DMA((ROW_DMA_CHUNKS,)),
                        pltpu.SemaphoreType.DMA((ROW_DMA_CHUNKS,))],
        compiler_params=_params("arbitrary", "arbitrary"),
        name="ffn",
    )(x, pre_g, pre_b, shift, scale, gate, wi, wi, wo, ln_g, ln_b)


def _dot_nt(a, b_t):
    return lax.dot_general(a, b_t, (((1,), (1,)), ((), ())), preferred_element_type=F32)


def _proj_kernel(x_hbm, sh_ref, sc_ref, w_ref, wgate_ref, cw_ref, cb_ref, o_ref, og_ref,
                 x_ref, h_ref, sem, *, period):
    i = pl.program_id(0)
    tm = x_ref.shape[0]
    rows = tm // ROW_DMA_CHUNKS

    def x_copy(c):
        return pltpu.make_async_copy(x_hbm.at[pl.ds(i * tm + c * rows, rows), :],
                                     x_ref.at[pl.ds(c * rows, rows), :], sem.at[c])

    @pl.when(pl.program_id(1) == 0)
    def _():
        for c in range(ROW_DMA_CHUNKS):
            x_copy(c).start()

        def body(rs):
            h_ref[rs, :] = (x_ref[rs, :] * (1.0 + sc_ref[...]) + sh_ref[...]).astype(BF16)
        for c in range(ROW_DMA_CHUNKS):
            x_copy(c).wait()
            _for_chunk_rows(c, rows, body)
        og_ref[...] = _dot_nt(h_ref[...], wgate_ref[...].astype(BF16))

    r = _dot_nt(h_ref[...], w_ref[...].astype(BF16))
    pos = lax.broadcasted_iota(jnp.int32, (tm, 1), 0) % period
    prev = jnp.where(pos == 0, 0.0, pltpu.roll(r, 1, 0))
    nxt = jnp.where(pos == period - 1, 0.0, pltpu.roll(r, tm - 1, 0))
    cw = cw_ref[...]
    o_ref[...] = prev * cw[0:1] + r * cw[1:2] + nxt * cw[2:3] + cb_ref[...]


def _proj(x, shift, scale, w_t, w_block, gate_row0, n_gates, conv_w, conv_b, *, rows_per_mod,
          period, tm, tn):
    n, d = x.shape
    p = conv_w.shape[1]
    per = rows_per_mod // tm
    mod_spec = pl.BlockSpec((None, 1, d), lambda i, j: (i // per, 0, 0))
    return pl.pallas_call(
        functools.partial(_proj_kernel, period=period),
        grid=(n // tm, p // tn),
        in_specs=[pl.BlockSpec(memory_space=pl.ANY),
                  mod_spec, mod_spec,
                  pl.BlockSpec((None, tn, d), lambda i, j: (0, w_block(j), 0)),
                  pl.BlockSpec((None, n_gates, d), lambda i, j: (0, gate_row0 // n_gates, 0)),
                  pl.BlockSpec((V7X_SUBLANES, tn), lambda i, j: (0, j)),
                  pl.BlockSpec((1, tn), lambda i, j: (0, j))],
        out_specs=[pl.BlockSpec((tm, tn), lambda i, j: (i, j)),
                   pl.BlockSpec((tm, n_gates), lambda i, j: (i, 0))],
        out_shape=[jax.ShapeDtypeStruct((n, p), F32),
                   jax.ShapeDtypeStruct((n, n_gates), F32)],
        scratch_shapes=[pltpu.VMEM((tm, d), F32),
                        pltpu.VMEM((tm, d), BF16),
                        pltpu.SemaphoreType.DMA((ROW_DMA_CHUNKS,))],
        compiler_params=_params("arbitrary", "arbitrary"),
        name="proj",
    )(x, shift, scale, w_t, w_t, conv_w, conv_b)


def _filt_kernel(z_ref, w1_ref, b1_ref, f1_ref, w2_ref, b2_ref, f2_ref, w3_ref, dl_ref,
                 h_ref, asum_ref):
    hi = lax.Precision.HIGHEST
    z = z_ref[...]
    a = jnp.sin(f1_ref[...] * (jnp.dot(z, w1_ref[...], precision=hi, preferred_element_type=F32)
                               + b1_ref[...]))
    a = jnp.sin(f2_ref[...] * (jnp.dot(a, w2_ref[...], precision=hi, preferred_element_type=F32)
                               + b2_ref[...]))
    split = lambda v: (v.astype(BF16), (v - v.astype(BF16).astype(F32)).astype(BF16))
    (a_hi, a_lo), (w_hi, w_lo) = split(a), split(w3_ref[...])
    h = (jnp.dot(a_hi, w_hi, preferred_element_type=F32)
         + jnp.dot(a_hi, w_lo, preferred_element_type=F32)
         + jnp.dot(a_lo, w_hi, preferred_element_type=F32))
    h = h * jnp.exp(-z[:, 0:1] * dl_ref[...])

    @pl.when(pl.program_id(0) == 0)
    def _():
        asum_ref[...] = jnp.zeros_like(asum_ref)

    h_ref[...] = h
    asum_ref[...] += jnp.sum(jnp.abs(h), axis=0, keepdims=True)


def _filters(z, w1, b1, f1, w2, b2, f2, w3, deltas, tl):
    l, ze = z.shape
    fh = w2.shape[0]
    n = w3.shape[1]
    full = lambda shape: pl.BlockSpec(shape, lambda i: (0, 0))
    return pl.pallas_call(
        _filt_kernel,
        grid=(l // tl,),
        in_specs=[pl.BlockSpec((tl, ze), lambda i: (i, 0)),
                  full((ze, fh)), full((1, fh)), full((1, fh)),
                  full((fh, fh)), full((1, fh)), full((1, fh)),
                  full((fh, n)), full((1, n))],
        out_specs=[pl.BlockSpec((tl, n), lambda i: (i, 0)), full((1, n))],
        out_shape=[jax.ShapeDtypeStruct((l, n), F32), jax.ShapeDtypeStruct((1, n), F32)],
        compiler_params=_params("arbitrary"),
        name="filt",
    )(z, w1, b1, f1, w2, b2, f2, w3, deltas)


@functools.lru_cache(maxsize=None)
def _dft_constants(seq_len):
    n = 2 * seq_len
    nh = n // DFT_N2 // 2
    t1 = np.arange(nh)[None, :]
    k1 = np.arange(nh)[:, None]
    th = 2.0 * np.pi * (t1 * k1) / (2 * nh)
    fwd = np.stack([np.cos(th), -np.sin(th)], axis=1).reshape(2 * nh, nh)
    pair = np.where(k1 == 0, 1.0, 2.0) / n
    inv = np.stack([pair * np.cos(th), -pair * np.sin(th)], axis=1).reshape(2 * nh, nh).T
    eye = np.eye(V7X_SUBLANES)
    fa = np.kron(fwd, eye)
    fa_inv = np.kron(inv, eye)
    t2 = np.arange(DFT_N2)[None, None, :]
    k2 = np.arange(DFT_N2)[None, :, None]
    ph = 2.0 * np.pi * (t2 * k2 / DFT_N2 + t2 * np.arange(nh + 1)[:, None, None] / n)
    mr, mi = np.cos(ph), -np.sin(ph)
    mb = np.concatenate([np.concatenate([mr, -mi], axis=2),
                         np.concatenate([mi, mr], axis=2)], axis=1)
    as_bf16 = lambda a: jnp.asarray(a, dtype=F32).astype(BF16)
    return as_bf16(fa), as_bf16(fa_inv), as_bf16(mb)


def _units_per_step(units):
    return max(k for k in range(1, 12) if units % k == 0)


def _nyquist_sign(nh):
    t1 = lax.broadcasted_iota(jnp.int32, (nh, 1, 1), 0)
    return jnp.where(t1 % 2 == 0, 1.0, -1.0).astype(F32)


def _stage_a(fa_ref, u, s_ref):
    nh = fa_ref.shape[1] // V7X_SUBLANES
    lanes = s_ref.shape[-1]
    sign = _nyquist_sign(nh)

    def body(g, carry):
        rhs = u(g)
        out = jnp.dot(fa_ref[...], rhs.reshape(nh * V7X_SUBLANES, lanes).astype(BF16),
                      preferred_element_type=F32)
        s_ref[0:2 * nh, g, :, :] = out.reshape(2 * nh, V7X_SUBLANES, lanes)
        s_ref[2 * nh, g, :, :] = jnp.sum(rhs * sign, axis=0)
        s_ref[2 * nh + 1, g, :, :] = jnp.zeros((V7X_SUBLANES, lanes), F32)
        return carry
    lax.fori_loop(0, DFT_GROUPS, body, 0, unroll=True)


def _spec_kernel(hf_ref, hb_ref, nf_ref, nb_ref, fa_ref, mb_ref, g_ref, s_ref, *, kb):
    cb = hf_ref.shape[-1]
    half = DFT_N2

    def both(g):
        hf, hb = hf_ref[:, g, :, :], hb_ref[:, g, :, :]
        return jnp.concatenate([hf + hb, hf - hb], axis=-1)
    _stage_a(fa_ref, both, s_ref)

    inv_norm = 1.0 / (nf_ref[...] + nb_ref[...])

    def unit(k1, carry):
        a = s_ref[pl.ds(2 * k1, 2), :, :, :].reshape(2 * half, 2 * cb).astype(BF16)
        z = jnp.dot(mb_ref[k1], a, preferred_element_type=F32)
        g_ref[k1, 0:half, :] = (z[0:half, 0:cb] * inv_norm).astype(g_ref.dtype)
        g_ref[k1, half:, :] = (z[half:, cb:] * inv_norm).astype(g_ref.dtype)
        return carry
    lax.fori_loop(0, mb_ref.shape[0], unit, 0, unroll=kb)


def _spectra(hraw, asum, consts, hy_w, cb):
    fa, _, mb = consts
    units = mb.shape[0]
    nh = units - 1
    kb = _units_per_step(units)
    ncb = hy_w // cb
    h5 = hraw.reshape(nh, DFT_GROUPS, V7X_SUBLANES, hraw.shape[1])
    blk = (nh, DFT_GROUPS, V7X_SUBLANES, cb)
    once = pl.Buffered(1)
    return pl.pallas_call(
        functools.partial(_spec_kernel, kb=kb),
        grid=(HY_ORDER, ncb),
        in_specs=[pl.BlockSpec(blk, lambda o, c: (0, 0, 0, (2 * o) * ncb + c)),
                  pl.BlockSpec(blk, lambda o, c: (0, 0, 0, (2 * o + 1) * ncb + c)),
                  pl.BlockSpec((1, cb), lambda o, c: (0, (2 * o) * ncb + c)),
                  pl.BlockSpec((1, cb), lambda o, c: (0, (2 * o + 1) * ncb + c)),
                  pl.BlockSpec(fa.shape, lambda o, c: (0, 0), pipeline_mode=once),
                  pl.BlockSpec(mb.shape, lambda o, c: (0, 0, 0), pipeline_mode=once)],
        out_specs=pl.BlockSpec((None, units, 2 * DFT_N2, cb), lambda o, c: (o, 0, 0, c)),
        out_shape=jax.ShapeDtypeStruct((HY_ORDER, units, 2 * DFT_N2, hy_w), BF16),
        scratch_shapes=[pltpu.VMEM((2 * units, DFT_GROUPS, V7X_SUBLANES, 2 * cb), F32)],
        compiler_params=_params("arbitrary", "arbitrary"),
        name="spec",
    )(h5, h5, asum, asum, fa, mb)


def _conv_kernel(u_ref, x_ref, bias_ref, fa_ref, fai_ref, mb_ref, g_ref, o_ref, s_ref, *, kb):
    lanes = s_ref.shape[-1]
    half = DFT_N2
    nh = fa_ref.shape[1] // V7X_SUBLANES

    _stage_a(fa_ref, lambda g: u_ref[:, g, :, :], s_ref)

    def unit(k1, carry):
        a = s_ref[pl.ds(2 * k1, 2), :, :, :].reshape(2 * half, lanes).astype(BF16)
        z = jnp.dot(mb_ref[k1], a, preferred_element_type=F32)
        zr, zi = z[0:half], z[half:]
        gr = g_ref[k1, 0:half, :].astype(F32)
        gi = g_ref[k1, half:, :].astype(F32)
        p = jnp.concatenate([zr * gr - zi * gi, zr * gi + zi * gr], axis=0).astype(BF16)
        y = lax.dot_general(mb_ref[k1], p, (((0,), (0,)), ((), ())), preferred_element_type=F32)
        s_ref[pl.ds(2 * k1, 2), :, :, :] = y.reshape(2, DFT_GROUPS, V7X_SUBLANES, lanes)
        return carry
    lax.fori_loop(0, nh + 1, unit, 0, unroll=kb)

    n = 2 * nh * DFT_N2
    sign = _nyquist_sign(nh) * (1.0 / n)

    def body(g, carry):
        rhs = s_ref[0:2 * nh, g, :, :].reshape(2 * nh * V7X_SUBLANES, lanes)
        conv = jnp.dot(fai_ref[...], rhs.astype(BF16), preferred_element_type=F32)
        conv = conv.reshape(nh, V7X_SUBLANES, lanes) + sign * s_ref[2 * nh, g, :, :]
        u = u_ref[:, g, :, :]
        o_ref[:, g, :, :] = x_ref[:, g, :, :] * (conv + bias_ref[...] * u)
        return carry
    lax.fori_loop(0, DFT_GROUPS, body, 0, unroll=True)


def _longconv_gate(u_src, u_col, x_src, x_col, bias, spectra, order, consts, hy_w, cb):
    fa, fa_inv, mb = consts
    b, l = u_src.shape[:2]
    units = mb.shape[0]
    nh = units - 1
    kb = _units_per_step(units)
    ncb = hy_w // cb
    view = lambda a: a.reshape(b, nh, DFT_GROUPS, V7X_SUBLANES, a.shape[2])
    blk = (None, nh, DFT_GROUPS, V7X_SUBLANES, cb)
    once = pl.Buffered(1)
    out = pl.pallas_call(
        functools.partial(_conv_kernel, kb=kb),
        grid=(ncb, b),
        in_specs=[pl.BlockSpec(blk, lambda c, i: (i, 0, 0, 0, u_col // cb + c)),
                  pl.BlockSpec(blk, lambda c, i: (i, 0, 0, 0, x_col // cb + c)),
                  pl.BlockSpec((1, cb), lambda c, i: (0, c)),
                  pl.BlockSpec(fa.shape, lambda c, i: (0, 0), pipeline_mode=once),
                  pl.BlockSpec(fa_inv.shape, lambda c, i: (0, 0), pipeline_mode=once),
                  pl.BlockSpec(mb.shape, lambda c, i: (0, 0, 0), pipeline_mode=once),
                  pl.BlockSpec((None, units, 2 * DFT_N2, cb), lambda c, i: (order, 0, 0, c),
                               pipeline_mode=once)],
        out_specs=pl.BlockSpec(blk, lambda c, i: (i, 0, 0, 0, c)),
        out_shape=jax.ShapeDtypeStruct((b, nh, DFT_GROUPS, V7X_SUBLANES, hy_w), F32),
        scratch_shapes=[pltpu.VMEM((2 * units, DFT_GROUPS, V7X_SUBLANES, cb), F32)],
        compiler_params=_params("arbitrary", "arbitrary"),
        name="conv",
    )(view(u_src), view(x_src), bias, fa, fa_inv, mb, spectra)
    return out.reshape(b, l, hy_w)


def _mlstm_dir(q, k, v_aug, ig_col, b_col, ig_row, b_row, b_end, keep, ct_ref, m_ref, idx):
    m = m_ref[idx]
    ct = ct_ref[idx]
    end_row = b_end - b_row + ig_row
    end_col = b_end - b_col + ig_col
    m_new = jnp.maximum(b_end + m, jnp.max(end_row, axis=1, keepdims=True))
    h = None
    if q is not None:
        dlog = jnp.where(keep, b_col - b_row + ig_row, -jnp.inf)
        inter = b_col + m
        m_j = jnp.maximum(inter, jnp.max(dlog, axis=1, keepdims=True))
        s = lax.dot_general(q, k, (((1,), (1,)), ((), ())), preferred_element_type=F32)
        p = (s * jnp.exp(dlog - m_j)).astype(BF16)
        nd = (jnp.dot(p, v_aug, preferred_element_type=F32)
              + jnp.exp(inter - m_j) * jnp.dot(q, ct.astype(BF16), preferred_element_type=F32))
        dv = nd.shape[1] - V7X_LANES
        den = nd[:, dv:dv + 1]
        h = nd[:, :dv] / jnp.maximum(jnp.abs(den), jnp.exp(-m_j))
    w_state = jnp.exp(end_col - m_new)
    decay = jnp.exp(b_end + m - m_new)
    vw = (v_aug.astype(F32) * w_state).astype(BF16)
    ct_ref[idx] = decay * ct + lax.dot_general(k, vw, (((0,), (0,)), ((), ())),
                                               preferred_element_type=F32)
    m_ref[idx] = m_new
    return h


def _mlstm_kernel(qf_ref, kf_ref, vf_ref, gf_ref, gtf_ref,
                  qb_ref, kb_ref, vb_ref, gb_ref, gtb_ref,
                  vc_ref, kc_ref, gc_ref, gtc_ref, bias_col_ref, bias_row_ref,
                  hf_ref, hb_ref, ct_ref, m_ref, *, heads):
    i = pl.program_id(1)
    hi = lax.Precision.HIGHEST
    lc = kc_ref.shape[0]
    dqk = kc_ref.shape[1] // heads
    dv = vc_ref.shape[1] // heads
    ng = 4 * heads
    row = lax.broadcasted_iota(jnp.int32, (lc, lc), 0)
    col = lax.broadcasted_iota(jnp.int32, (lc, lc), 1)
    lower = row >= col
    upper = row <= col
    tri_lo = lower.astype(F32)
    tri_up = upper.astype(F32)
    ones = jnp.ones((lc, V7X_LANES), BF16)

    def run(direction, q_ref, k_ref, v_ref, g_ref, gt_ref, out_ref):
        gates = g_ref[...] + bias_col_ref[...]
        gates_t = gt_ref[...] + bias_row_ref[...]
        tri_col, tri_row, keep = (tri_lo, tri_up, lower) if direction == 0 else (tri_up, tri_lo, upper)
        lf_t = jax.nn.log_sigmoid(gates_t)
        b_cols = jnp.dot(tri_col, jax.nn.log_sigmoid(gates), precision=hi, preferred_element_type=F32)
        b_rows = jnp.dot(lf_t, tri_row, precision=hi, preferred_element_type=F32)
        b_ends = jnp.sum(lf_t, axis=1, keepdims=True)
        for h in range(heads):
            ci = direction * 2 * heads + h
            cf = ci + heads
            k = k_ref[:, h * dqk:(h + 1) * dqk].astype(BF16)
            v_aug = jnp.concatenate([v_ref[:, h * dv:(h + 1) * dv].astype(BF16), ones], axis=1)
            q = None if q_ref is None else q_ref[:, h * dqk:(h + 1) * dqk].astype(BF16)
            out = _mlstm_dir(q, k, v_aug, gates[:, ci:ci + 1], b_cols[:, cf:cf + 1],
                             gates_t[ci:ci + 1, :], b_rows[cf:cf + 1, :], b_ends[cf:cf + 1, :],
                             keep, ct_ref, m_ref, direction * heads + h)
            if out_ref is not None:
                out_ref[:, h * dv:(h + 1) * dv] = out

    @pl.when(i == 0)
    def _():
        ct_ref[...] = jnp.zeros_like(ct_ref)
        m_ref[...] = jnp.zeros_like(m_ref)
        run(0, None, kc_ref, vc_ref, gc_ref, gtc_ref, None)
        run(1, None, kc_ref, vc_ref, gc_ref, gtc_ref, None)

    @pl.when(i > 0)
    def _():
        run(0, qf_ref, kf_ref, vf_ref, gf_ref, gtf_ref, hf_ref)
        run(1, qb_ref, kb_ref, vb_ref, gb_ref, gtb_ref, hb_ref)


def _mlstm(p_lat, g_lat, gt_lat, p_ctx, g_ctx, gt_ctx, gate_b, *, q_col, k_col, v_col,
           kc_col, vc_col, heads, dqk, dv):
    b, l = p_lat.shape[:2]
    lc = p_ctx.shape[1]
    nc = l // lc
    qk_w, v_w = heads * dqk, heads * dv
    ng = 4 * heads
    fwd = lambda i: jnp.maximum(i - 1, 0)
    bwd = lambda i: nc - jnp.maximum(i, 1)

    def lat_specs(chunk):
        return [pl.BlockSpec((None, lc, qk_w), lambda bi, i: (bi, chunk(i), q_col // qk_w)),
                pl.BlockSpec((None, lc, qk_w), lambda bi, i: (bi, chunk(i), k_col // qk_w)),
                pl.BlockSpec((None, lc, v_w), lambda bi, i: (bi, chunk(i), v_col // v_w)),
                pl.BlockSpec((None, lc, ng), lambda bi, i: (bi, chunk(i), 0)),
                pl.BlockSpec((None, ng, lc), lambda bi, i: (bi, 0, chunk(i)))]

    ctx_specs = [pl.BlockSpec((None, lc, v_w), lambda bi, i: (bi, 0, vc_col // v_w)),
                 pl.BlockSpec((None, lc, qk_w), lambda bi, i: (bi, 0, kc_col // qk_w)),
                 pl.BlockSpec((None, lc, ng), lambda bi, i: (bi, 0, 0)),
                 pl.BlockSpec((None, ng, lc), lambda bi, i: (bi, 0, 0))]
    bias_specs = [pl.BlockSpec((1, ng), lambda bi, i: (0, 0)),
                  pl.BlockSpec((ng, 1), lambda bi, i: (0, 0))]
    out_sds = jax.ShapeDtypeStruct((b, l, v_w), F32)
    return pl.pallas_call(
        functools.partial(_mlstm_kernel, heads=heads),
        grid=(b, nc + 1),
        in_specs=lat_specs(fwd) + lat_specs(bwd) + ctx_specs + bias_specs,
        out_specs=[pl.BlockSpec((None, lc, v_w), lambda bi, i: (bi, fwd(i), 0)),
                   pl.BlockSpec((None, lc, v_w), lambda bi, i: (bi, bwd(i), 0))],
        out_shape=[out_sds, out_sds],
        scratch_shapes=[pltpu.VMEM((2 * heads, dqk, dv + V7X_LANES), F32),
                        pltpu.VMEM((2 * heads, 1, 1), F32)],
        compiler_params=_params("arbitrary", "arbitrary"),
        name="mlstm",
    )(p_lat, p_lat, p_lat, g_lat, gt_lat, p_lat, p_lat, p_lat, g_lat, gt_lat,
      p_ctx, p_ctx, g_ctx, gt_ctx, gate_b.reshape(1, ng), gate_b.reshape(ng, 1))


def _merge_kernel(hy_ref, hf_ref, hb_ref, o_ref, w_ref, out_ref, *, hy_blocks):
    j = pl.program_id(1)

    @pl.when(j < hy_blocks)
    def _():
        out_ref[...] = hy_ref[...].astype(out_ref.dtype)

    @pl.when(j >= hy_blocks)
    def _():
        h = _ln(hf_ref[...] + hb_ref[...])
        out_ref[...] = (h * w_ref[...] * jax.nn.sigmoid(o_ref[...])).astype(out_ref.dtype)


def _merge(hy, hf, hb, p_lat, o_col, norm_w, heads, tm):
    n, w = hf.shape
    dv = w // heads
    hy_blocks = hy.shape[1] // dv
    head = lambda j: jnp.maximum(j - hy_blocks, 0)
    return pl.pallas_call(
        functools.partial(_merge_kernel, hy_blocks=hy_blocks),
        grid=(n // tm, hy_blocks + heads),
        in_specs=[pl.BlockSpec((tm, dv), lambda i, j: (i, jnp.minimum(j, hy_blocks - 1))),
                  pl.BlockSpec((tm, dv), lambda i, j: (i, head(j))),
                  pl.BlockSpec((tm, dv), lambda i, j: (i, head(j))),
                  pl.BlockSpec((tm, dv), lambda i, j: (i, o_col // dv + head(j))),
                  pl.BlockSpec((1, dv), lambda i, j: (0, head(j)))],
        out_specs=pl.BlockSpec((tm, dv), lambda i, j: (i, j)),
        out_shape=jax.ShapeDtypeStruct((n, hy.shape[1] + w), BF16),
        compiler_params=_params("arbitrary", "arbitrary"),
        name="merge",
    )(hy, hf, hb, p_lat, norm_w)


def _outp_kernel(a_ref, w_ref, x_ref, gt_ref, o_ref, *, alpha):
    y = jnp.dot(a_ref[...], w_ref[...], preferred_element_type=F32)
    o_ref[...] = alpha * x_ref[...] + gt_ref[...] * y


def _outp(a, w, x, gate, *, rows_per_mod, alpha, tm, tn):
    n, d = x.shape
    k = a.shape[1]
    per = rows_per_mod // tm
    return pl.pallas_call(
        functools.partial(_outp_kernel, alpha=alpha),
        grid=(d // tn, n // tm),
        in_specs=[pl.BlockSpec((tm, k), lambda j, i: (i, 0)),
                  pl.BlockSpec((k, tn), lambda j, i: (0, j)),
                  pl.BlockSpec((tm, tn), lambda j, i: (i, j)),
                  pl.BlockSpec((None, 1, tn), lambda j, i: (i // per, 0, j))],
        out_specs=pl.BlockSpec((tm, tn), lambda j, i: (i, j)),
        out_shape=jax.ShapeDtypeStruct((n, d), F32),
        compiler_params=_params("arbitrary", "arbitrary"),
        name="outp",
    )(a, w, x, gate)


def kernel(x, c, ctx, c_ctx, ada_w, ada_b, ln_g, ln_b, ffn1_wi, ffn1_wo, ffn2_wi, ffn2_wo,
           w_in, hy_conv_w, hy_conv_b, hy_filt_w1, hy_filt_b1, hy_filt_f1, hy_filt_w2,
           hy_filt_b2, hy_filt_f2, hy_filt_w3, hy_bias, ml_conv_w, ml_conv_b, ml_gate_b,
           ml_norm_w, w_out):
    depth = ada_w.shape[0]
    assert depth == 1, "only the depth-1 block is implemented"
    b, l, d = x.shape
    lc = ctx.shape[1]
    heads = ML_HEADS
    hy_w = hy_bias.shape[2]
    ml_w = ml_norm_w.shape[1]
    dv = ml_w // heads
    ml_qk = ml_conv_w.shape[2] // 2
    dqk = ml_qk // heads
    p_hy = 3 * hy_w
    p_state0 = p_hy + ml_qk + ml_w
    p_main = p_state0 + ml_qk + ml_w
    n_gates = 4 * heads
    alpha = (2.0 * depth) ** 0.25
    assert l % GRID_W == 0 and l % lc == 0 and lc % GRID_W == 0

    tm = min(512, l)
    tm_ctx = min(512, b * lc)
    tm_ffn = min(1024, l)
    tm_ffn_ctx = min(1024, b * lc)

    cvec = jnp.concatenate([c, c_ctx[None], jnp.zeros((V7X_SUBLANES - (b + 1) % V7X_SUBLANES, d), F32)])
    mods = _ada(cvec, ada_w.reshape(d, 9 * d), ada_b.reshape(1, 9 * d), tn=min(1024, d))
    mods = mods.reshape(cvec.shape[0], 9, 1, d)
    m_lat = [mods[:b, k] for k in range(9)]
    m_ctx = [mods[b:b + 1, k] for k in range(9)]

    lg = [ln_g[0, k][None] for k in range(3)]
    lb = [ln_b[0, k][None] for k in range(3)]
    bf = lambda a: a.astype(BF16)
    x2 = x.reshape(b * l, d)
    ctx2 = ctx.reshape(b * lc, d)

    wi1, wo1 = ffn1_wi[0], ffn1_wo[0]
    tf = 256
    x1 = _ffn(x2, lg[0], lb[0], m_lat[0], m_lat[1], m_lat[2], wi1, wo1, lg[0], lb[0],
              rows_per_mod=l, pre="ln", alpha=alpha, tm=tm_ffn, tf=tf)
    c1 = _ffn(ctx2, lg[0], lb[0], m_ctx[0], m_ctx[1], m_ctx[2], wi1, wo1, lg[0], lb[0],
              rows_per_mod=b * lc, pre="ln", alpha=alpha, tm=tm_ffn_ctx, tf=tf)

    w_in_t = jnp.swapaxes(w_in, 1, 2)
    q_scale = dqk ** -0.5
    ident = jnp.array([0.0, 1.0, 0.0], F32)[:, None]
    taps = jnp.concatenate([
        hy_conv_w[0], ml_conv_w[0][:, :ml_qk] * q_scale, jnp.tile(ident, (1, ml_w)),
        ml_conv_w[0][:, ml_qk:], jnp.tile(ident, (1, ml_w))], axis=1)
    taps = jnp.pad(taps, ((0, V7X_SUBLANES - 3), (0, 0)))
    tap_b = jnp.concatenate([
        hy_conv_b[0], ml_conv_b[0][:ml_qk] * q_scale, jnp.zeros((ml_w,), F32),
        ml_conv_b[0][ml_qk:], jnp.zeros((ml_w,), F32)])[None]
    tn = min(512, ml_qk)
    p_lat, g_lat = _proj(x1, m_lat[3], m_lat[4], w_in_t, lambda j: j, p_main, n_gates, taps, tap_b,
                         rows_per_mod=l, period=GRID_W, tm=tm_ffn, tn=tn)
    v0 = p_state0 + ml_qk
    ctx_cols = lambda a: jnp.concatenate([a[:, v0:p_main], a[:, p_state0:v0]], axis=1)
    nv = ml_w // tn
    ctx_block = lambda j: jnp.where(j < nv, j + v0 // tn, j - nv + p_state0 // tn)
    p_ctx, g_ctx = _proj(c1, m_ctx[3], m_ctx[4], w_in_t, ctx_block, p_main, n_gates,
                         ctx_cols(taps), ctx_cols(tap_b),
                         rows_per_mod=b * lc, period=lc, tm=tm_ctx, tn=tn)
    p_lat3 = p_lat.reshape(b, l, p_main)
    p_ctx3 = p_ctx.reshape(b, lc, p_main - p_state0)

    t = jnp.linspace(0.0, 1.0, l, dtype=F32)[:, None]
    w = (2.0 * math.pi / l) * jnp.arange(l, dtype=F32)[:, None]
    bands = jnp.linspace(1e-4, HY_BANDS - 1, HY_BANDS, dtype=F32)[None, :]
    z = jnp.concatenate([t, jnp.cos(bands * w), -jnp.sin(bands * w)], axis=-1)
    ze = V7X_LANES
    z = jnp.pad(z, ((0, 0), (0, ze - z.shape[1])))
    w1 = jnp.pad(hy_filt_w1[0], ((0, ze - hy_filt_w1.shape[1]), (0, 0)))
    max_decay = math.log(HY_DECAY_TARGET) / HY_FAST_PCT
    min_decay = math.log(HY_DECAY_TARGET) / HY_SLOW_PCT
    deltas = jnp.abs(jnp.linspace(min_decay, max_decay, hy_w, dtype=F32))
    deltas = jnp.tile(deltas, 2 * HY_ORDER)[None]
    hraw, asum = _filters(z, w1, hy_filt_b1[0][None], hy_filt_f1[0][None], hy_filt_w2[0],
                          hy_filt_b2[0][None], hy_filt_f2[0][None], hy_filt_w3[0], deltas,
                          tl=min(256, l))
    consts = _dft_constants(l)
    spectra = _spectra(hraw, asum, consts, hy_w, cb=V7X_LANES)
    cb = min(V7X_MXU_DIM, hy_w)
    z1 = _longconv_gate(p_lat3, 0, p_lat3, hy_w, hy_bias[0, 0][None], spectra, 0, consts, hy_w, cb)
    hy = _longconv_gate(z1, 0, p_lat3, 2 * hy_w, hy_bias[0, 1][None], spectra, 1, consts, hy_w, cb)

    g_lat, g_ctx = g_lat.reshape(b, l, n_gates), g_ctx.reshape(b, lc, n_gates)
    gt_lat, gt_ctx = jnp.swapaxes(g_lat, 1, 2), jnp.swapaxes(g_ctx, 1, 2)
    hf, hb = _mlstm(p_lat3, g_lat, gt_lat, p_ctx3, g_ctx, gt_ctx, ml_gate_b[0],
                    q_col=p_hy, k_col=p_state0, v_col=p_state0 + ml_qk,
                    kc_col=ml_w, vc_col=0, heads=heads, dqk=dqk, dv=dv)
    mixed = _merge(hy.reshape(b * l, hy_w), hf.reshape(b * l, ml_w), hb.reshape(b * l, ml_w),
                   p_lat, p_hy + ml_qk, ml_norm_w[0][None], heads, min(2048, l))

    y2 = _outp(mixed, bf(w_out[0]), x1, m_lat[5],
               rows_per_mod=l, alpha=alpha, tm=tm_ffn, tn=min(1024, d))

    out = _ffn(y2, lg[1], lb[1], m_lat[6], m_lat[7], m_lat[8], ffn2_wi[0], ffn2_wo[0],
               lg[2], lb[2], rows_per_mod=l, pre="ln_affine", alpha=alpha, tm=tm_ffn, tf=tf)
    return out.reshape(b, l, d)
```

```python
import functools
import math

import numpy as np
import jax
import jax.numpy as jnp
from jax import lax
from jax.experimental import pallas as pl
from jax.experimental.pallas import tpu as pltpu

GRID_W = 64
ML_HEADS = 4
HY_ORDER = 2
HY_BANDS = 16
HY_DECAY_TARGET = 1e-2
HY_FAST_PCT = 0.3
HY_SLOW_PCT = 1.5
LN_EPS = 1e-5

V7X_VMEM_BYTES = 64 * 1024 * 1024
V7X_LANES = 128
V7X_SUBLANES = 8
V7X_MXU_DIM = 256
VMEM_LIMIT_BYTES = V7X_VMEM_BYTES - 4 * 1024 * 1024

DFT_N2 = 128
DFT_GROUPS = DFT_N2 // V7X_SUBLANES
ROW_CHUNK = 128
ROW_DMA_CHUNKS = 4

F32 = jnp.float32
BF16 = jnp.bfloat16


def _params(*semantics):
    return pltpu.CompilerParams(dimension_semantics=semantics, vmem_limit_bytes=VMEM_LIMIT_BYTES)


def _ln(x):
    mu = jnp.mean(x, axis=-1, keepdims=True)
    xc = x - mu
    var = jnp.mean(xc * xc, axis=-1, keepdims=True)
    return xc * lax.rsqrt(var + LN_EPS)


def _silu(x):
    return x * jax.nn.sigmoid(x)


def _ada_kernel(c_ref, w_ref, b_ref, o_ref):
    s = _silu(c_ref[...]).astype(BF16)
    o_ref[...] = jnp.dot(s, w_ref[...].astype(BF16), preferred_element_type=F32) + b_ref[...]


def _ada(cvec, w, b, tn):
    rows, d = cvec.shape
    n = w.shape[1]
    return pl.pallas_call(
        _ada_kernel,
        grid=(n // tn,),
        in_specs=[pl.BlockSpec((rows, d), lambda j: (0, 0)),
                  pl.BlockSpec((d, tn), lambda j: (0, j)),
                  pl.BlockSpec((1, tn), lambda j: (0, j))],
        out_specs=pl.BlockSpec((rows, tn), lambda j: (0, j)),
        out_shape=jax.ShapeDtypeStruct((rows, n), F32),
        compiler_params=_params("arbitrary"),
        name="ada",
    )(cvec, w, b)


def _for_chunk_rows(c, rows, body):
    size = min(ROW_CHUNK, rows)
    assert rows % size == 0

    for r in range(rows // size):
        body(pl.ds(c * rows + r * size, size))


def _ffn_kernel(x_hbm, pg_ref, pb_ref, sh_ref, sc_ref, gt_ref, wg_ref, wu_ref, wo_ref,
                lg_ref, lb_ref, out_hbm, acc_ref, h_ref, in_sem, out_sem, *, pre, alpha):
    i = pl.program_id(0)
    f = pl.program_id(1)
    chunks = ROW_DMA_CHUNKS
    tm = acc_ref.shape[0]
    rows = tm // chunks

    def x_copy(c):
        return pltpu.make_async_copy(x_hbm.at[pl.ds(i * tm + c * rows, rows), :],
                                     acc_ref.at[pl.ds(c * rows, rows), :], in_sem.at[c])

    def out_copy(tile, c):
        return pltpu.make_async_copy(acc_ref.at[pl.ds(c * rows, rows), :],
                                     out_hbm.at[pl.ds(tile * tm + c * rows, rows), :], out_sem.at[c])

    def request(c):
        @pl.when(i > 0)
        def _():
            out_copy(i - 1, c).wait()
        x_copy(c).start()

    def prologue(c):
        x_copy(c).wait()

        def body(rs):
            x = acc_ref[rs, :]
            if pre != "none":
                x = _ln(x)
            if pre == "ln_affine":
                x = x * pg_ref[...] + pb_ref[...]
            h_ref[rs, :] = (x * (1.0 + sc_ref[...]) + sh_ref[...]).astype(BF16)
            acc_ref[rs, :] = alpha * x
        _for_chunk_rows(c, rows, body)

    def ffn_step(rs):
        h = h_ref[rs, :]
        g = jnp.dot(h, wg_ref[...].astype(BF16), preferred_element_type=F32)
        u = jnp.dot(h, wu_ref[...].astype(BF16), preferred_element_type=F32)
        a = (_silu(g) * u).astype(BF16)
        acc_ref[rs, :] += (0.5 * gt_ref[...]) * jnp.dot(a, wo_ref[...].astype(BF16),
                                                        preferred_element_type=F32)

    chunk_rows = lambda c: slice(c * rows, (c + 1) * rows)
    last = pl.num_programs(1) - 1

    @pl.when(f == 0)
    def _():
        for c in range(chunks - 1):
            request(c)
        prologue(0)
        request(chunks - 1)
        for c in range(1, chunks):
            prologue(c)
            ffn_step(chunk_rows(c - 1))
        ffn_step(chunk_rows(chunks - 1))

    @pl.when(jnp.logical_and(f > 0, f < last))
    def _():
        ffn_step(slice(None))

    @pl.when(f == last)
    def _():
        def body(rs):
            acc_ref[rs, :] = _ln(acc_ref[rs, :]) * lg_ref[...] + lb_ref[...]
        for c in range(chunks):
            ffn_step(chunk_rows(c))
            _for_chunk_rows(c, rows, body)
            out_copy(i, c).start()

        @pl.when(i == pl.num_programs(0) - 1)
        def _():
            for c in range(chunks):
                out_copy(i, c).wait()


def _ffn(x, pre_g, pre_b, shift, scale, gate, wi, wo, ln_g, ln_b, *, rows_per_mod, pre, alpha,
         tm, tf):
    n, d = x.shape
    ff = wo.shape[0]
    nf = ff // tf
    assert nf >= 2, "the first and the last F step are distinct code paths"
    per = rows_per_mod // tm
    mod_spec = pl.BlockSpec((None, 1, d), lambda i, f: (i // per, 0, 0))
    vec_spec = pl.BlockSpec((1, d), lambda i, f: (0, 0))
    return pl.pallas_call(
        functools.partial(_ffn_kernel, pre=pre, alpha=alpha),
        grid=(n // tm, nf),
        in_specs=[pl.BlockSpec(memory_space=pl.ANY),
                  vec_spec, vec_spec,
                  mod_spec, mod_spec, mod_spec,
                  pl.BlockSpec((d, tf), lambda i, f: (0, f)),
                  pl.BlockSpec((d, tf), lambda i, f: (0, nf + f)),
                  pl.BlockSpec((tf, d), lambda i, f: (f, 0)),
                  vec_spec, vec_spec],
        out_specs=pl.BlockSpec(memory_space=pl.ANY),
        out_shape=jax.ShapeDtypeStruct((n, d), F32),
        scratch_shapes=[pltpu.VMEM((tm, d), F32),
                        pltpu.VMEM((tm, d), BF16),
                        pltpu.SemaphoreType.DMA((ROW_DMA_CHUNKS,)),
                        pltpu.SemaphoreType.DMA((ROW_DMA_CHUNKS,))],
        compiler_params=_params("arbitrary", "arbitrary"),
        name="ffn",
    )(x, pre_g, pre_b, shift, scale, gate, wi, wi, wo, ln_g, ln_b)


def _dot_nt(a, b_t):
    return lax.dot_general(a, b_t, (((1,), (1,)), ((), ())), preferred_element_type=F32)


def _proj_kernel(x_hbm, sh_ref, sc_ref, w_ref, wgate_ref, cw_ref, cb_ref, o_ref, og_ref,
                 x_ref, h_ref, sem, *, period):
    i = pl.program_id(0)
    tm = x_ref.shape[0]
    rows = tm // ROW_DMA_CHUNKS

    def x_copy(c):
        return pltpu.make_async_copy(x_hbm.at[pl.ds(i * tm + c * rows, rows), :],
                                     x_ref.at[pl.ds(c * rows, rows), :], sem.at[c])

    @pl.when(pl.program_id(1) == 0)
    def _():
        for c in range(ROW_DMA_CHUNKS):
            x_copy(c).start()

        def body(rs):
            h_ref[rs, :] = (x_ref[rs, :] * (1.0 + sc_ref[...]) + sh_ref[...]).astype(BF16)
        for c in range(ROW_DMA_CHUNKS):
            x_copy(c).wait()
            _for_chunk_rows(c, rows, body)
        og_ref[...] = _dot_nt(h_ref[...], wgate_ref[...].astype(BF16))

    r = _dot_nt(h_ref[...], w_ref[...].astype(BF16))
    pos = lax.broadcasted_iota(jnp.int32, (tm, 1), 0) % period
    prev = jnp.where(pos == 0, 0.0, pltpu.roll(r, 1, 0))
    nxt = jnp.where(pos == period - 1, 0.0, pltpu.roll(r, tm - 1, 0))
    cw = cw_ref[...]
    o_ref[...] = prev * cw[0:1] + r * cw[1:2] + nxt * cw[2:3] + cb_ref[...]


def _proj(x, shift, scale, w_t, w_block, gate_row0, n_gates, conv_w, conv_b, *, rows_per_mod,
          period, tm, tn):
    n, d = x.shape
    p = conv_w.shape[1]
    per = rows_per_mod // tm
    mod_spec = pl.BlockSpec((None, 1, d), lambda i, j: (i // per, 0, 0))
    return pl.pallas_call(
        functools.partial(_proj_kernel, period=period),
        grid=(n // tm, p // tn),
        in_specs=[pl.BlockSpec(memory_space=pl.ANY),
                  mod_spec, mod_spec,
                  pl.BlockSpec((None, tn, d), lambda i, j: (0, w_block(j), 0)),
                  pl.BlockSpec((None, n_gates, d), lambda i, j: (0, gate_row0 // n_gates, 0)),
                  pl.BlockSpec((V7X_SUBLANES, tn), lambda i, j: (0, j)),
                  pl.BlockSpec((1, tn), lambda i, j: (0, j))],
        out_specs=[pl.BlockSpec((tm, tn), lambda i, j: (i, j)),
                   pl.BlockSpec((tm, n_gates), lambda i, j: (i, 0))],
        out_shape=[jax.ShapeDtypeStruct((n, p), F32),
                   jax.ShapeDtypeStruct((n, n_gates), F32)],
        scratch_shapes=[pltpu.VMEM((tm, d), F32),
                        pltpu.VMEM((tm, d), BF16),
                        pltpu.SemaphoreType.DMA((ROW_DMA_CHUNKS,))],
        compiler_params=_params("arbitrary", "arbitrary"),
        name="proj",
    )(x, shift, scale, w_t, w_t, conv_w, conv_b)


def _filt_kernel(z_ref, w1_ref, b1_ref, f1_ref, w2_ref, b2_ref, f2_ref, w3_ref, dl_ref,
                 h_ref, asum_ref):
    hi = lax.Precision.HIGHEST
    z = z_ref[...]
    a = jnp.sin(f1_ref[...] * (jnp.dot(z, w1_ref[...], precision=hi, preferred_element_type=F32)
                               + b1_ref[...]))
    a = jnp.sin(f2_ref[...] * (jnp.dot(a, w2_ref[...], precision=hi, preferred_element_type=F32)
                               + b2_ref[...]))
    split = lambda v: (v.astype(BF16), (v - v.astype(BF16).astype(F32)).astype(BF16))
    (a_hi, a_lo), (w_hi, w_lo) = split(a), split(w3_ref[...])
    h = (jnp.dot(a_hi, w_hi, preferred_element_type=F32)
         + jnp.dot(a_hi, w_lo, preferred_element_type=F32)
         + jnp.dot(a_lo, w_hi, preferred_element_type=F32))
    h = h * jnp.exp(-z[:, 0:1] * dl_ref[...])

    @pl.when(pl.program_id(0) == 0)
    def _():
        asum_ref[...] = jnp.zeros_like(asum_ref)

    h_ref[...] = h
    asum_ref[...] += jnp.sum(jnp.abs(h), axis=0, keepdims=True)


def _filters(z, w1, b1, f1, w2, b2, f2, w3, deltas, tl):
    l, ze = z.shape
    fh = w2.shape[0]
    n = w3.shape[1]
    full = lambda shape: pl.BlockSpec(shape, lambda i: (0, 0))
    return pl.pallas_call(
        _filt_kernel,
        grid=(l // tl,),
        in_specs=[pl.BlockSpec((tl, ze), lambda i: (i, 0)),
                  full((ze, fh)), full((1, fh)), full((1, fh)),
                  full((fh, fh)), full((1, fh)), full((1, fh)),
                  full((fh, n)), full((1, n))],
        out_specs=[pl.BlockSpec((tl, n), lambda i: (i, 0)), full((1, n))],
        out_shape=[jax.ShapeDtypeStruct((l, n), F32), jax.ShapeDtypeStruct((1, n), F32)],
        compiler_params=_params("arbitrary"),
        name="filt",
    )(z, w1, b1, f1, w2, b2, f2, w3, deltas)


@functools.lru_cache(maxsize=None)
def _dft_constants(seq_len):
    n = 2 * seq_len
    nh = n // DFT_N2 // 2
    t1 = np.arange(nh)[None, :]
    k1 = np.arange(nh)[:, None]
    th = 2.0 * np.pi * (t1 * k1) / (2 * nh)
    fwd = np.stack([np.cos(th), -np.sin(th)], axis=1).reshape(2 * nh, nh)
    pair = np.where(k1 == 0, 1.0, 2.0) / n
    inv = np.stack([pair * np.cos(th), -pair * np.sin(th)], axis=1).reshape(2 * nh, nh).T
    eye = np.eye(V7X_SUBLANES)
    fa = np.kron(fwd, eye)
    fa_inv = np.kron(inv, eye)
    t2 = np.arange(DFT_N2)[None, None, :]
    k2 = np.arange(DFT_N2)[None, :, None]
    ph = 2.0 * np.pi * (t2 * k2 / DFT_N2 + t2 * np.arange(nh + 1)[:, None, None] / n)
    mr, mi = np.cos(ph), -np.sin(ph)
    mb = np.concatenate([np.concatenate([mr, -mi], axis=2),
                         np.concatenate([mi, mr], axis=2)], axis=1)
    as_bf16 = lambda a: jnp.asarray(a, dtype=F32).astype(BF16)
    return as_bf16(fa), as_bf16(fa_inv), as_bf16(mb)


def _units_per_step(units):
    return max(k for k in range(1, 12) if units % k == 0)


def _nyquist_sign(nh):
    t1 = lax.broadcasted_iota(jnp.int32, (nh, 1, 1), 0)
    return jnp.where(t1 % 2 == 0, 1.0, -1.0).astype(F32)


def _stage_a(fa_ref, u, s_ref):
    nh = fa_ref.shape[1] // V7X_SUBLANES
    lanes = s_ref.shape[-1]
    sign = _nyquist_sign(nh)

    def body(g, carry):
        rhs = u(g)
        out = jnp.dot(fa_ref[...], rhs.reshape(nh * V7X_SUBLANES, lanes).astype(BF16),
                      preferred_element_type=F32)
        s_ref[0:2 * nh, g, :, :] = out.reshape(2 * nh, V7X_SUBLANES, lanes)
        s_ref[2 * nh, g, :, :] = jnp.sum(rhs * sign, axis=0)
        s_ref[2 * nh + 1, g, :, :] = jnp.zeros((V7X_SUBLANES, lanes), F32)
        return carry
    lax.fori_loop(0, DFT_GROUPS, body, 0, unroll=True)


def _spec_kernel(hf_ref, hb_ref, nf_ref, nb_ref, fa_ref, mb_ref, g_ref, s_ref, *, kb):
    cb = hf_ref.shape[-1]
    half = DFT_N2

    def both(g):
        hf, hb = hf_ref[:, g, :, :], hb_ref[:, g, :, :]
        return jnp.concatenate([hf + hb, hf - hb], axis=-1)
    _stage_a(fa_ref, both, s_ref)

    inv_norm = 1.0 / (nf_ref[...] + nb_ref[...])

    def unit(k1, carry):
        a = s_ref[pl.ds(2 * k1, 2), :, :, :].reshape(2 * half, 2 * cb).astype(BF16)
        z = jnp.dot(mb_ref[k1], a, preferred_element_type=F32)
        g_ref[k1, 0:half, :] = (z[0:half, 0:cb] * inv_norm).astype(g_ref.dtype)
        g_ref[k1, half:, :] = (z[half:, cb:] * inv_norm).astype(g_ref.dtype)
        return carry
    lax.fori_loop(0, mb_ref.shape[0], unit, 0, unroll=kb)


def _spectra(hraw, asum, consts, hy_w, cb):
    fa, _, mb = consts
    units = mb.shape[0]
    nh = units - 1
    kb = _units_per_step(units)
    ncb = hy_w // cb
    h5 = hraw.reshape(nh, DFT_GROUPS, V7X_SUBLANES, hraw.shape[1])
    blk = (nh, DFT_GROUPS, V7X_SUBLANES, cb)
    once = pl.Buffered(1)
    return pl.pallas_call(
        functools.partial(_spec_kernel, kb=kb),
        grid=(HY_ORDER, ncb),
        in_specs=[pl.BlockSpec(blk, lambda o, c: (0, 0, 0, (2 * o) * ncb + c)),
                  pl.BlockSpec(blk, lambda o, c: (0, 0, 0, (2 * o + 1) * ncb + c)),
                  pl.BlockSpec((1, cb), lambda o, c: (0, (2 * o) * ncb + c)),
                  pl.BlockSpec((1, cb), lambda o, c: (0, (2 * o + 1) * ncb + c)),
                  pl.BlockSpec(fa.shape, lambda o, c: (0, 0), pipeline_mode=once),
                  pl.BlockSpec(mb.shape, lambda o, c: (0, 0, 0), pipeline_mode=once)],
        out_specs=pl.BlockSpec((None, units, 2 * DFT_N2, cb), lambda o, c: (o, 0, 0, c)),
        out_shape=jax.ShapeDtypeStruct((HY_ORDER, units, 2 * DFT_N2, hy_w), BF16),
        scratch_shapes=[pltpu.VMEM((2 * units, DFT_GROUPS, V7X_SUBLANES, 2 * cb), F32)],
        compiler_params=_params("arbitrary", "arbitrary"),
        name="spec",
    )(h5, h5, asum, asum, fa, mb)


def _conv_kernel(u_ref, x_ref, bias_ref, fa_ref, fai_ref, mb_ref, g_ref, o_ref, s_ref, *, kb):
    lanes = s_ref.shape[-1]
    half = DFT_N2
    nh = fa_ref.shape[1] // V7X_SUBLANES

    _stage_a(fa_ref, lambda g: u_ref[:, g, :, :], s_ref)

    def unit(k1, carry):
        a = s_ref[pl.ds(2 * k1, 2), :, :, :].reshape(2 * half, lanes).astype(BF16)
        z = jnp.dot(mb_ref[k1], a, preferred_element_type=F32)
        zr, zi = z[0:half], z[half:]
        gr = g_ref[k1, 0:half, :].astype(F32)
        gi = g_ref[k1, half:, :].astype(F32)
        p = jnp.concatenate([zr * gr - zi * gi, zr * gi + zi * gr], axis=0).astype(BF16)
        y = lax.dot_general(mb_ref[k1], p, (((0,), (0,)), ((), ())), preferred_element_type=F32)
        s_ref[pl.ds(2 * k1, 2), :, :, :] = y.reshape(2, DFT_GROUPS, V7X_SUBLANES, lanes)
        return carry
    lax.fori_loop(0, nh + 1, unit, 0, unroll=kb)

    n = 2 * nh * DFT_N2
    sign = _nyquist_sign(nh) * (1.0 / n)

    def body(g, carry):
        rhs = s_ref[0:2 * nh, g, :, :].reshape(2 * nh * V7X_SUBLANES, lanes)
        conv = jnp.dot(fai_ref[...], rhs.astype(BF16), preferred_element_type=F32)
        conv = conv.reshape(nh, V7X_SUBLANES, lanes) + sign * s_ref[2 * nh, g, :, :]
        u = u_ref[:, g, :, :]
        o_ref[:, g, :, :] = x_ref[:, g, :, :] * (conv + bias_ref[...] * u)
        return carry
    lax.fori_loop(0, DFT_GROUPS, body, 0, unroll=True)


def _longconv_gate(u_src, u_col, x_src, x_col, bias, spectra, order, consts, hy_w, cb):
    fa, fa_inv, mb = consts
    b, l = u_src.shape[:2]
    units = mb.shape[0]
    nh = units - 1
    kb = _units_per_step(units)
    ncb = hy_w // cb
    view = lambda a: a.reshape(b, nh, DFT_GROUPS, V7X_SUBLANES, a.shape[2])
    blk = (None, nh, DFT_GROUPS, V7X_SUBLANES, cb)
    once = pl.Buffered(1)
    out = pl.pallas_call(
        functools.partial(_conv_kernel, kb=kb),
        grid=(ncb, b),
        in_specs=[pl.BlockSpec(blk, lambda c, i: (i, 0, 0, 0, u_col // cb + c)),
                  pl.BlockSpec(blk, lambda c, i: (i, 0, 0, 0, x_col // cb + c)),
                  pl.BlockSpec((1, cb), lambda c, i: (0, c)),
                  pl.BlockSpec(fa.shape, lambda c, i: (0, 0), pipeline_mode=once),
                  pl.BlockSpec(fa_inv.shape, lambda c, i: (0, 0), pipeline_mode=once),
                  pl.BlockSpec(mb.shape, lambda c, i: (0, 0, 0), pipeline_mode=once),
                  pl.BlockSpec((None, units, 2 * DFT_N2, cb), lambda c, i: (order, 0, 0, c),
                               pipeline_mode=once)],
        out_specs=pl.BlockSpec(blk, lambda c, i: (i, 0, 0, 0, c)),
        out_shape=jax.ShapeDtypeStruct((b, nh, DFT_GROUPS, V7X_SUBLANES, hy_w), F32),
        scratch_shapes=[pltpu.VMEM((2 * units, DFT_GROUPS, V7X_SUBLANES, cb), F32)],
        compiler_params=_params("arbitrary", "arbitrary"),
        name="conv",
    )(view(u_src), view(x_src), bias, fa, fa_inv, mb, spectra)
    return out.reshape(b, l, hy_w)


def _mlstm_dir(q, k, v_aug, ig_col, b_col, ig_row, b_row, b_end, keep, ct_ref, m_ref, idx):
    m = m_ref[idx]
    ct = ct_ref[idx]
    end_row = b_end - b_row + ig_row
    end_col = b_end - b_col + ig_col
    m_new = jnp.maximum(b_end + m, jnp.max(end_row, axis=1, keepdims=True))
    h = None
    if q is not None:
        dlog = jnp.where(keep, b_col - b_row + ig_row, -jnp.inf)
        inter = b_col + m
        m_j = jnp.maximum(inter, jnp.max(dlog, axis=1, keepdims=True))
        s = lax.dot_general(q, k, (((1,), (1,)), ((), ())), preferred_element_type=F32)
        p = (s * jnp.exp(dlog - m_j)).astype(BF16)
        nd = (jnp.dot(p, v_aug, preferred_element_type=F32)
              + jnp.exp(inter - m_j) * jnp.dot(q, ct.astype(BF16), preferred_element_type=F32))
        dv = nd.shape[1] - V7X_LANES
        den = nd[:, dv:dv + 1]
        h = nd[:, :dv] / jnp.maximum(jnp.abs(den), jnp.exp(-m_j))
    w_state = jnp.exp(end_col - m_new)
    decay = jnp.exp(b_end + m - m_new)
    vw = (v_aug.astype(F32) * w_state).astype(BF16)
    ct_ref[idx] = decay * ct + lax.dot_general(k, vw, (((0,), (0,)), ((), ())),
                                               preferred_element_type=F32)
    m_ref[idx] = m_new
    return h


def _mlstm_kernel(qf_ref, kf_ref, vf_ref, gf_ref, gtf_ref,
                  qb_ref, kb_ref, vb_ref, gb_ref, gtb_ref,
                  vc_ref, kc_ref, gc_ref, gtc_ref, bias_col_ref, bias_row_ref,
                  hf_ref, hb_ref, ct_ref, m_ref, *, heads):
    i = pl.program_id(1)
    hi = lax.Precision.HIGHEST
    lc = kc_ref.shape[0]
    dqk = kc_ref.shape[1] // heads
    dv = vc_ref.shape[1] // heads
    ng = 4 * heads
    row = lax.broadcasted_iota(jnp.int32, (lc, lc), 0)
    col = lax.broadcasted_iota(jnp.int32, (lc, lc), 1)
    lower = row >= col
    upper = row <= col
    tri_lo = lower.astype(F32)
    tri_up = upper.astype(F32)
    ones = jnp.ones((lc, V7X_LANES), BF16)

    def run(direction, q_ref, k_ref, v_ref, g_ref, gt_ref, out_ref):
        gates = g_ref[...] + bias_col_ref[...]
        gates_t = gt_ref[...] + bias_row_ref[...]
        tri_col, tri_row, keep = (tri_lo, tri_up, lower) if direction == 0 else (tri_up, tri_lo, upper)
        lf_t = jax.nn.log_sigmoid(gates_t)
        b_cols = jnp.dot(tri_col, jax.nn.log_sigmoid(gates), precision=hi, preferred_element_type=F32)
        b_rows = jnp.dot(lf_t, tri_row, precision=hi, preferred_element_type=F32)
        b_ends = jnp.sum(lf_t, axis=1, keepdims=True)
        for h in range(heads):
            ci = direction * 2 * heads + h
            cf = ci + heads
            k = k_ref[:, h * dqk:(h + 1) * dqk].astype(BF16)
            v_aug = jnp.concatenate([v_ref[:, h * dv:(h + 1) * dv].astype(BF16), ones], axis=1)
            q = None if q_ref is None else q_ref[:, h * dqk:(h + 1) * dqk].astype(BF16)
            out = _mlstm_dir(q, k, v_aug, gates[:, ci:ci + 1], b_cols[:, cf:cf + 1],
                             gates_t[ci:ci + 1, :], b_rows[cf:cf + 1, :], b_ends[cf:cf + 1, :],
                             keep, ct_ref, m_ref, direction * heads + h)
            if out_ref is not None:
                out_ref[:, h * dv:(h + 1) * dv] = out

    @pl.when(i == 0)
    def _():
        ct_ref[...] = jnp.zeros_like(ct_ref)
        m_ref[...] = jnp.zeros_like(m_ref)
        run(0, None, kc_ref, vc_ref, gc_ref, gtc_ref, None)
        run(1, None, kc_ref, vc_ref, gc_ref, gtc_ref, None)

    @pl.when(i > 0)
    def _():
        run(0, qf_ref, kf_ref, vf_ref, gf_ref, gtf_ref, hf_ref)
        run(1, qb_ref, kb_ref, vb_ref, gb_ref, gtb_ref, hb_ref)


def _mlstm(p_lat, g_lat, gt_lat, p_ctx, g_ctx, gt_ctx, gate_b, *, q_col, k_col, v_col,
           kc_col, vc_col, heads, dqk, dv):
    b, l = p_lat.shape[:2]
    lc = p_ctx.shape[1]
    nc = l // lc
    qk_w, v_w = heads * dqk, heads * dv
    ng = 4 * heads
    fwd = lambda i: jnp.maximum(i - 1, 0)
    bwd = lambda i: nc - jnp.maximum(i, 1)

    def lat_specs(chunk):
        return [pl.BlockSpec((None, lc, qk_w), lambda bi, i: (bi, chunk(i), q_col // qk_w)),
                pl.BlockSpec((None, lc, qk_w), lambda bi, i: (bi, chunk(i), k_col // qk_w)),
                pl.BlockSpec((None, lc, v_w), lambda bi, i: (bi, chunk(i), v_col // v_w)),
                pl.BlockSpec((None, lc, ng), lambda bi, i: (bi, chunk(i), 0)),
                pl.BlockSpec((None, ng, lc), lambda bi, i: (bi, 0, chunk(i)))]

    ctx_specs = [pl.BlockSpec((None, lc, v_w), lambda bi, i: (bi, 0, vc_col // v_w)),
                 pl.BlockSpec((None, lc, qk_w), lambda bi, i: (bi, 0, kc_col // qk_w)),
                 pl.BlockSpec((None, lc, ng), lambda bi, i: (bi, 0, 0)),
                 pl.BlockSpec((None, ng, lc), lambda bi, i: (bi, 0, 0))]
    bias_specs = [pl.BlockSpec((1, ng), lambda bi, i: (0, 0)),
                  pl.BlockSpec((ng, 1), lambda bi, i: (0, 0))]
    out_sds = jax.ShapeDtypeStruct((b, l, v_w), F32)
    return pl.pallas_call(
        functools.partial(_mlstm_kernel, heads=heads),
        grid=(b, nc + 1),
        in_specs=lat_specs(fwd) + lat_specs(bwd) + ctx_specs + bias_specs,
        out_specs=[pl.BlockSpec((None, lc, v_w), lambda bi, i: (bi, fwd(i), 0)),
                   pl.BlockSpec((None, lc, v_w), lambda bi, i: (bi, bwd(i), 0))],
        out_shape=[out_sds, out_sds],
        scratch_shapes=[pltpu.VMEM((2 * heads, dqk, dv + V7X_LANES), F32),
                        pltpu.VMEM((2 * heads, 1, 1), F32)],
        compiler_params=_params("arbitrary", "arbitrary"),
        name="mlstm",
    )(p_lat, p_lat, p_lat, g_lat, gt_lat, p_lat, p_lat, p_lat, g_lat, gt_lat,
      p_ctx, p_ctx, g_ctx, gt_ctx, gate_b.reshape(1, ng), gate_b.reshape(ng, 1))


def _merge_kernel(hy_ref, hf_ref, hb_ref, o_ref, w_ref, out_ref, *, hy_blocks):
    j = pl.program_id(1)

    @pl.when(j < hy_blocks)
    def _():
        out_ref[...] = hy_ref[...].astype(out_ref.dtype)

    @pl.when(j >= hy_blocks)
    def _():
        h = _ln(hf_ref[...] + hb_ref[...])
        out_ref[...] = (h * w_ref[...] * jax.nn.sigmoid(o_ref[...])).astype(out_ref.dtype)


def _merge(hy, hf, hb, p_lat, o_col, norm_w, heads, tm):
    n, w = hf.shape
    dv = w // heads
    hy_blocks = hy.shape[1] // dv
    head = lambda j: jnp.maximum(j - hy_blocks, 0)
    return pl.pallas_call(
        functools.partial(_merge_kernel, hy_blocks=hy_blocks),
        grid=(n // tm, hy_blocks + heads),
        in_specs=[pl.BlockSpec((tm, dv), lambda i, j: (i, jnp.minimum(j, hy_blocks - 1))),
                  pl.BlockSpec((tm, dv), lambda i, j: (i, head(j))),
                  pl.BlockSpec((tm, dv), lambda i, j: (i, head(j))),
                  pl.BlockSpec((tm, dv), lambda i, j: (i, o_col // dv + head(j))),
                  pl.BlockSpec((1, dv), lambda i, j: (0, head(j)))],
        out_specs=pl.BlockSpec((tm, dv), lambda i, j: (i, j)),
        out_shape=jax.ShapeDtypeStruct((n, hy.shape[1] + w), BF16),
        compiler_params=_params("arbitrary", "arbitrary"),
        name="merge",
    )(hy, hf, hb, p_lat, norm_w)


def _outp_kernel(a_ref, w_ref, x_ref, gt_ref, o_ref, *, alpha):
    y = jnp.dot(a_ref[...], w_ref[...], preferred_element_type=F32)
    o_ref[...] = alpha * x_ref[...] + gt_ref[...] * y


def _outp(a, w, x, gate, *, rows_per_mod, alpha, tm, tn):
    n, d = x.shape
    k = a.shape[1]
    per = rows_per_mod // tm
    return pl.pallas_call(
        functools.partial(_outp_kernel, alpha=alpha),
        grid=(d // tn, n // tm),
        in_specs=[pl.BlockSpec((tm, k), lambda j, i: (i, 0)),
                  pl.BlockSpec((k, tn), lambda j, i: (0, j)),
                  pl.BlockSpec((tm, tn), lambda j, i: (i, j)),
                  pl.BlockSpec((None, 1, tn), lambda j, i: (i // per, 0, j))],
        out_specs=pl.BlockSpec((tm, tn), lambda j, i: (i, j)),
        out_shape=jax.ShapeDtypeStruct((n, d), F32),
        compiler_params=_params("arbitrary", "arbitrary"),
        name="outp",
    )(a, w, x, gate)


def kernel(x, c, ctx, c_ctx, ada_w, ada_b, ln_g, ln_b, ffn1_wi, ffn1_wo, ffn2_wi, ffn2_wo,
           w_in, hy_conv_w, hy_conv_b, hy_filt_w1, hy_filt_b1, hy_filt_f1, hy_filt_w2,
           hy_filt_b2, hy_filt_f2, hy_filt_w3, hy_bias, ml_conv_w, ml_conv_b, ml_gate_b,
           ml_norm_w, w_out):
    depth = ada_w.shape[0]
    assert depth == 1, "only the depth-1 block is implemented"
    b, l, d = x.shape
    lc = ctx.shape[1]
    heads = ML_HEADS
    hy_w = hy_bias.shape[2]
    ml_w = ml_norm_w.shape[1]
    dv = ml_w // heads
    ml_qk = ml_conv_w.shape[2] // 2
    dqk = ml_qk // heads
    p_hy = 3 * hy_w
    p_state0 = p_hy + ml_qk + ml_w
    p_main = p_state0 + ml_qk + ml_w
    n_gates = 4 * heads
    alpha = (2.0 * depth) ** 0.25
    assert l % GRID_W == 0 and l % lc == 0 and lc % GRID_W == 0

    tm = min(512, l)
    tm_ctx = min(512, b * lc)
    tm_ffn = min(1024, l)
    tm_ffn_ctx = min(1024, b * lc)

    cvec = jnp.concatenate([c, c_ctx[None], jnp.zeros((V7X_SUBLANES - (b + 1) % V7X_SUBLANES, d), F32)])
    mods = _ada(cvec, ada_w.reshape(d, 9 * d), ada_b.reshape(1, 9 * d), tn=min(1024, d))
    mods = mods.reshape(cvec.shape[0], 9, 1, d)
    m_lat = [mods[:b, k] for k in range(9)]
    m_ctx = [mods[b:b + 1, k] for k in range(9)]

    lg = [ln_g[0, k][None] for k in range(3)]
    lb = [ln_b[0, k][None] for k in range(3)]
    bf = lambda a: a.astype(BF16)
    x2 = x.reshape(b * l, d)
    ctx2 = ctx.reshape(b * lc, d)

    wi1, wo1 = ffn1_wi[0], ffn1_wo[0]
    tf = 256
    x1 = _ffn(x2, lg[0], lb[0], m_lat[0], m_lat[1], m_lat[2], wi1, wo1, lg[0], lb[0],
              rows_per_mod=l, pre="ln", alpha=alpha, tm=tm_ffn, tf=tf)
    c1 = _ffn(ctx2, lg[0], lb[0], m_ctx[0], m_ctx[1], m_ctx[2], wi1, wo1, lg[0], lb[0],
              rows_per_mod=b * lc, pre="ln", alpha=alpha, tm=tm_ffn_ctx, tf=tf)

    w_in_t = jnp.swapaxes(w_in, 1, 2)
    q_scale = dqk ** -0.5
    ident = jnp.array([0.0, 1.0, 0.0], F32)[:, None]
    taps = jnp.concatenate([
        hy_conv_w[0], ml_conv_w[0][:, :ml_qk] * q_scale, jnp.tile(ident, (1, ml_w)),
        ml_conv_w[0][:, ml_qk:], jnp.tile(ident, (1, ml_w))], axis=1)
    taps = jnp.pad(taps, ((0, V7X_SUBLANES - 3), (0, 0)))
    tap_b = jnp.concatenate([
        hy_conv_b[0], ml_conv_b[0][:ml_qk] * q_scale, jnp.zeros((ml_w,), F32),
        ml_conv_b[0][ml_qk:], jnp.zeros((ml_w,), F32)])[None]
    tn = min(512, ml_qk)
    p_lat, g_lat = _proj(x1, m_lat[3], m_lat[4], w_in_t, lambda j: j, p_main, n_gates, taps, tap_b,
                         rows_per_mod=l, period=GRID_W, tm=tm_ffn, tn=tn)
    v0 = p_state0 + ml_qk
    ctx_cols = lambda a: jnp.concatenate([a[:, v0:p_main], a[:, p_state0:v0]], axis=1)
    nv = ml_w // tn
    ctx_block = lambda j: jnp.where(j < nv, j + v0 // tn, j - nv + p_state0 // tn)
    p_ctx, g_ctx = _proj(c1, m_ctx[3], m_ctx[4], w_in_t, ctx_block, p_main, n_gates,
                         ctx_cols(taps), ctx_cols(tap_b),
                         rows_per_mod=b * lc, period=lc, tm=tm_ctx, tn=tn)
    p_lat3 = p_lat.reshape(b, l, p_main)
    p_ctx3 = p_ctx.reshape(b, lc, p_main - p_state0)

    t = jnp.linspace(0.0, 1.0, l, dtype=F32)[:, None]
    w = (2.0 * math.pi / l) * jnp.arange(l, dtype=F32)[:, None]
    bands = jnp.linspace(1e-4, HY_BANDS - 1, HY_BANDS, dtype=F32)[None, :]
    z = jnp.concatenate([t, jnp.cos(bands * w), -jnp.sin(bands * w)], axis=-1)
    ze = V7X_LANES
    z = jnp.pad(z, ((0, 0), (0, ze - z.shape[1])))
    w1 = jnp.pad(hy_filt_w1[0], ((0, ze - hy_filt_w1.shape[1]), (0, 0)))
    max_decay = math.log(HY_DECAY_TARGET) / HY_FAST_PCT
    min_decay = math.log(HY_DECAY_TARGET) / HY_SLOW_PCT
    deltas = jnp.abs(jnp.linspace(min_decay, max_decay, hy_w, dtype=F32))
    deltas = jnp.tile(deltas, 2 * HY_ORDER)[None]
    hraw, asum = _filters(z, w1, hy_filt_b1[0][None], hy_filt_f1[0][None], hy_filt_w2[0],
                          hy_filt_b2[0][None], hy_filt_f2[0][None], hy_filt_w3[0], deltas,
                          tl=min(256, l))
    consts = _dft_constants(l)
    spectra = _spectra(hraw, asum, consts, hy_w, cb=V7X_LANES)
    cb = min(V7X_MXU_DIM, hy_w)
    z1 = _longconv_gate(p_lat3, 0, p_lat3, hy_w, hy_bias[0, 0][None], spectra, 0, consts, hy_w, cb)
    hy = _longconv_gate(z1, 0, p_lat3, 2 * hy_w, hy_bias[0, 1][None], spectra, 1, consts, hy_w, cb)

    g_lat, g_ctx = g_lat.reshape(b, l, n_gates), g_ctx.reshape(b, lc, n_gates)
    gt_lat, gt_ctx = jnp.swapaxes(g_lat, 1, 2), jnp.swapaxes(g_ctx, 1, 2)
    hf, hb = _mlstm(p_lat3, g_lat, gt_lat, p_ctx3, g_ctx, gt_ctx, ml_gate_b[0],
                    q_col=p_hy, k_col=p_state0, v_col=p_state0 + ml_qk,
                    kc_col=ml_w, vc_col=0, heads=heads, dqk=dqk, dv=dv)
    mixed = _merge(hy.reshape(b * l, hy_w), hf.reshape(b * l, ml_w), hb.reshape(b * l, ml_w),
                   p_lat, p_hy + ml_qk, ml_norm_w[0][None], heads, min(2048, l))

    y2 = _outp(mixed, bf(w_out[0]), x1, m_lat[5],
               rows_per_mod=l, alpha=alpha, tm=tm_ffn, tn=min(1024, d))

    out = _ffn(y2, lg[1], lb[1], m_lat[6], m_lat[7], m_lat[8], ffn2_wi[0], ffn2_wo[0],
               lg[2], lb[2], rows_per_mod=l, pre="ln_affine", alpha=alpha, tm=tm_ffn, tf=tf)
    return out.reshape(b, l, d)
```
